```python
import jax
import jax.numpy as jnp
from jax import lax
import numpy as np

D_MODEL = 1024
BATCH = 2
SEQ = 8192
DEPTH = 2
DEC_BATCH = 32
DEC_SEQ = 1
PAST_LEN = 16384
PAGE_SIZE = 128

N_MIXERS = 2
N_ATTN_LAYERS = (DEPTH + 1) // 2
N_CONV_LAYERS = DEPTH // 2
N_HEADS = 8
N_KV_HEADS = 2
GQA_GROUP = N_HEADS // N_KV_HEADS
HEAD_DIM = D_MODEL // N_HEADS
QKV_WIDTH = (N_HEADS + 2 * N_KV_HEADS) * HEAD_DIM
ROT_DIM = HEAD_DIM // 4
ROPE_THETA = 500000.0
ATTN_SCALE = HEAD_DIM ** -0.5
MOBA_BLOCK = 256
MOBA_TOPK = 3
Q_CHUNK = 32
CONV_WIDTH = 31
CONV_CHANNELS = D_MODEL
N_EXPERTS = 32
N_GROUPS = 4
EXPERTS_PER_GROUP = N_EXPERTS // N_GROUPS
TOP_K = 2
D_EXPERT = 512
MAX_ROWS = 512
MIN_ROWS = 8
DEEPNORM_ALPHA = (2 * DEPTH) ** 0.25
DEEPNORM_BETA = (8 * DEPTH) ** -0.25
LN_EPS = 1e-5

kernel_name = 'moba_conformer_moe_hybrid_step'


def layer_norm(x, g, b):
    xf = x.astype(jnp.float32)
    mu = jnp.mean(xf, axis=-1, keepdims=True)
    var = jnp.mean(jnp.square(xf - mu), axis=-1, keepdims=True)
    return ((xf - mu) * lax.rsqrt(var + LN_EPS)).astype(x.dtype) * g + b


def ada_modulation(c, w_ada, b_ada):
    m = jax.nn.silu(c) @ w_ada + b_ada
    return jnp.split(m[:, None, :], 6, axis=-1)


def modulate(x, shift, scale):
    return x * (1 + scale) + shift


def deepnorm_residual(x, out, gate, g, b):
    return layer_norm(DEEPNORM_ALPHA * x + gate * out, g, b)


def partial_rotary(x, pos):
    half = ROT_DIM // 2
    inv_freq = ROPE_THETA ** (-jnp.arange(half, dtype=jnp.float32) / half)
    ang = pos.astype(jnp.float32)[:, None] * inv_freq[None, :]
    cos = jnp.cos(ang)[:, None, :].astype(x.dtype)
    sin = jnp.sin(ang)[:, None, :].astype(x.dtype)
    x1, x2, x_pass = x[..., :half], x[..., half:ROT_DIM], x[..., ROT_DIM:]
    return jnp.concatenate([x1 * cos - x2 * sin, x2 * cos + x1 * sin, x_pass], axis=-1)


def qkv_project(h, w_qkv, pos):
    B, S, _ = h.shape
    qkv = h @ w_qkv
    nq, nk = N_HEADS * HEAD_DIM, N_KV_HEADS * HEAD_DIM
    q = qkv[..., :nq].reshape(B, S, N_HEADS, HEAD_DIM)
    k = qkv[..., nq:nq + nk].reshape(B, S, N_KV_HEADS, HEAD_DIM)
    v = qkv[..., nq + nk:].reshape(B, S, N_KV_HEADS, HEAD_DIM)
    return partial_rotary(q, pos), partial_rotary(k, pos), v


def softmax_over_parts(parts):
    p = jax.nn.softmax(jnp.concatenate(parts, axis=-1), axis=-1)
    cuts, acc = [], 0
    for t in parts[:-1]:
        acc += t.shape[-1]
        cuts.append(acc)
    return jnp.split(p, cuts, axis=-1)


def moba_prompt(q, k, v):
    B, S = q.shape[0], q.shape[1]
    n_blk = -(-S // MOBA_BLOCK)
    pad = ((0, 0), (0, n_blk * MOBA_BLOCK - S), (0, 0), (0, 0))
    kb = jnp.pad(k, pad).reshape(B, n_blk, MOBA_BLOCK, N_KV_HEADS, HEAD_DIM).transpose(0, 3, 1, 2, 4)
    vb = jnp.pad(v, pad).reshape(B, n_blk, MOBA_BLOCK, N_KV_HEADS, HEAD_DIM).transpose(0, 3, 1, 2, 4)
    kvh = jnp.arange(N_HEADS) // GQA_GROUP
    k_mean = jnp.mean(kb.astype(jnp.float32), axis=3)[:, kvh]
    topk = min(MOBA_TOPK, n_blk - 1)
    b_idx = jnp.arange(B)[:, None, None, None]
    h_kv = kvh[None, None, :, None]
    blk_ids = jnp.arange(n_blk)
    q_offs = jnp.arange(Q_CHUNK)
    blk_offs = jnp.arange(MOBA_BLOCK)

    def one_chunk(start):
        qc = lax.dynamic_slice_in_dim(q, start, Q_CHUNK, axis=1)
        qpos = start + q_offs
        own = start // MOBA_BLOCK
        k_own = lax.dynamic_index_in_dim(kb, own, axis=2, keepdims=False)
        v_own = lax.dynamic_index_in_dim(vb, own, axis=2, keepdims=False)
        kpos = own * MOBA_BLOCK + blk_offs
        qg = qc.reshape(B, Q_CHUNK, N_KV_HEADS, GQA_GROUP, HEAD_DIM)
        s_own = jnp.einsum('bqngd,bnkd->bqngk', qg, k_own).reshape(B, Q_CHUNK, N_HEADS, MOBA_BLOCK)
        causal = (kpos[None, :] <= qpos[:, None])[None, :, None, :]
        s_own = jnp.where(causal, s_own.astype(jnp.float32) * ATTN_SCALE, -jnp.inf)
        if topk == 0:
            p_own = jax.nn.softmax(s_own, axis=-1)
            out_sel = jnp.zeros(qc.shape, v.dtype)
        else:
            gate = jnp.einsum('bqhd,bhnd->bqhn', qc.astype(jnp.float32), k_mean)
            gate = jnp.where(blk_ids < own, gate, -jnp.inf)
            _, sel = lax.top_k(gate, topk)
            k_sel = kb[b_idx, h_kv, sel]
            v_sel = vb[b_idx, h_kv, sel]
            s_sel = jnp.einsum('bqhd,bqhjkd->bqhjk', qc, k_sel).astype(jnp.float32) * ATTN_SCALE
            s_sel = jnp.where((sel < own)[..., None], s_sel, -jnp.inf)
            s_sel = s_sel.reshape(B, Q_CHUNK, N_HEADS, topk * MOBA_BLOCK)
            p_own, p_sel = softmax_over_parts([s_own, s_sel])
            p_sel = p_sel.reshape(B, Q_CHUNK, N_HEADS, topk, MOBA_BLOCK).astype(v.dtype)
            out_sel = jnp.einsum('bqhjk,bqhjkd->bqhd', p_sel, v_sel)
        p_own = p_own.reshape(B, Q_CHUNK, N_KV_HEADS, GQA_GROUP, MOBA_BLOCK).astype(v.dtype)
        out_own = jnp.einsum('bqngk,bnkd->bqngd', p_own, v_own).reshape(B, Q_CHUNK, N_HEADS, HEAD_DIM)
        return out_own + out_sel

    starts = jnp.arange(S // Q_CHUNK) * Q_CHUNK
    outs = lax.map(one_chunk, starts)
    return outs.transpose(1, 0, 2, 3, 4).reshape(B, S, N_HEADS * HEAD_DIM)


def moba_sample(q, k_new, v_new, cache_k, cache_v, page_table, layer):
    DB, T = q.shape[0], q.shape[1]
    n_pages = PAST_LEN // PAGE_SIZE
    ppb = MOBA_BLOCK // PAGE_SIZE
    n_full = PAST_LEN // MOBA_BLOCK
    topk = min(MOBA_TOPK, n_full)
    kvh = jnp.arange(N_HEADS) // GQA_GROUP
    qg = q.reshape(DB, T, N_KV_HEADS, GQA_GROUP, HEAD_DIM)
    t_ids = jnp.arange(T)
    s_new = jnp.einsum('btngd,bsnd->btngs', qg, k_new).reshape(DB, T, N_HEADS, T)
    s_new = jnp.where((t_ids[None, :] <= t_ids[:, None])[None, :, None, :], s_new.astype(jnp.float32) * ATTN_SCALE, -jnp.inf)
    parts = [s_new]
    has_own_past = n_pages > n_full * ppb
    if has_own_past:
        own_pages = page_table[:, n_full * ppb:]
        k_op = cache_k[own_pages, layer].transpose(0, 2, 1, 3, 4).reshape(DB, N_KV_HEADS, -1, HEAD_DIM)
        v_op = cache_v[own_pages, layer].transpose(0, 2, 1, 3, 4).reshape(DB, N_KV_HEADS, -1, HEAD_DIM)
        s_op = jnp.einsum('btngd,bnkd->btngk', qg, k_op).reshape(DB, T, N_HEADS, -1)
        parts.append(s_op.astype(jnp.float32) * ATTN_SCALE)
    if topk > 0:
        full_pages = page_table[:, :n_full * ppb]
        k_full = cache_k[full_pages, layer].astype(jnp.float32)
        k_mean = jnp.mean(k_full.reshape(DB, n_full, ppb, N_KV_HEADS, PAGE_SIZE, HEAD_DIM), axis=(2, 4))[:, :, kvh]
        gate = jnp.einsum('bthd,bnhd->bthn', q.astype(jnp.float32), k_mean)
        _, sel = lax.top_k(gate, topk)
        sel_pages = sel[..., None] * ppb + jnp.arange(ppb)
        phys = page_table[jnp.arange(DB)[:, None, None, None, None], sel_pages]
        h_kv = kvh[None, None, :, None, None]
        k_sel = cache_k[phys, layer, h_kv].reshape(DB, T, N_HEADS, topk * MOBA_BLOCK, HEAD_DIM)
        v_sel = cache_v[phys, layer, h_kv].reshape(DB, T, N_HEADS, topk * MOBA_BLOCK, HEAD_DIM)
        s_sel = jnp.einsum('bthd,bthkd->bthk', q, k_sel)
        parts.append(s_sel.astype(jnp.float32) * ATTN_SCALE)
    probs = softmax_over_parts(parts)
    vd = v_new.dtype
    p_new = probs[0].reshape(DB, T, N_KV_HEADS, GQA_GROUP, T).astype(vd)
    out = jnp.einsum('btngs,bsnd->btngd', p_new, v_new).reshape(DB, T, N_HEADS, HEAD_DIM)
    idx = 1
    if has_own_past:
        p_op = probs[idx].reshape(DB, T, N_KV_HEADS, GQA_GROUP, -1).astype(vd)
        out = out + jnp.einsum('btngk,bnkd->btngd', p_op, v_op).reshape(DB, T, N_HEADS, HEAD_DIM)
        idx += 1
    if topk > 0:
        out = out + jnp.einsum('bthk,bthkd->bthd', probs[idx].astype(vd), v_sel)
    return out.reshape(DB, T, N_HEADS * HEAD_DIM)


def glu_in(h, w_in):
    a, g = jnp.split(h @ w_in, 2, axis=-1)
    return a * jax.nn.sigmoid(g)


def conv_module_tail(u_ext, w_dw, g, b, w_out):
    y = lax.conv_general_dilated(u_ext, w_dw[:, None, :].astype(u_ext.dtype), (1,), 'VALID',
                                 dimension_numbers=('NWC', 'WIO', 'NWC'), feature_group_count=CONV_CHANNELS)
    return jax.nn.silu(layer_norm(y, g, b)) @ w_out


def grouped_moe(h, w_router, b_router, w_gate, w_up, w_down):
    N, D = h.shape
    scores = jax.nn.sigmoid((h @ w_router).astype(jnp.float32))
    biased = (scores + b_router.astype(jnp.float32)).reshape(N, N_GROUPS, EXPERTS_PER_GROUP)
    group_score = jnp.sum(lax.top_k(biased, 2)[0], axis=-1)
    g_sel = jnp.argmax(group_score, axis=-1).astype(jnp.int32)
    in_group = jnp.take_along_axis(biased, g_sel[:, None, None], axis=1)[:, 0]
    _, local = lax.top_k(in_group, TOP_K)
    e_idx = g_sel[:, None] * EXPERTS_PER_GROUP + local
    wts = jnp.take_along_axis(scores, e_idx, axis=1)
    wts = wts / jnp.sum(wts, axis=-1, keepdims=True)
    n_assign = N * TOP_K
    per_exp = -(-n_assign // N_EXPERTS)
    rows = min(MAX_ROWS, max(MIN_ROWS, 1 << (per_exp - 1).bit_length()))
    n_blocks = -(-(n_assign + N_EXPERTS * (rows - 1)) // rows)
    flat_e = e_idx.reshape(-1)
    flat_tok = jnp.arange(n_assign, dtype=jnp.int32) // TOP_K
    order = jnp.argsort(flat_e)
    sorted_e = flat_e[order]
    counts = jnp.zeros((N_EXPERTS,), jnp.int32).at[flat_e].add(1)
    padded = (counts + rows - 1) // rows * rows
    start = jnp.cumsum(counts) - counts
    pend = jnp.cumsum(padded)
    pstart = pend - padded
    dest_sorted = pstart[sorted_e] + jnp.arange(n_assign, dtype=jnp.int32) - start[sorted_e]
    row_tok = jnp.zeros((n_blocks * rows,), jnp.int32).at[dest_sorted].set(flat_tok[order])
    blk_e = jnp.minimum(jnp.searchsorted(pend, jnp.arange(n_blocks, dtype=jnp.int32) * rows, side='right'), N_EXPERTS - 1)
    xs = h[row_tok].reshape(n_blocks, rows, D)

    def expert_rows(args):
        xb, e = args
        return (jax.nn.silu(xb @ w_gate[e]) * (xb @ w_up[e])) @ w_down[e]

    ys = lax.map(expert_rows, (xs, blk_e)).reshape(n_blocks * rows, D)
    dest = jnp.zeros((n_assign,), jnp.int32).at[order].set(dest_sorted)
    y = ys[dest].reshape(N, TOP_K, D)
    return jnp.einsum('nk,nkd->nd', wts.astype(y.dtype), y)


def setup_inputs(seed: int = 0) -> dict:
    key = jax.random.key(seed)
    ks = jax.random.split(key, 24)
    f32 = jnp.float32
    n_pages = PAST_LEN // PAGE_SIZE
    n_phys = (DEC_BATCH * n_pages * 5) // 4

    def nrm(k, shape, s):
        return jax.random.normal(k, shape, f32) * s

    x_prompt = nrm(ks[0], (BATCH, SEQ, D_MODEL), 1.0)
    x_sample = nrm(ks[1], (DEC_BATCH, DEC_SEQ, D_MODEL), 1.0)
    cache_k = nrm(ks[2], (n_phys, N_ATTN_LAYERS, N_KV_HEADS, PAGE_SIZE, HEAD_DIM), 1.0)
    cache_v = nrm(ks[3], (n_phys, N_ATTN_LAYERS, N_KV_HEADS, PAGE_SIZE, HEAD_DIM), 1.0)
    state_conv = nrm(ks[4], (N_CONV_LAYERS, DEC_BATCH, CONV_WIDTH - 1, CONV_CHANNELS), 0.5)
    page_table = jax.random.permutation(ks[5], n_phys)[:DEC_BATCH * n_pages].reshape(DEC_BATCH, n_pages).astype(jnp.int32)
    c_prompt = nrm(ks[6], (BATCH, D_MODEL), 1.0)
    c_sample = nrm(ks[7], (DEC_BATCH, D_MODEL), 1.0)
    w_ada = nrm(ks[8], (DEPTH, D_MODEL, 6 * D_MODEL), 0.5 * D_MODEL ** -0.5)
    b_ada = nrm(ks[9], (DEPTH, 6 * D_MODEL), 0.01)
    ln_g = 1.0 + nrm(ks[10], (DEPTH, 2, D_MODEL), 0.01)
    ln_b = nrm(ks[11], (DEPTH, 2, D_MODEL), 0.01)
    w_qkv = nrm(ks[12], (N_ATTN_LAYERS, D_MODEL, QKV_WIDTH), D_MODEL ** -0.5)
    w_o = nrm(ks[13], (N_ATTN_LAYERS, N_HEADS * HEAD_DIM, D_MODEL), (N_HEADS * HEAD_DIM) ** -0.5 * DEEPNORM_BETA)
    conv_w_in = nrm(ks[14], (N_CONV_LAYERS, D_MODEL, 2 * CONV_CHANNELS), D_MODEL ** -0.5)
    conv_w_dw = nrm(ks[15], (N_CONV_LAYERS, CONV_WIDTH, CONV_CHANNELS), CONV_WIDTH ** -0.5)
    conv_ln_g = 1.0 + nrm(ks[16], (N_CONV_LAYERS, CONV_CHANNELS), 0.01)
    conv_ln_b = nrm(ks[17], (N_CONV_LAYERS, CONV_CHANNELS), 0.01)
    conv_w_out = nrm(ks[18], (N_CONV_LAYERS, CONV_CHANNELS, D_MODEL), CONV_CHANNELS ** -0.5 * DEEPNORM_BETA)
    w_router = nrm(ks[19], (D_MODEL, N_EXPERTS), D_MODEL ** -0.5)
    b_router = nrm(ks[20], (N_EXPERTS,), 0.01)
    w_gate = nrm(ks[21], (DEPTH, N_EXPERTS, D_MODEL, D_EXPERT), D_MODEL ** -0.5)
    w_up = nrm(ks[22], (DEPTH, N_EXPERTS, D_MODEL, D_EXPERT), D_MODEL ** -0.5)
    w_down = nrm(ks[23], (DEPTH, N_EXPERTS, D_EXPERT, D_MODEL), D_EXPERT ** -0.5 * DEEPNORM_BETA)
    return {'x_prompt': x_prompt, 'x_sample': x_sample, 'cache_k': cache_k, 'cache_v': cache_v,
            'state_conv': state_conv, 'page_table': page_table, 'c_prompt': c_prompt, 'c_sample': c_sample,
            'w_ada': w_ada, 'b_ada': b_ada, 'ln_g': ln_g, 'ln_b': ln_b, 'w_qkv': w_qkv, 'w_o': w_o,
            'conv_w_in': conv_w_in, 'conv_w_dw': conv_w_dw, 'conv_ln_g': conv_ln_g, 'conv_ln_b': conv_ln_b,
            'conv_w_out': conv_w_out, 'w_router': w_router, 'b_router': b_router,
            'w_gate': w_gate, 'w_up': w_up, 'w_down': w_down}


def reference(x_prompt, x_sample, cache_k, cache_v, state_conv, page_table, c_prompt, c_sample,
              w_ada, b_ada, ln_g, ln_b, w_qkv, w_o, conv_w_in, conv_w_dw, conv_ln_g, conv_ln_b,
              conv_w_out, w_router, b_router, w_gate, w_up, w_down):
    pos_p = jnp.arange(SEQ)
    pos_s = PAST_LEN + jnp.arange(DEC_SEQ)
    xp, xs = x_prompt, x_sample
    kp_pages, vp_pages, ks_rows, vs_rows, conv_p, conv_s = [], [], [], [], [], []
    for i in range(DEPTH):
        mp = ada_modulation(c_prompt, w_ada[i], b_ada[i])
        ms = ada_modulation(c_sample, w_ada[i], b_ada[i])
        hp = modulate(xp, mp[0], mp[1])
        hs = modulate(xs, ms[0], ms[1])
        if i % N_MIXERS == 0:
            ia = i // N_MIXERS
            qp, kp, vp = qkv_project(hp, w_qkv[ia], pos_p)
            op = moba_prompt(qp, kp, vp) @ w_o[ia]
            qs, ks_, vs_ = qkv_project(hs, w_qkv[ia], pos_s)
            os_ = moba_sample(qs, ks_, vs_, cache_k, cache_v, page_table, ia) @ w_o[ia]
            npp = SEQ // PAGE_SIZE
            kp_pages.append(kp.reshape(BATCH, npp, PAGE_SIZE, N_KV_HEADS, HEAD_DIM).transpose(0, 1, 3, 2, 4))
            vp_pages.append(vp.reshape(BATCH, npp, PAGE_SIZE, N_KV_HEADS, HEAD_DIM).transpose(0, 1, 3, 2, 4))
            ks_rows.append(ks_.transpose(0, 2, 1, 3))
            vs_rows.append(vs_.transpose(0, 2, 1, 3))
        else:
            ic = i // N_MIXERS
            up_ext = jnp.pad(glu_in(hp, conv_w_in[ic]), ((0, 0), (CONV_WIDTH - 1, 0), (0, 0)))
            op = conv_module_tail(up_ext, conv_w_dw[ic], conv_ln_g[ic], conv_ln_b[ic], conv_w_out[ic])
            us = glu_in(hs, conv_w_in[ic])
            us_ext = jnp.concatenate([state_conv[ic].astype(us.dtype), us], axis=1)
            os_ = conv_module_tail(us_ext, conv_w_dw[ic], conv_ln_g[ic], conv_ln_b[ic], conv_w_out[ic])
            conv_p.append(up_ext[:, -(CONV_WIDTH - 1):])
            conv_s.append(us_ext[:, -(CONV_WIDTH - 1):])
        xp = deepnorm_residual(xp, op, mp[2], ln_g[i, 0], ln_b[i, 0])
        xs = deepnorm_residual(xs, os_, ms[2], ln_g[i, 0], ln_b[i, 0])
        hp = modulate(xp, mp[3], mp[4])
        hs = modulate(xs, ms[3], ms[4])
        fp = grouped_moe(hp.reshape(-1, D_MODEL), w_router, b_router, w_gate[i], w_up[i], w_down[i]).reshape(hp.shape)
        fs = grouped_moe(hs.reshape(-1, D_MODEL), w_router, b_router, w_gate[i], w_up[i], w_down[i]).reshape(hs.shape)
        xp = deepnorm_residual(xp, fp, mp[5], ln_g[i, 1], ln_b[i, 1])
        xs = deepnorm_residual(xs, fs, ms[5], ln_g[i, 1], ln_b[i, 1])
    k_prompt = jnp.stack(kp_pages, axis=2)
    v_prompt = jnp.stack(vp_pages, axis=2)
    conv_prompt = jnp.stack(conv_p, axis=0)
    k_sample = jnp.stack(ks_rows, axis=1)
    v_sample = jnp.stack(vs_rows, axis=1)
    conv_sample = jnp.stack(conv_s, axis=0)
    return (xp, xs, k_prompt, v_prompt, conv_prompt, k_sample, v_sample, conv_sample)
```

```python
import functools
import math

import jax
import jax.numpy as jnp
from jax import lax
from jax.experimental import pallas as pl
from jax.experimental.pallas import tpu as pltpu

F32 = jnp.float32
I32 = jnp.int32
MXU_DTYPE = jnp.bfloat16

N_HEADS = 8
N_KV_HEADS = 2
GQA_GROUP = N_HEADS // N_KV_HEADS
HEAD_DIM = 128
ROT_DIM = HEAD_DIM // 4
ROT_HALF = ROT_DIM // 2
ROPE_THETA = 500000.0
ATTN_SCALE = HEAD_DIM ** -0.5
MOBA_BLOCK = 256
MOBA_TOPK = 3
PAGE_SIZE = 128
PAGES_PER_BLOCK = MOBA_BLOCK // PAGE_SIZE
CONV_WIDTH = 31
CONV_HALO = 32
N_EXPERTS = 32
N_GROUPS = 4
EXPERTS_PER_GROUP = N_EXPERTS // N_GROUPS
TOP_K = 2
LN_EPS = 1e-5
MASK_VALUE = -1e30
LANES = 128
VMEM_LIMIT = 56 * 1024 * 1024

TM = 512
TM_EXPERT = 256
TM_COMBINE = 256
KMEAN_PAGES = 16


def _params(*sem):
    return pltpu.CompilerParams(dimension_semantics=sem, vmem_limit_bytes=VMEM_LIMIT)


def _dot(a, b):
    return jnp.dot(a.astype(MXU_DTYPE), b.astype(MXU_DTYPE), preferred_element_type=F32)


def _dot_nt(a, b):
    return lax.dot_general(a.astype(MXU_DTYPE), b.astype(MXU_DTYPE), (((1,), (1,)), ((), ())),
                           preferred_element_type=F32)


def _split(a):
    hi = a.astype(MXU_DTYPE)
    lo = (a - hi.astype(F32)).astype(MXU_DTYPE)
    return hi, lo


def _dot3(a, b):
    a_hi, a_lo = _split(a)
    b_hi, b_lo = _split(b)
    d = functools.partial(jnp.dot, preferred_element_type=F32)
    return d(a_hi, b_hi) + (d(a_hi, b_lo) + d(a_lo, b_hi))


def _dot3_nt(a, b):
    a_hi, a_lo = _split(a)
    b_hi, b_lo = _split(b)
    d = functools.partial(lax.dot_general, dimension_numbers=(((1,), (1,)), ((), ())),
                          preferred_element_type=F32)
    return d(a_hi, b_hi) + (d(a_hi, b_lo) + d(a_lo, b_hi))


def _layer_norm(z, g, b):
    mu = jnp.mean(z, axis=-1, keepdims=True)
    zc = z - mu
    var = jnp.mean(zc * zc, axis=-1, keepdims=True)
    return zc * lax.rsqrt(var + LN_EPS) * g + b


def _silu(x):
    return x * jax.nn.sigmoid(x)


def _rotary(xc, cos, sin, lane):
    x_up = pltpu.roll(xc, LANES - ROT_HALF, axis=1)
    x_dn = pltpu.roll(xc, ROT_HALF, axis=1)
    first = xc * cos - x_up * sin
    second = xc * cos + x_dn * sin
    return jnp.where(lane < ROT_HALF, first, jnp.where(lane < ROT_DIM, second, xc))


def _ada_body(c_ref, w_ref, b_ref, o_ref):
    o_ref[0] = _dot3(_silu(c_ref[...]), w_ref[0]) + b_ref[0]


def _ada(c_all, w_ada, b_ada):
    depth, d, n6 = w_ada.shape
    rows = c_all.shape[0]
    tn = 1536
    return pl.pallas_call(
        _ada_body,
        grid=(depth, n6 // tn),
        in_specs=[pl.BlockSpec((rows, d), lambda i, j: (0, 0)),
                  pl.BlockSpec((1, d, tn), lambda i, j: (i, 0, j)),
                  pl.BlockSpec((1, 1, tn), lambda i, j: (i, 0, j))],
        out_specs=pl.BlockSpec((1, rows, tn), lambda i, j: (i, 0, j)),
        out_shape=jax.ShapeDtypeStruct((depth, rows, n6), F32),
        compiler_params=_params("arbitrary", "arbitrary"),
        name="ada",
    )(c_all, w_ada, b_ada.reshape(depth, 1, n6))


def _qkv_prompt_body(x_ref, sh_ref, sc_ref, w_ref, cos_ref, sin_ref,
                     q_ref, kp_ref, vp_ref, kx_ref, v16_ref, km_ref, w16, *, tiles_per_seq):
    t = pl.program_id(0)
    tm = x_ref.shape[0]

    @pl.when(t == 0)
    def _():
        w16[...] = w_ref[...].astype(MXU_DTYPE)

    h = x_ref[...] * (1 + sc_ref[0]) + sh_ref[0]
    qkv = jnp.dot(h.astype(MXU_DTYPE), w16[...], preferred_element_type=F32)
    cos = cos_ref[...]
    sin = sin_ref[...]
    lane = lax.broadcasted_iota(I32, (tm, LANES), 1)
    nq = N_HEADS * HEAD_DIM
    nk = N_KV_HEADS * HEAD_DIM
    for hh in range(N_HEADS):
        sl = slice(hh * HEAD_DIM, (hh + 1) * HEAD_DIM)
        q_ref[:, sl] = _rotary(qkv[:, sl], cos, sin, lane).astype(q_ref.dtype)
    row = lax.broadcasted_iota(I32, (tm, LANES), 0)
    blk = ((t % tiles_per_seq) * tm + row) // MOBA_BLOCK
    onehot = jnp.where(lane == blk, 1.0, 0.0).astype(kx_ref.dtype)
    for n in range(N_KV_HEADS):
        kc = _rotary(qkv[:, nq + n * HEAD_DIM:nq + (n + 1) * HEAD_DIM], cos, sin, lane)
        vc = qkv[:, nq + nk + n * HEAD_DIM:nq + nk + (n + 1) * HEAD_DIM]
        kp_ref[0, :, 0, n] = kc.reshape(tm // PAGE_SIZE, PAGE_SIZE, HEAD_DIM)
        vp_ref[0, :, 0, n] = vc.reshape(tm // PAGE_SIZE, PAGE_SIZE, HEAD_DIM)
        kx_ref[0, n, :, 0:HEAD_DIM] = kc.astype(kx_ref.dtype)
        kx_ref[0, n, :, HEAD_DIM:2 * HEAD_DIM] = onehot
        v16_ref[0, n] = vc.astype(v16_ref.dtype)
        km_ref[0, 0, n] = jnp.sum(kc.reshape(tm // MOBA_BLOCK, MOBA_BLOCK, HEAD_DIM), axis=1) * (1.0 / MOBA_BLOCK)


def _qkv_prompt(x, shift, scale, w_qkv, cos_t, sin_t, batch, seq):
    n, d = x.shape
    tm = TM
    tps = seq // tm
    width = w_qkv.shape[1]
    npg = seq // PAGE_SIZE
    mod_spec = pl.BlockSpec((1, 1, d), lambda t: (t // tps, 0, 0))
    rot_spec = pl.BlockSpec((tm, LANES), lambda t: (t % tps, 0))
    page_spec = pl.BlockSpec((1, tm // PAGE_SIZE, 1, N_KV_HEADS, PAGE_SIZE, HEAD_DIM),
                             lambda t: (t // tps, t % tps, 0, 0, 0, 0))
    page_shape = jax.ShapeDtypeStruct((batch, npg, 1, N_KV_HEADS, PAGE_SIZE, HEAD_DIM), F32)
    return pl.pallas_call(
        functools.partial(_qkv_prompt_body, tiles_per_seq=tps),
        grid=(n // tm,),
        in_specs=[pl.BlockSpec((tm, d), lambda t: (t, 0)), mod_spec, mod_spec,
                  pl.BlockSpec((d, width), lambda t: (0, 0)), rot_spec, rot_spec],
        out_specs=[pl.BlockSpec((tm, N_HEADS * HEAD_DIM), lambda t: (t, 0)),
                   page_spec, page_spec,
                   pl.BlockSpec((1, N_KV_HEADS, tm, 2 * HEAD_DIM), lambda t: (t // tps, 0, t % tps, 0)),
                   pl.BlockSpec((1, N_KV_HEADS, tm, HEAD_DIM), lambda t: (t // tps, 0, t % tps, 0)),
                   pl.BlockSpec((1, 1, N_KV_HEADS, tm // MOBA_BLOCK, HEAD_DIM), lambda t: (t // tps, t % tps, 0, 0, 0))],
        out_shape=[jax.ShapeDtypeStruct((n, N_HEADS * HEAD_DIM), MXU_DTYPE),
                   page_shape, page_shape,
                   jax.ShapeDtypeStruct((batch, N_KV_HEADS, seq, 2 * HEAD_DIM), MXU_DTYPE),
                   jax.ShapeDtypeStruct((batch, N_KV_HEADS, seq, HEAD_DIM), MXU_DTYPE),
                   jax.ShapeDtypeStruct((batch, tps, N_KV_HEADS, tm // MOBA_BLOCK, HEAD_DIM), F32)],
        scratch_shapes=[pltpu.VMEM((d, width), MXU_DTYPE)],
        compiler_params=_params("arbitrary"),
        name="qkv_prompt",
    )(x, shift, scale, w_qkv, cos_t, sin_t)


def _qkv_sample_body(x_ref, sh_ref, sc_ref, w_ref, cos_ref, sin_ref, o_ref):
    rows = x_ref.shape[0]
    h = x_ref[...] * (1 + sc_ref[...]) + sh_ref[...]
    qkv = _dot3(h, w_ref[...])
    lane = lax.broadcasted_iota(I32, (rows, LANES), 1)
    n_rot = N_HEADS + N_KV_HEADS
    for c in range(n_rot):
        sl = slice(c * HEAD_DIM, (c + 1) * HEAD_DIM)
        o_ref[:, sl] = _rotary(qkv[:, sl], cos_ref[...], sin_ref[...], lane)
    o_ref[:, n_rot * HEAD_DIM:] = qkv[:, n_rot * HEAD_DIM:]


def _qkv_sample(x, shift, scale, w_qkv, cos_t, sin_t):
    rows, d = x.shape
    width = w_qkv.shape[1]
    full = lambda shape: pl.BlockSpec(shape, lambda: tuple(0 for _ in shape))
    return pl.pallas_call(
        _qkv_sample_body,
        in_specs=[full((rows, d)), full((rows, d)), full((rows, d)), full((d, width)),
                  full((rows, LANES)), full((rows, LANES))],
        out_specs=full((rows, width)),
        out_shape=jax.ShapeDtypeStruct((rows, width), F32),
        compiler_params=pltpu.CompilerParams(vmem_limit_bytes=VMEM_LIMIT),
        name="qkv_sample",
    )(x, shift, scale, w_qkv, cos_t, sin_t)


def _moba_prompt_body(q_ref, kx_ref, v_ref, km_ref, o_ref, qx_ref, m_ref, l_ref, acc_ref):
    i = pl.program_id(2)
    rows = GQA_GROUP * MOBA_BLOCK
    n_blk = km_ref.shape[2]
    for h in range(GQA_GROUP):
        qx_ref[h * MOBA_BLOCK:(h + 1) * MOBA_BLOCK, 0:HEAD_DIM] = q_ref[:, h * HEAD_DIM:(h + 1) * HEAD_DIM]

    gate = _dot_nt(km_ref[0, 0], qx_ref[:, 0:HEAD_DIM])
    blk = lax.broadcasted_iota(I32, (n_blk, rows), 0)
    valid = blk < i
    cand = jnp.where(valid, gate, -jnp.inf)
    sel = blk == i
    for _ in range(MOBA_TOPK):
        top = jnp.max(cand, axis=0, keepdims=True)
        idx = jnp.min(jnp.where(cand == top, blk, n_blk), axis=0, keepdims=True)
        pick = blk == idx
        sel = sel | (pick & valid)
        cand = jnp.where(pick, -jnp.inf, cand)
    bias = jnp.where(sel, 0.0, MASK_VALUE)
    if n_blk < LANES:
        bias = jnp.concatenate([bias, jnp.zeros((LANES - n_blk, rows), F32)], axis=0)
    qx_ref[:, HEAD_DIM:2 * HEAD_DIM] = bias.T.astype(qx_ref.dtype)

    def scores(j):
        start = pl.multiple_of(j * MOBA_BLOCK, MOBA_BLOCK)
        s = _dot_nt(qx_ref[...], kx_ref[0, 0, pl.ds(start, MOBA_BLOCK), :]) * ATTN_SCALE
        return s, v_ref[0, 0, pl.ds(start, MOBA_BLOCK), :]

    s, v = scores(i)
    qpos = lax.broadcasted_iota(I32, (rows, MOBA_BLOCK), 0) % MOBA_BLOCK
    kpos = lax.broadcasted_iota(I32, (rows, MOBA_BLOCK), 1)
    s = jnp.where(kpos <= qpos, s, MASK_VALUE)
    m0 = jnp.max(s, axis=1, keepdims=True)
    p = jnp.exp(s - m0)
    m_ref[...] = m0
    l_ref[...] = jnp.sum(p, axis=1, keepdims=True)
    acc_ref[...] = _dot(p, v)

    def past(j, carry):
        s, v = scores(j)
        m_old = m_ref[...]
        m_new = jnp.maximum(m_old, jnp.max(s, axis=1, keepdims=True))
        alpha = jnp.exp(m_old - m_new)
        p = jnp.exp(s - m_new)
        l_ref[...] = alpha * l_ref[...] + jnp.sum(p, axis=1, keepdims=True)
        acc_ref[...] = alpha * acc_ref[...] + _dot(p, v)
        m_ref[...] = m_new
        return carry

    lax.fori_loop(0, i, past, 0)
    out = acc_ref[...] / l_ref[...]
    for h in range(GQA_GROUP):
        o_ref[:, h * HEAD_DIM:(h + 1) * HEAD_DIM] = out[h * MOBA_BLOCK:(h + 1) * MOBA_BLOCK].astype(o_ref.dtype)


def _moba_prompt(q, kx, v16, k_mean, batch, seq):
    n = q.shape[0]
    nq = seq // MOBA_BLOCK
    rows = GQA_GROUP * MOBA_BLOCK
    gw = GQA_GROUP * HEAD_DIM
    return pl.pallas_call(
        _moba_prompt_body,
        grid=(batch, N_KV_HEADS, nq),
        in_specs=[pl.BlockSpec((MOBA_BLOCK, gw), lambda b, g, i: (b * nq + i, g)),
                  pl.BlockSpec((1, 1, seq, 2 * HEAD_DIM), lambda b, g, i: (b, g, 0, 0)),
                  pl.BlockSpec((1, 1, seq, HEAD_DIM), lambda b, g, i: (b, g, 0, 0)),
                  pl.BlockSpec((1, 1, nq, HEAD_DIM), lambda b, g, i: (b, g, 0, 0))],
        out_specs=pl.BlockSpec((MOBA_BLOCK, gw), lambda b, g, i: (b * nq + i, g)),
        out_shape=jax.ShapeDtypeStruct((n, N_HEADS * HEAD_DIM), MXU_DTYPE),
        scratch_shapes=[pltpu.VMEM((rows, 2 * HEAD_DIM), MXU_DTYPE),
                        pltpu.VMEM((rows, 1), F32), pltpu.VMEM((rows, 1), F32),
                        pltpu.VMEM((rows, HEAD_DIM), F32)],
        compiler_params=_params("arbitrary", "arbitrary", "arbitrary"),
        name="moba_prompt",
    )(q, kx, v16, k_mean)


def _kmean_copy(pt_ref, ck_hbm, buf, sem, layer, step, slot, p):
    phys = pt_ref[step * KMEAN_PAGES + p]
    return pltpu.make_async_copy(ck_hbm.at[phys, layer], buf.at[slot, p], sem.at[slot])


def _kmean_sample_body(pt_ref, ck_hbm, o_ref, buf, sem, *, layer):
    nc = pl.num_programs(1)
    step = pl.program_id(0) * nc + pl.program_id(1)
    total = pl.num_programs(0) * nc
    slot = step % 2

    def start(step_, slot_):
        for p in range(KMEAN_PAGES):
            _kmean_copy(pt_ref, ck_hbm, buf, sem, layer, step_, slot_, p).start()

    @pl.when(step == 0)
    def _():
        start(step, slot)

    @pl.when(step + 1 < total)
    def _():
        start(step + 1, 1 - slot)

    for p in range(KMEAN_PAGES):
        _kmean_copy(pt_ref, ck_hbm, buf, sem, layer, step, slot, p).wait()
    for jb in range(KMEAN_PAGES // PAGES_PER_BLOCK):
        for n in range(N_KV_HEADS):
            acc = jnp.zeros((1, HEAD_DIM), F32)
            for pp in range(PAGES_PER_BLOCK):
                acc = acc + jnp.sum(buf[slot, jb * PAGES_PER_BLOCK + pp, n], axis=0, keepdims=True)
            o_ref[0, jb, pl.ds(n, 1), :] = acc * (1.0 / MOBA_BLOCK)


def _kmean_sample(page_table, cache_k, layer):
    db, n_pages = page_table.shape
    n_full = n_pages // PAGES_PER_BLOCK
    page_shape = cache_k.shape[2:]
    bps = KMEAN_PAGES // PAGES_PER_BLOCK
    return pl.pallas_call(
        functools.partial(_kmean_sample_body, layer=layer),
        grid_spec=pltpu.PrefetchScalarGridSpec(
            num_scalar_prefetch=1,
            grid=(db, n_pages // KMEAN_PAGES),
            in_specs=[pl.BlockSpec(memory_space=pl.ANY)],
            out_specs=pl.BlockSpec((1, bps, N_KV_HEADS, HEAD_DIM), lambda b, c, pt: (b, c, 0, 0)),
            scratch_shapes=[pltpu.VMEM((2, KMEAN_PAGES) + page_shape, F32),
                            pltpu.SemaphoreType.DMA((2,))]),
        out_shape=jax.ShapeDtypeStruct((db, n_full, N_KV_HEADS, HEAD_DIM), F32),
        compiler_params=_params("arbitrary", "arbitrary"),
        name="kmean_sample",
    )(page_table.reshape(-1), cache_k)


def _attn_sample_body(pages_ref, q_ref, kn_ref, vn_ref, *refs):
    n_sel = (len(refs) - 1) // 2
    k_refs, v_refs, o_ref = refs[:n_sel], refs[n_sel:2 * n_sel], refs[-1]
    q = q_ref[0]
    q8 = jnp.broadcast_to(q, (8, HEAD_DIM))
    s_new = jnp.sum(q * kn_ref[0], axis=1, keepdims=True) * ATTN_SCALE
    s_sel = [_dot3_nt(q8, k_ref[0, 0, 0])[0:1] * ATTN_SCALE for k_ref in k_refs]
    m = s_new
    for s in s_sel:
        m = jnp.maximum(m, jnp.max(s, axis=1, keepdims=True))
    p_new = jnp.exp(s_new - m)
    denom = p_new
    out = p_new * vn_ref[0]
    for s, v_ref in zip(s_sel, v_refs):
        p = jnp.exp(s - m)
        denom = denom + jnp.sum(p, axis=1, keepdims=True)
        out = out + _dot3(jnp.broadcast_to(p, (8, PAGE_SIZE)), v_ref[0, 0, 0])[0:1]
    o_ref[0] = out / denom


def _attn_sample(phys, q, k_new, v_new, cache_k, cache_v, layer):
    db = q.shape[0] // N_HEADS
    n_sel = phys.shape[0] // (db * N_HEADS)
    vec = lambda f: pl.BlockSpec((1, 1, HEAD_DIM), f)

    def page_spec(p):
        return pl.BlockSpec((1, 1, 1, PAGE_SIZE, HEAD_DIM),
                            lambda b, h, pg: (pg[(b * N_HEADS + h) * n_sel + p], layer, h // GQA_GROUP, 0, 0))

    page_specs = [page_spec(p) for p in range(n_sel)]
    return pl.pallas_call(
        _attn_sample_body,
        grid_spec=pltpu.PrefetchScalarGridSpec(
            num_scalar_prefetch=1,
            grid=(db, N_HEADS),
            in_specs=[vec(lambda b, h, pg: (b * N_HEADS + h, 0, 0)),
                      vec(lambda b, h, pg: (b * N_KV_HEADS + h // GQA_GROUP, 0, 0)),
                      vec(lambda b, h, pg: (b * N_KV_HEADS + h // GQA_GROUP, 0, 0))] + page_specs + page_specs,
            out_specs=vec(lambda b, h, pg: (b * N_HEADS + h, 0, 0))),
        out_shape=jax.ShapeDtypeStruct((db * N_HEADS, 1, HEAD_DIM), F32),
        compiler_params=_params("arbitrary", "arbitrary"),
        name="attn_sample",
    )(phys, q, k_new, v_new, *([cache_k] * n_sel), *([cache_v] * n_sel))


def _glu_body(x_ref, sh_ref, sc_ref, w_ref, o_ref, w16, *, three_pass):
    c = o_ref.shape[1]
    h = x_ref[...] * (1 + sc_ref[0]) + sh_ref[0]
    if three_pass:
        ag = _dot3(h, w_ref[...])
    else:
        @pl.when(pl.program_id(0) == 0)
        def _():
            w16[...] = w_ref[...].astype(MXU_DTYPE)

        ag = jnp.dot(h.astype(MXU_DTYPE), w16[...], preferred_element_type=F32)
    o_ref[...] = ag[:, :c] * jax.nn.sigmoid(ag[:, c:])


def _glu(x, shift, scale, w_in, tm, rows_per_mod, three_pass):
    n, d = x.shape
    c2 = w_in.shape[1]
    r = shift.shape[1]
    mod_spec = pl.BlockSpec((1, r, d), lambda t: (t // rows_per_mod, 0, 0))
    return pl.pallas_call(
        functools.partial(_glu_body, three_pass=three_pass),
        grid=(n // tm,),
        in_specs=[pl.BlockSpec((tm, d), lambda t: (t, 0)), mod_spec, mod_spec,
                  pl.BlockSpec((d, c2), lambda t: (0, 0))],
        out_specs=pl.BlockSpec((tm, c2 // 2), lambda t: (t, 0)),
        out_shape=jax.ShapeDtypeStruct((n, c2 // 2), F32),
        scratch_shapes=[pltpu.VMEM((d, c2) if not three_pass else (8, LANES), MXU_DTYPE)],
        compiler_params=_params("arbitrary"),
        name="glu",
    )(x, shift, scale, w_in)


def _post_body(*refs, n_tiles, **kw):
    n_mixer = 1 if kw["mode"] == "attn" else 5
    h2_ref, lg_ref = refs[n_mixer + 9], refs[n_mixer + 10]
    t = pl.program_id(0)

    @pl.when(t < n_tiles)
    def _():
        _post_tile(*refs, **kw)

    @pl.when(t >= n_tiles)
    def _():
        h2_ref[...] = jnp.zeros(h2_ref.shape, F32)
        lg_ref[...] = jnp.zeros(lg_ref.shape, F32)


def _post_tile(*refs, mode, three_pass, alpha, tiles_per_seq):
    if mode == "attn":
        a_ref, rest = refs[0], refs[1:]
    elif mode == "conv":
        u_ref, prev_ref, wdw_ref, cg_ref, cb_ref = refs[:5]
        rest = refs[5:]
    else:
        u_ref, st_ref, wdw_ref, cg_ref, cb_ref = refs[:5]
        rest = refs[5:]
    (w_ref, x_ref, g1_ref, lng_ref, lnb_ref, sh2_ref, sc2_ref, wr_ref,
     x1_ref, h2_ref, lg_ref, w16) = rest[:12]
    t = pl.program_id(0)
    tm = x_ref.shape[0]

    if mode == "attn":
        a = a_ref[...]
    else:
        if mode == "conv":
            ext_ref = rest[12]
            first = (t % tiles_per_seq) == 0
            ext_ref[0:CONV_HALO] = jnp.where(first, 0.0, prev_ref[...])
            ext_ref[CONV_HALO:] = u_ref[...]
            off = CONV_HALO - (CONV_WIDTH - 1)
            y = ext_ref[pl.ds(off, tm), :] * wdw_ref[0:1, :]
            for w in range(1, CONV_WIDTH):
                y = y + ext_ref[pl.ds(off + w, tm), :] * wdw_ref[w:w + 1, :]
        else:
            y = u_ref[...] * wdw_ref[CONV_WIDTH - 1:CONV_WIDTH, :]
            for w in range(CONV_WIDTH - 1):
                y = y + st_ref[w] * wdw_ref[w:w + 1, :]
        a = _silu(_layer_norm(y, cg_ref[...], cb_ref[...]))

    if three_pass:
        f = _dot3(a.astype(F32), w_ref[...])
    else:
        @pl.when(t == 0)
        def _():
            w16[...] = w_ref[...].astype(MXU_DTYPE)

        f = jnp.dot(a.astype(MXU_DTYPE), w16[...], preferred_element_type=F32)
    x1 = _layer_norm(alpha * x_ref[...] + g1_ref[0] * f, lng_ref[...], lnb_ref[...])
    h2 = x1 * (1 + sc2_ref[0]) + sh2_ref[0]
    x1_ref[...] = x1
    h2_ref[...] = h2
    lg_ref[...] = _dot3(h2, wr_ref[...])


def _post(mode, mixer_in, w, x, gate1, ln_g, ln_b, shift2, scale2, w_router, *, tm, rows_per_mod, three_pass,
          alpha, seq=None, out_rows=None, into=None):
    n, d = x.shape
    out_rows = n if out_rows is None else out_rows
    r = gate1.shape[1]
    tps = None if seq is None else seq // tm
    n_tiles = n // tm
    n_steps = -(-out_rows // tm)
    last = n_tiles - 1
    row = lambda t: (jnp.minimum(t, last), 0)
    const = lambda t: (0, 0)
    mod_spec = pl.BlockSpec((1, r, d), lambda t: (jnp.minimum(t, last) // rows_per_mod, 0, 0))
    vec_spec = pl.BlockSpec((1, d), const)
    if mode == "attn":
        mixer_specs = [pl.BlockSpec((tm, d), row)]
    elif mode == "conv":
        u, w_dw, cg, cb = mixer_in
        per = tm // CONV_HALO
        mixer_in = (u, u, w_dw, cg, cb)
        mixer_specs = [pl.BlockSpec((tm, d), row),
                       pl.BlockSpec((CONV_HALO, d), lambda t: (jnp.maximum(jnp.minimum(t, last) * per - 1, 0), 0)),
                       pl.BlockSpec(w_dw.shape, const), vec_spec, vec_spec]
    else:
        u, state, w_dw, cg, cb = mixer_in
        mixer_specs = [pl.BlockSpec((tm, d), row), pl.BlockSpec(state.shape, lambda t: (0, 0, 0)),
                       pl.BlockSpec(w_dw.shape, const), vec_spec, vec_spec]
    if mode == "attn":
        mixer_in = (mixer_in,)
    in_specs = mixer_specs + [pl.BlockSpec(w.shape, const), pl.BlockSpec((tm, d), row), mod_spec, vec_spec, vec_spec,
                              mod_spec, mod_spec, pl.BlockSpec(w_router.shape, const)]
    args = list(mixer_in) + [w, x, gate1, ln_g, ln_b, shift2, scale2, w_router]
    out_shape = [jax.ShapeDtypeStruct((n, d), F32)]
    out_specs = [pl.BlockSpec((tm, d), row)]
    aliases = {}
    if into is None:
        out_shape += [jax.ShapeDtypeStruct((out_rows, d), F32), jax.ShapeDtypeStruct((out_rows, LANES), F32)]
        out_specs += [pl.BlockSpec((tm, d), lambda t: (t, 0)), pl.BlockSpec((tm, LANES), lambda t: (t, 0))]
    else:
        h2_all, lg_all, row_block = into
        out_shape += [jax.ShapeDtypeStruct(h2_all.shape, F32), jax.ShapeDtypeStruct(lg_all.shape, F32)]
        out_specs += [pl.BlockSpec((tm, d), lambda t: (row_block + t, 0)),
                      pl.BlockSpec((tm, LANES), lambda t: (row_block + t, 0))]
        aliases = {len(args): 1, len(args) + 1: 2}
        in_specs += [pl.BlockSpec(memory_space=pl.ANY), pl.BlockSpec(memory_space=pl.ANY)]
        args += [h2_all, lg_all]
    scratch = [pltpu.VMEM(w.shape if not three_pass else (8, LANES), MXU_DTYPE)]
    if mode == "conv":
        scratch.append(pltpu.VMEM((CONV_HALO + tm, d), F32))

    def body(*refs):
        if into is not None:
            n_in = len(args)
            refs = refs[:n_in - 2] + refs[n_in:]
        _post_body(*refs, n_tiles=n_tiles, mode=mode, three_pass=three_pass, alpha=alpha, tiles_per_seq=tps)

    return pl.pallas_call(
        body,
        grid=(n_steps,),
        in_specs=in_specs,
        out_specs=out_specs,
        out_shape=out_shape,
        scratch_shapes=scratch,
        input_output_aliases=aliases,
        compiler_params=_params("arbitrary"),
        name="post_" + mode,
    )(*args)


def _expert_body(tok_ref, te_ref, nu_ref, h_hbm, rw_ref, wg_ref, wu_ref, wd_ref, o_ref,
                 xbuf, wg16, wu16, wd16, sem):
    i = pl.program_id(0)
    tm = o_ref.shape[0]
    used = i < nu_ref[0]

    @pl.when(used)
    def _():
        def row_copy(r):
            return pltpu.make_async_copy(h_hbm.at[pl.ds(tok_ref[i * tm + r], 1)], xbuf.at[pl.ds(r, 1)], sem.at[0])

        def issue(r, c):
            row_copy(r).start()
            return c

        lax.fori_loop(0, tm, issue, 0)
        changed = jnp.logical_or(i == 0, te_ref[i] != te_ref[jnp.maximum(i - 1, 0)])

        @pl.when(changed)
        def _():
            wg16[...] = wg_ref[0].astype(MXU_DTYPE)
            wu16[...] = wu_ref[0].astype(MXU_DTYPE)
            wd16[...] = wd_ref[0].astype(MXU_DTYPE)

        def drain(r, c):
            row_copy(r).wait()
            return c

        lax.fori_loop(0, tm, drain, 0)
        x = xbuf[...].astype(MXU_DTYPE)
        g = jnp.dot(x, wg16[...], preferred_element_type=F32)
        u = jnp.dot(x, wu16[...], preferred_element_type=F32)
        y = jnp.dot((_silu(g) * u).astype(MXU_DTYPE), wd16[...], preferred_element_type=F32)
        o_ref[...] = y * rw_ref[...]

    @pl.when(jnp.logical_not(used))
    def _():
        o_ref[...] = jnp.zeros(o_ref.shape, F32)


def _experts(row_tok, tile_e, n_used, h_all, row_w, w_gate, w_up, w_down, layer):
    n_rows = row_tok.shape[0]
    tm = TM_EXPERT
    d = h_all.shape[1]
    de = w_gate.shape[-1]
    wspec_in = pl.BlockSpec((None, 1, d, de), lambda i, tok, te, nu: (layer, te[i], 0, 0))
    wspec_out = pl.BlockSpec((None, 1, de, d), lambda i, tok, te, nu: (layer, te[i], 0, 0))
    return pl.pallas_call(
        _expert_body,
        grid_spec=pltpu.PrefetchScalarGridSpec(
            num_scalar_prefetch=3,
            grid=(n_rows // tm,),
            in_specs=[pl.BlockSpec(memory_space=pl.ANY),
                      pl.BlockSpec((tm, 1), lambda i, tok, te, nu: (i, 0)),
                      wspec_in, wspec_in, wspec_out],
            out_specs=pl.BlockSpec((tm, d), lambda i, tok, te, nu: (i, 0)),
            scratch_shapes=[pltpu.VMEM((tm, d), F32),
                            pltpu.VMEM((d, de), MXU_DTYPE), pltpu.VMEM((d, de), MXU_DTYPE),
                            pltpu.VMEM((de, d), MXU_DTYPE), pltpu.SemaphoreType.DMA((1,))]),
        out_shape=jax.ShapeDtypeStruct((n_rows, d), F32),
        compiler_params=_params("arbitrary"),
        name="experts",
    )(row_tok, tile_e, n_used, h_all, row_w, w_gate, w_up, w_down)


def _combine_body(dest_ref, ys_hbm, x_ref, g_ref, lng_ref, lnb_ref, o_ref, buf, sem, *, alpha, tok0):
    t = pl.program_id(0)
    tm = x_ref.shape[0]

    def row_copy(r, k):
        src = dest_ref[(tok0 + t * tm + r) * TOP_K + k]
        return pltpu.make_async_copy(ys_hbm.at[pl.ds(src, 1)], buf.at[k, pl.ds(r, 1)], sem.at[k])

    def issue(r, c):
        for k in range(TOP_K):
            row_copy(r, k).start()
        return c

    def drain(r, c):
        for k in range(TOP_K):
            row_copy(r, k).wait()
        return c

    lax.fori_loop(0, tm, issue, 0)
    lax.fori_loop(0, tm, drain, 0)
    f = buf[0]
    for k in range(1, TOP_K):
        f = f + buf[k]
    o_ref[...] = _layer_norm(alpha * x_ref[...] + g_ref[0] * f, lng_ref[...], lnb_ref[...])


def _combine(dest, ys, x, gate2, ln_g, ln_b, *, tm, rows_per_mod, alpha, tok0):
    n, d = x.shape
    r = gate2.shape[1]
    vec_spec = pl.BlockSpec((1, d), lambda t, ds_: (0, 0))
    return pl.pallas_call(
        functools.partial(_combine_body, alpha=alpha, tok0=tok0),
        grid_spec=pltpu.PrefetchScalarGridSpec(
            num_scalar_prefetch=1,
            grid=(n // tm,),
            in_specs=[pl.BlockSpec(memory_space=pl.ANY),
                      pl.BlockSpec((tm, d), lambda t, ds_: (t, 0)),
                      pl.BlockSpec((1, r, d), lambda t, ds_: (t // rows_per_mod, 0, 0)),
                      vec_spec, vec_spec],
            out_specs=pl.BlockSpec((tm, d), lambda t, ds_: (t, 0)),
            scratch_shapes=[pltpu.VMEM((TOP_K, tm, d), F32), pltpu.SemaphoreType.DMA((TOP_K,))]),
        out_shape=jax.ShapeDtypeStruct((n, d), F32),
        compiler_params=_params("arbitrary"),
        name="combine",
    )(dest, ys, x, gate2, ln_g, ln_b)


def _route(logits, b_router, n_tok, n_rows):
    tm = TM_EXPERT
    scores = jax.nn.sigmoid(logits[:n_tok, :N_EXPERTS])
    biased = (scores + b_router.astype(F32)).reshape(n_tok, N_GROUPS, EXPERTS_PER_GROUP)
    group_score = jnp.sum(lax.top_k(biased, 2)[0], axis=-1)
    g_sel = jnp.argmax(group_score, axis=-1).astype(I32)
    in_group = jnp.take_along_axis(biased, g_sel[:, None, None], axis=1)[:, 0]
    _, local = lax.top_k(in_group, TOP_K)
    e_idx = g_sel[:, None] * EXPERTS_PER_GROUP + local
    wts = jnp.take_along_axis(scores, e_idx, axis=1)
    wts = wts / jnp.sum(wts, axis=-1, keepdims=True)
    flat_e = e_idx.reshape(-1)
    onehot = (flat_e[:, None] == jnp.arange(N_EXPERTS, dtype=I32)[None, :]).astype(I32)
    rank = jnp.take_along_axis(jnp.cumsum(onehot, axis=0), flat_e[:, None], axis=1)[:, 0] - 1
    counts = jnp.sum(onehot, axis=0)
    padded = (counts + tm - 1) // tm * tm
    pend = jnp.cumsum(padded)
    pstart = pend - padded
    dest = (pstart[flat_e] + rank).astype(I32)
    n_assign = n_tok * TOP_K
    flat_tok = jnp.arange(n_assign, dtype=I32) // TOP_K
    row_tok = jnp.zeros((n_rows,), I32).at[dest].set(flat_tok)
    row_w = jnp.zeros((n_rows,), F32).at[dest].set(wts.reshape(-1))
    n_tiles = n_rows // tm
    n_used = (pend[-1] // tm).astype(I32)
    tile_start = jnp.minimum(jnp.arange(n_tiles, dtype=I32), n_used - 1) * tm
    tile_e = jnp.minimum(jnp.searchsorted(pend, tile_start, side="right"), N_EXPERTS - 1).astype(I32)
    return row_tok, row_w[:, None], tile_e, n_used.reshape(1), dest


def _rotary_tables(pos):
    inv_freq = ROPE_THETA ** (-jnp.arange(ROT_HALF, dtype=F32) / ROT_HALF)
    ang = pos.astype(F32)[:, None] * inv_freq[None, :]
    ones = jnp.ones((pos.shape[0], LANES - ROT_DIM), F32)
    cos_t = jnp.concatenate([jnp.cos(ang), jnp.cos(ang), ones], axis=1)
    sin_t = jnp.concatenate([jnp.sin(ang), jnp.sin(ang), 0.0 * ones], axis=1)
    return cos_t, sin_t


def kernel(x_prompt, x_sample, cache_k, cache_v, state_conv, page_table, c_prompt, c_sample,
           w_ada, b_ada, ln_g, ln_b, w_qkv, w_o, conv_w_in, conv_w_dw, conv_ln_g, conv_ln_b,
           conv_w_out, w_router, b_router, w_gate, w_up, w_down):
    batch, seq, d = x_prompt.shape
    db, dec_seq, _ = x_sample.shape
    assert dec_seq == 1 and seq % TM == 0 and TM % MOBA_BLOCK == 0 and d == N_HEADS * HEAD_DIM
    depth = w_ada.shape[0]
    n_pages = page_table.shape[1]
    past_len = n_pages * PAGE_SIZE
    assert n_pages % KMEAN_PAGES == 0 and n_pages % PAGES_PER_BLOCK == 0
    n_full = n_pages // PAGES_PER_BLOCK
    topk_s = min(MOBA_TOPK, n_full)
    alpha = (2 * depth) ** 0.25
    n_p = batch * seq
    n_all = n_p + db
    assert n_p % db == 0
    tps = seq // TM

    c_rows = -(-(batch + db) // 8) * 8
    c_all = jnp.concatenate([c_prompt, c_sample, jnp.zeros((c_rows - batch - db, d), F32)], axis=0)
    mod = _ada(c_all, w_ada, b_ada)

    wr_pad = jnp.pad(w_router, ((0, 0), (0, LANES - N_EXPERTS)))
    n_assign = n_all * TOP_K
    n_rows = -(-(n_assign + N_EXPERTS * (TM_EXPERT - 1)) // TM_EXPERT) * TM_EXPERT

    cos_p, sin_p = _rotary_tables(jnp.arange(seq))
    cos_s, sin_s = _rotary_tables(past_len + jnp.zeros((db,), I32))

    xp = x_prompt.reshape(n_p, d)
    xs = x_sample.reshape(db, d)
    kp_pages, vp_pages, ks_rows, vs_rows, conv_p, conv_s = [], [], [], [], [], []
    for i in range(depth):
        mp = [m[:, None, :] for m in jnp.split(mod[i, :batch], 6, axis=-1)]
        ms = [m[None] for m in jnp.split(mod[i, batch:batch + db], 6, axis=-1)]
        if i % 2 == 0:
            ia = i // 2
            q, kp, vp, kx, v16, km = _qkv_prompt(xp, mp[0], mp[1], w_qkv[ia], cos_p, sin_p, batch, seq)
            km = km.transpose(0, 2, 1, 3, 4).reshape(batch, N_KV_HEADS, seq // MOBA_BLOCK, HEAD_DIM)
            attn_p = _moba_prompt(q, kx, v16, km, batch, seq)
            kp_pages.append(kp)
            vp_pages.append(vp)

            qkv_s = _qkv_sample(xs, ms[0][0], ms[1][0], w_qkv[ia], cos_s, sin_s)
            nq = N_HEADS * HEAD_DIM
            nk = N_KV_HEADS * HEAD_DIM
            q_s = qkv_s[:, :nq].reshape(db, N_HEADS, HEAD_DIM)
            k_s = qkv_s[:, nq:nq + nk].reshape(db, N_KV_HEADS, HEAD_DIM)
            v_s = qkv_s[:, nq + nk:].reshape(db, N_KV_HEADS, HEAD_DIM)
            ks_rows.append(k_s[:, :, None, :])
            vs_rows.append(v_s[:, :, None, :])
            if topk_s > 0:
                kmean_s = _kmean_sample(page_table, cache_k, ia)
                kvh = jnp.arange(N_HEADS) // GQA_GROUP
                gate_s = jnp.einsum("bhd,bnhd->bhn", q_s, kmean_s[:, :, kvh], precision=lax.Precision.HIGHEST)
                _, sel = lax.top_k(gate_s, topk_s)
                sel_pages = (sel[..., None] * PAGES_PER_BLOCK + jnp.arange(PAGES_PER_BLOCK)).reshape(db, N_HEADS, -1)
                phys = jnp.take_along_axis(page_table[:, None, :], sel_pages, axis=2).astype(I32)
                attn_s = _attn_sample(phys.reshape(-1), q_s.reshape(db * N_HEADS, 1, HEAD_DIM),
                                      k_s.reshape(db * N_KV_HEADS, 1, HEAD_DIM),
                                      v_s.reshape(db * N_KV_HEADS, 1, HEAD_DIM), cache_k, cache_v, ia)
                attn_s = attn_s.reshape(db, nq)
            else:
                attn_s = jnp.repeat(v_s, GQA_GROUP, axis=1).reshape(db, nq)
            mixer_p, mixer_s, w_mix = attn_p, attn_s, w_o[ia]
            mode_p, mode_s = "attn", "attn"
        else:
            ic = i // 2
            u_p = _glu(xp, mp[0], mp[1], conv_w_in[ic], TM, tps, False)
            u_s = _glu(xs, ms[0], ms[1], conv_w_in[ic], db, 1, True)
            cg, cb = conv_ln_g[ic][None], conv_ln_b[ic][None]
            mixer_p = (u_p, conv_w_dw[ic], cg, cb)
            mixer_s = (u_s, state_conv[ic].transpose(1, 0, 2), conv_w_dw[ic], cg, cb)
            w_mix = conv_w_out[ic]
            mode_p, mode_s = "conv", "conv_step"
            conv_p.append(u_p.reshape(batch, seq, d)[:, seq - (CONV_WIDTH - 1):])
            conv_s.append(jnp.concatenate([state_conv[ic][:, 1:], u_s[:, None, :]], axis=1))

        x1p, h2_all, lg_all = _post(mode_p, mixer_p, w_mix, xp, mp[2], ln_g[i, 0][None], ln_b[i, 0][None],
                                    mp[3], mp[4], wr_pad, tm=TM, rows_per_mod=tps, three_pass=False, alpha=alpha,
                                    seq=seq, out_rows=n_all)
        x1s, h2_all, lg_all = _post(mode_s, mixer_s, w_mix, xs, ms[2], ln_g[i, 0][None], ln_b[i, 0][None],
                                    ms[3], ms[4], wr_pad, tm=db, rows_per_mod=1, three_pass=True, alpha=alpha,
                                    into=(h2_all, lg_all, n_p // db))
        row_tok, row_w, tile_e, n_used, dest = _route(lg_all, b_router, n_all, n_rows)
        ys = _experts(row_tok, tile_e, n_used, h2_all, row_w, w_gate, w_up, w_down, i)
        xp = _combine(dest, ys, x1p, mp[5], ln_g[i, 1][None], ln_b[i, 1][None],
                      tm=TM_COMBINE, rows_per_mod=seq // TM_COMBINE, alpha=alpha, tok0=0)
        xs = _combine(dest, ys, x1s, ms[5], ln_g[i, 1][None], ln_b[i, 1][None],
                      tm=db, rows_per_mod=1, alpha=alpha, tok0=n_p)

    k_prompt = jnp.concatenate(kp_pages, axis=2)
    v_prompt = jnp.concatenate(vp_pages, axis=2)
    return (xp.reshape(batch, seq, d), xs.reshape(db, 1, d), k_prompt, v_prompt, jnp.stack(conv_p, axis=0),
            jnp.stack(ks_rows, axis=1), jnp.stack(vs_rows, axis=1), jnp.stack(conv_s, axis=0))
```

```python
import functools
import math

import jax
import jax.numpy as jnp
from jax import lax
from jax.experimental import pallas as pl
from jax.experimental.pallas import tpu as pltpu

F32 = jnp.float32
I32 = jnp.int32
MXU_DTYPE = jnp.bfloat16

N_HEADS = 8
N_KV_HEADS = 2
GQA_GROUP = N_HEADS // N_KV_HEADS
HEAD_DIM = 128
ROT_DIM = HEAD_DIM // 4
ROT_HALF = ROT_DIM // 2
ROPE_THETA = 500000.0
ATTN_SCALE = HEAD_DIM ** -0.5
MOBA_BLOCK = 256
MOBA_TOPK = 3
MOBA_CHUNK = 128
EXP2_SCALE = ATTN_SCALE * math.log2(math.e)
PAGE_SIZE = 128
PAGES_PER_BLOCK = MOBA_BLOCK // PAGE_SIZE
CONV_WIDTH = 31
CONV_HALO = 32
N_EXPERTS = 32
N_GROUPS = 4
EXPERTS_PER_GROUP = N_EXPERTS // N_GROUPS
TOP_K = 2
LN_EPS = 1e-5
MASK_VALUE = -1e30
LANES = 128
VMEM_LIMIT = 56 * 1024 * 1024

TM = 512
TM_EXPERT = 256
TM_COMBINE = 256
KMEAN_PAGES = 16
ROUTE_TILE = 512
GATHER_UNROLL = 8


def _params(*sem, row_gather=False):
    return pltpu.CompilerParams(dimension_semantics=sem, vmem_limit_bytes=VMEM_LIMIT,
                                disable_bounds_checks=row_gather)


def _dot(a, b):
    return jnp.dot(a.astype(MXU_DTYPE), b.astype(MXU_DTYPE), preferred_element_type=F32)


def _dot_nt(a, b):
    return lax.dot_general(a.astype(MXU_DTYPE), b.astype(MXU_DTYPE), (((1,), (1,)), ((), ())),
                           preferred_element_type=F32)


def _split(a):
    hi = a.astype(MXU_DTYPE)
    lo = (a - hi.astype(F32)).astype(MXU_DTYPE)
    return hi, lo


def _dot3(a, b):
    a_hi, a_lo = _split(a)
    b_hi, b_lo = _split(b)
    d = functools.partial(jnp.dot, preferred_element_type=F32)
    return d(a_hi, b_hi) + (d(a_hi, b_lo) + d(a_lo, b_hi))


def _dot3_nt(a, b):
    a_hi, a_lo = _split(a)
    b_hi, b_lo = _split(b)
    d = functools.partial(lax.dot_general, dimension_numbers=(((1,), (1,)), ((), ())),
                          preferred_element_type=F32)
    return d(a_hi, b_hi) + (d(a_hi, b_lo) + d(a_lo, b_hi))


def _layer_norm(z, g, b):
    mu = jnp.mean(z, axis=-1, keepdims=True)
    zc = z - mu
    var = jnp.mean(zc * zc, axis=-1, keepdims=True)
    return zc * lax.rsqrt(var + LN_EPS) * g + b


def _silu(x):
    return x * jax.nn.sigmoid(x)


def _rotary(xc, cos, sin, lane):
    x_up = pltpu.roll(xc, LANES - ROT_HALF, axis=1)
    x_dn = pltpu.roll(xc, ROT_HALF, axis=1)
    first = xc * cos - x_up * sin
    second = xc * cos + x_dn * sin
    return jnp.where(lane < ROT_HALF, first, jnp.where(lane < ROT_DIM, second, xc))


def _ada_body(c_ref, w_ref, b_ref, o_ref):
    o_ref[0] = _dot3(_silu(c_ref[...]), w_ref[0]) + b_ref[0]


def _ada(c_all, w_ada, b_ada):
    depth, d, n6 = w_ada.shape
    rows = c_all.shape[0]
    tn = 1536
    return pl.pallas_call(
        _ada_body,
        grid=(depth, n6 // tn),
        in_specs=[pl.BlockSpec((rows, d), lambda i, j: (0, 0)),
                  pl.BlockSpec((1, d, tn), lambda i, j: (i, 0, j)),
                  pl.BlockSpec((1, 1, tn), lambda i, j: (i, 0, j))],
        out_specs=pl.BlockSpec((1, rows, tn), lambda i, j: (i, 0, j)),
        out_shape=jax.ShapeDtypeStruct((depth, rows, n6), F32),
        compiler_params=_params("arbitrary", "arbitrary"),
        name="ada",
    )(c_all, w_ada, b_ada.reshape(depth, 1, n6))


def _qkv_prompt_body(x_ref, sh_ref, sc_ref, w_ref, cos_ref, sin_ref,
                     q_ref, kp_ref, vp_ref, kx_ref, v16_ref, km_ref, w16, *, tiles_per_seq):
    t = pl.program_id(0)
    tm = x_ref.shape[0]

    @pl.when(t == 0)
    def _():
        w16[...] = w_ref[...].astype(MXU_DTYPE)

    h = x_ref[...] * (1 + sc_ref[0]) + sh_ref[0]
    qkv = jnp.dot(h.astype(MXU_DTYPE), w16[...], preferred_element_type=F32)
    cos = cos_ref[...]
    sin = sin_ref[...]
    lane = lax.broadcasted_iota(I32, (tm, LANES), 1)
    nq = N_HEADS * HEAD_DIM
    nk = N_KV_HEADS * HEAD_DIM
    for hh in range(N_HEADS):
        sl = slice(hh * HEAD_DIM, (hh + 1) * HEAD_DIM)
        q_ref[:, sl] = _rotary(qkv[:, sl], cos, sin, lane).astype(q_ref.dtype)
    row = lax.broadcasted_iota(I32, (tm, LANES), 0)
    blk = ((t % tiles_per_seq) * tm + row) // MOBA_BLOCK
    onehot = jnp.where(lane == blk, 1.0, 0.0).astype(kx_ref.dtype)
    for n in range(N_KV_HEADS):
        kc = _rotary(qkv[:, nq + n * HEAD_DIM:nq + (n + 1) * HEAD_DIM], cos, sin, lane)
        vc = qkv[:, nq + nk + n * HEAD_DIM:nq + nk + (n + 1) * HEAD_DIM]
        kp_ref[0, :, 0, n] = kc.reshape(tm // PAGE_SIZE, PAGE_SIZE, HEAD_DIM)
        vp_ref[0, :, 0, n] = vc.reshape(tm // PAGE_SIZE, PAGE_SIZE, HEAD_DIM)
        kx_ref[0, n, :, 0:HEAD_DIM] = kc.astype(kx_ref.dtype)
        kx_ref[0, n, :, HEAD_DIM:2 * HEAD_DIM] = onehot
        v16_ref[0, n] = vc.astype(v16_ref.dtype)
        km_ref[0, 0, n] = jnp.sum(kc.reshape(tm // MOBA_BLOCK, MOBA_BLOCK, HEAD_DIM), axis=1) * (1.0 / MOBA_BLOCK)


def _qkv_prompt(x, shift, scale, w_qkv, cos_t, sin_t, batch, seq):
    n, d = x.shape
    tm = TM
    tps = seq // tm
    width = w_qkv.shape[1]
    npg = seq // PAGE_SIZE
    mod_spec = pl.BlockSpec((1, 1, d), lambda t: (t // tps, 0, 0))
    rot_spec = pl.BlockSpec((tm, LANES), lambda t: (t % tps, 0))
    page_spec = pl.BlockSpec((1, tm // PAGE_SIZE, 1, N_KV_HEADS, PAGE_SIZE, HEAD_DIM),
                             lambda t: (t // tps, t % tps, 0, 0, 0, 0))
    page_shape = jax.ShapeDtypeStruct((batch, npg, 1, N_KV_HEADS, PAGE_SIZE, HEAD_DIM), F32)
    return pl.pallas_call(
        functools.partial(_qkv_prompt_body, tiles_per_seq=tps),
        grid=(n // tm,),
        in_specs=[pl.BlockSpec((tm, d), lambda t: (t, 0)), mod_spec, mod_spec,
                  pl.BlockSpec((d, width), lambda t: (0, 0)), rot_spec, rot_spec],
        out_specs=[pl.BlockSpec((tm, N_HEADS * HEAD_DIM), lambda t: (t, 0)),
                   page_spec, page_spec,
                   pl.BlockSpec((1, N_KV_HEADS, tm, 2 * HEAD_DIM), lambda t: (t // tps, 0, t % tps, 0)),
                   pl.BlockSpec((1, N_KV_HEADS, tm, HEAD_DIM), lambda t: (t // tps, 0, t % tps, 0)),
                   pl.BlockSpec((1, 1, N_KV_HEADS, tm // MOBA_BLOCK, HEAD_DIM), lambda t: (t // tps, t % tps, 0, 0, 0))],
        out_shape=[jax.ShapeDtypeStruct((n, N_HEADS * HEAD_DIM), MXU_DTYPE),
                   page_shape, page_shape,
                   jax.ShapeDtypeStruct((batch, N_KV_HEADS, seq, 2 * HEAD_DIM), MXU_DTYPE),
                   jax.ShapeDtypeStruct((batch, N_KV_HEADS, seq, HEAD_DIM), MXU_DTYPE),
                   jax.ShapeDtypeStruct((batch, tps, N_KV_HEADS, tm // MOBA_BLOCK, HEAD_DIM), F32)],
        scratch_shapes=[pltpu.VMEM((d, width), MXU_DTYPE)],
        compiler_params=_params("arbitrary"),
        name="qkv_prompt",
    )(x, shift, scale, w_qkv, cos_t, sin_t)


def _qkv_sample_body(x_ref, sh_ref, sc_ref, w_ref, cos_ref, sin_ref, o_ref):
    rows = x_ref.shape[0]
    h = x_ref[...] * (1 + sc_ref[...]) + sh_ref[...]
    qkv = _dot3(h, w_ref[...])
    lane = lax.broadcasted_iota(I32, (rows, LANES), 1)
    n_rot = N_HEADS + N_KV_HEADS
    for c in range(n_rot):
        sl = slice(c * HEAD_DIM, (c + 1) * HEAD_DIM)
        o_ref[:, sl] = _rotary(qkv[:, sl], cos_ref[...], sin_ref[...], lane)
    o_ref[:, n_rot * HEAD_DIM:] = qkv[:, n_rot * HEAD_DIM:]


def _qkv_sample(x, shift, scale, w_qkv, cos_t, sin_t):
    rows, d = x.shape
    width = w_qkv.shape[1]
    full = lambda shape: pl.BlockSpec(shape, lambda: tuple(0 for _ in shape))
    return pl.pallas_call(
        _qkv_sample_body,
        in_specs=[full((rows, d)), full((rows, d)), full((rows, d)), full((d, width)),
                  full((rows, LANES)), full((rows, LANES))],
        out_specs=full((rows, width)),
        out_shape=jax.ShapeDtypeStruct((rows, width), F32),
        compiler_params=pltpu.CompilerParams(vmem_limit_bytes=VMEM_LIMIT),
        name="qkv_sample",
    )(x, shift, scale, w_qkv, cos_t, sin_t)


def _moba_prompt_body(q_ref, kx_ref, v_ref, km_ref, o_ref, qx_ref, *state):
    i = pl.program_id(2)
    rows = GQA_GROUP * MOBA_BLOCK
    n_chunks = rows // MOBA_CHUNK
    s_refs, p_refs, a_refs, acc_ref = state[0:2], state[2:4], state[4:6], state[6]
    m_refs, l_refs = state[7:7 + n_chunks], state[7 + n_chunks:]
    n_blk = km_ref.shape[2]
    for h in range(GQA_GROUP):
        qx_ref[h * MOBA_BLOCK:(h + 1) * MOBA_BLOCK, 0:HEAD_DIM] = q_ref[:, h * HEAD_DIM:(h + 1) * HEAD_DIM]

    gate = _dot_nt(km_ref[0, 0], qx_ref[:, 0:HEAD_DIM])
    blk = lax.broadcasted_iota(I32, (n_blk, rows), 0)
    valid = blk < i
    cand = jnp.where(valid, gate, -jnp.inf)
    sel = blk == i
    for _ in range(MOBA_TOPK):
        top = jnp.max(cand, axis=0, keepdims=True)
        idx = jnp.min(jnp.where(cand == top, blk, n_blk), axis=0, keepdims=True)
        pick = blk == idx
        sel = sel | (pick & valid)
        cand = jnp.where(pick, -jnp.inf, cand)
    bias = jnp.where(sel, 0.0, MASK_VALUE)
    if n_blk < LANES:
        bias = jnp.concatenate([bias, jnp.zeros((LANES - n_blk, rows), F32)], axis=0)
    qx_ref[:, HEAD_DIM:2 * HEAD_DIM] = bias.T.astype(qx_ref.dtype)

    half = MOBA_BLOCK // 2

    def issue_scores(j, slot):
        start = pl.multiple_of(j * MOBA_BLOCK, MOBA_BLOCK)
        s_refs[slot][...] = _dot_nt(qx_ref[...], kx_ref[0, 0, pl.ds(start, MOBA_BLOCK), :])

    def softmax(slot, own):
        for c in range(n_chunks):
            rs = slice(c * MOBA_CHUNK, (c + 1) * MOBA_CHUNK)
            s = s_refs[slot][rs, :]
            if own:
                qpos = (c * MOBA_CHUNK) % MOBA_BLOCK + lax.broadcasted_iota(I32, (MOBA_CHUNK, MOBA_BLOCK), 0)
                kpos = lax.broadcasted_iota(I32, (MOBA_CHUNK, MOBA_BLOCK), 1)
                s = jnp.where(kpos <= qpos, s, MASK_VALUE)
            sa, sb = s[:, :half], s[:, half:]
            top = jnp.broadcast_to(jnp.max(jnp.maximum(sa, sb), axis=1, keepdims=True), (MOBA_CHUNK, half))
            if own:
                m_new = top
            else:
                m_old = m_refs[c][...]
                m_new = jnp.maximum(m_old, top)
                alpha = jnp.exp2((m_old - m_new) * EXP2_SCALE)
                a_refs[slot][rs, :] = alpha
            pa = jnp.exp2((sa - m_new) * EXP2_SCALE)
            pb = jnp.exp2((sb - m_new) * EXP2_SCALE)
            p_refs[slot][rs, :half] = pa.astype(MXU_DTYPE)
            p_refs[slot][rs, half:] = pb.astype(MXU_DTYPE)
            if own:
                l_refs[c][...] = pa + pb
            else:
                l_refs[c][...] = alpha * l_refs[c][...] + (pa + pb)
            m_refs[c][...] = m_new

    def accumulate(j, slot, own):
        start = pl.multiple_of(j * MOBA_BLOCK, MOBA_BLOCK)
        pv = jnp.dot(p_refs[slot][...], v_ref[0, 0, pl.ds(start, MOBA_BLOCK), :], preferred_element_type=F32)
        if own:
            acc_ref[...] = pv
        else:
            acc_ref[...] = a_refs[slot][...] * acc_ref[...] + pv

    issue_scores(i, 1)
    issue_scores(0, 0)
    softmax(1, True)
    accumulate(i, 1, True)

    def pair(t, carry):
        j = 2 * t
        issue_scores(j + 1, 1)
        softmax(0, False)
        accumulate(j, 0, False)
        issue_scores(jnp.minimum(j + 2, i - 1), 0)
        softmax(1, False)
        accumulate(j + 1, 1, False)
        return carry

    lax.fori_loop(0, i // 2, pair, 0)

    @pl.when(i % 2 == 1)
    def _():
        softmax(0, False)
        accumulate(i - 1, 0, False)

    per_head = MOBA_BLOCK // MOBA_CHUNK
    for c in range(n_chunks):
        rs = slice(c * MOBA_CHUNK, (c + 1) * MOBA_CHUNK)
        out = acc_ref[rs, :] / jnp.sum(l_refs[c][...], axis=1, keepdims=True)
        h, part = divmod(c, per_head)
        o_ref[part * MOBA_CHUNK:(part + 1) * MOBA_CHUNK, h * HEAD_DIM:(h + 1) * HEAD_DIM] = out.astype(o_ref.dtype)


def _moba_prompt(q, kx, v16, k_mean, batch, seq):
    n = q.shape[0]
    nq = seq // MOBA_BLOCK
    rows = GQA_GROUP * MOBA_BLOCK
    gw = GQA_GROUP * HEAD_DIM
    return pl.pallas_call(
        _moba_prompt_body,
        grid=(batch, N_KV_HEADS, nq),
        in_specs=[pl.BlockSpec((MOBA_BLOCK, gw), lambda b, g, i: (b * nq + i, g)),
                  pl.BlockSpec((1, 1, seq, 2 * HEAD_DIM), lambda b, g, i: (b, g, 0, 0)),
                  pl.BlockSpec((1, 1, seq, HEAD_DIM), lambda b, g, i: (b, g, 0, 0)),
                  pl.BlockSpec((1, 1, nq, HEAD_DIM), lambda b, g, i: (b, g, 0, 0))],
        out_specs=pl.BlockSpec((MOBA_BLOCK, gw), lambda b, g, i: (b * nq + i, g)),
        out_shape=jax.ShapeDtypeStruct((n, N_HEADS * HEAD_DIM), MXU_DTYPE),
        scratch_shapes=([pltpu.VMEM((rows, 2 * HEAD_DIM), MXU_DTYPE)]
                        + [pltpu.VMEM((rows, MOBA_BLOCK), F32)] * 2
                        + [pltpu.VMEM((rows, MOBA_BLOCK), MXU_DTYPE)] * 2
                        + [pltpu.VMEM((rows, MOBA_BLOCK // 2), F32)] * 2
                        + [pltpu.VMEM((rows, HEAD_DIM), F32)]
                        + [pltpu.VMEM((MOBA_CHUNK, MOBA_BLOCK // 2), F32)] * (2 * (rows // MOBA_CHUNK))),
        compiler_params=_params("arbitrary", "arbitrary", "arbitrary"),
        name="moba_prompt",
    )(q, kx, v16, k_mean)


def _kmean_copy(pt_ref, ck_hbm, buf, sem, layer, step, slot, p):
    phys = pt_ref[step * KMEAN_PAGES + p]
    return pltpu.make_async_copy(ck_hbm.at[phys, layer], buf.at[slot, p], sem.at[slot])


def _kmean_sample_body(pt_ref, ck_hbm, o_ref, buf, sem, *, layer):
    nc = pl.num_programs(1)
    step = pl.program_id(0) * nc + pl.program_id(1)
    total = pl.num_programs(0) * nc
    slot = step % 2

    def start(step_, slot_):
        for p in range(KMEAN_PAGES):
            _kmean_copy(pt_ref, ck_hbm, buf, sem, layer, step_, slot_, p).start()

    @pl.when(step == 0)
    def _():
        start(step, slot)

    @pl.when(step + 1 < total)
    def _():
        start(step + 1, 1 - slot)

    for p in range(KMEAN_PAGES):
        _kmean_copy(pt_ref, ck_hbm, buf, sem, layer, step, slot, p).wait()
    for jb in range(KMEAN_PAGES // PAGES_PER_BLOCK):
        for n in range(N_KV_HEADS):
            acc = jnp.zeros((1, HEAD_DIM), F32)
            for pp in range(PAGES_PER_BLOCK):
                acc = acc + jnp.sum(buf[slot, jb * PAGES_PER_BLOCK + pp, n], axis=0, keepdims=True)
            o_ref[0, jb, pl.ds(n, 1), :] = acc * (1.0 / MOBA_BLOCK)


def _kmean_sample(page_table, cache_k, layer):
    db, n_pages = page_table.shape
    n_full = n_pages // PAGES_PER_BLOCK
    page_shape = cache_k.shape[2:]
    bps = KMEAN_PAGES // PAGES_PER_BLOCK
    return pl.pallas_call(
        functools.partial(_kmean_sample_body, layer=layer),
        grid_spec=pltpu.PrefetchScalarGridSpec(
            num_scalar_prefetch=1,
            grid=(db, n_pages // KMEAN_PAGES),
            in_specs=[pl.BlockSpec(memory_space=pl.ANY)],
            out_specs=pl.BlockSpec((1, bps, N_KV_HEADS, HEAD_DIM), lambda b, c, pt: (b, c, 0, 0)),
            scratch_shapes=[pltpu.VMEM((2, KMEAN_PAGES) + page_shape, F32),
                            pltpu.SemaphoreType.DMA((2,))]),
        out_shape=jax.ShapeDtypeStruct((db, n_full, N_KV_HEADS, HEAD_DIM), F32),
        compiler_params=_params("arbitrary", "arbitrary"),
        name="kmean_sample",
    )(page_table.reshape(-1), cache_k)


def _attn_sample_body(pages_ref, q_ref, kn_ref, vn_ref, *refs):
    n_sel = (len(refs) - 1) // 2
    k_refs, v_refs, o_ref = refs[:n_sel], refs[n_sel:2 * n_sel], refs[-1]
    q = q_ref[0]
    q8 = jnp.broadcast_to(q, (8, HEAD_DIM))
    s_new = jnp.sum(q * kn_ref[0], axis=1, keepdims=True) * ATTN_SCALE
    s_sel = [_dot3_nt(q8, k_ref[0, 0, 0])[0:1] * ATTN_SCALE for k_ref in k_refs]
    m = s_new
    for s in s_sel:
        m = jnp.maximum(m, jnp.max(s, axis=1, keepdims=True))
    p_new = jnp.exp(s_new - m)
    denom = p_new
    out = p_new * vn_ref[0]
    for s, v_ref in zip(s_sel, v_refs):
        p = jnp.exp(s - m)
        denom = denom + jnp.sum(p, axis=1, keepdims=True)
        out = out + _dot3(jnp.broadcast_to(p, (8, PAGE_SIZE)), v_ref[0, 0, 0])[0:1]
    o_ref[0] = out / denom


def _attn_sample(phys, q, k_new, v_new, cache_k, cache_v, layer):
    db = q.shape[0] // N_HEADS
    n_sel = phys.shape[0] // (db * N_HEADS)
    vec = lambda f: pl.BlockSpec((1, 1, HEAD_DIM), f)

    def page_spec(p):
        return pl.BlockSpec((1, 1, 1, PAGE_SIZE, HEAD_DIM),
                            lambda b, h, pg: (pg[(b * N_HEADS + h) * n_sel + p], layer, h // GQA_GROUP, 0, 0))

    page_specs = [page_spec(p) for p in range(n_sel)]
    return pl.pallas_call(
        _attn_sample_body,
        grid_spec=pltpu.PrefetchScalarGridSpec(
            num_scalar_prefetch=1,
            grid=(db, N_HEADS),
            in_specs=[vec(lambda b, h, pg: (b * N_HEADS + h, 0, 0)),
                      vec(lambda b, h, pg: (b * N_KV_HEADS + h // GQA_GROUP, 0, 0)),
                      vec(lambda b, h, pg: (b * N_KV_HEADS + h // GQA_GROUP, 0, 0))] + page_specs + page_specs,
            out_specs=vec(lambda b, h, pg: (b * N_HEADS + h, 0, 0))),
        out_shape=jax.ShapeDtypeStruct((db * N_HEADS, 1, HEAD_DIM), F32),
        compiler_params=_params("arbitrary", "arbitrary"),
        name="attn_sample",
    )(phys, q, k_new, v_new, *([cache_k] * n_sel), *([cache_v] * n_sel))


def _glu_body(x_ref, sh_ref, sc_ref, w_ref, o_ref, w16, *, three_pass):
    c = o_ref.shape[1]
    h = x_ref[...] * (1 + sc_ref[0]) + sh_ref[0]
    if three_pass:
        ag = _dot3(h, w_ref[...])
    else:
        @pl.when(pl.program_id(0) == 0)
        def _():
            w16[...] = w_ref[...].astype(MXU_DTYPE)

        ag = jnp.dot(h.astype(MXU_DTYPE), w16[...], preferred_element_type=F32)
    o_ref[...] = ag[:, :c] * jax.nn.sigmoid(ag[:, c:])


def _glu(x, shift, scale, w_in, tm, rows_per_mod, three_pass):
    n, d = x.shape
    c2 = w_in.shape[1]
    r = shift.shape[1]
    mod_spec = pl.BlockSpec((1, r, d), lambda t: (t // rows_per_mod, 0, 0))
    return pl.pallas_call(
        functools.partial(_glu_body, three_pass=three_pass),
        grid=(n // tm,),
        in_specs=[pl.BlockSpec((tm, d), lambda t: (t, 0)), mod_spec, mod_spec,
                  pl.BlockSpec((d, c2), lambda t: (0, 0))],
        out_specs=pl.BlockSpec((tm, c2 // 2), lambda t: (t, 0)),
        out_shape=jax.ShapeDtypeStruct((n, c2 // 2), F32),
        scratch_shapes=[pltpu.VMEM((d, c2) if not three_pass else (8, LANES), MXU_DTYPE)],
        compiler_params=_params("arbitrary"),
        name="glu",
    )(x, shift, scale, w_in)


def _post_body(*refs, n_tiles, **kw):
    n_mixer = 1 if kw["mode"] == "attn" else 5
    h2_ref, lg_ref = refs[n_mixer + 9], refs[n_mixer + 10]
    t = pl.program_id(0)

    @pl.when(t < n_tiles)
    def _():
        _post_tile(*refs, **kw)

    @pl.when(t >= n_tiles)
    def _():
        h2_ref[...] = jnp.zeros(h2_ref.shape, F32)
        lg_ref[...] = jnp.zeros(lg_ref.shape, F32)


def _post_tile(*refs, mode, three_pass, alpha, tiles_per_seq):
    if mode == "attn":
        a_ref, rest = refs[0], refs[1:]
    elif mode == "conv":
        u_ref, prev_ref, wdw_ref, cg_ref, cb_ref = refs[:5]
        rest = refs[5:]
    else:
        u_ref, st_ref, wdw_ref, cg_ref, cb_ref = refs[:5]
        rest = refs[5:]
    (w_ref, x_ref, g1_ref, lng_ref, lnb_ref, sh2_ref, sc2_ref, wr_ref,
     x1_ref, h2_ref, lg_ref, w16) = rest[:12]
    t = pl.program_id(0)
    tm = x_ref.shape[0]

    if mode == "attn":
        a = a_ref[...]
    else:
        if mode == "conv":
            ext_ref = rest[12]
            first = (t % tiles_per_seq) == 0
            ext_ref[0:CONV_HALO] = jnp.where(first, 0.0, prev_ref[...])
            ext_ref[CONV_HALO:] = u_ref[...]
            off = CONV_HALO - (CONV_WIDTH - 1)
            y = ext_ref[pl.ds(off, tm), :] * wdw_ref[0:1, :]
            for w in range(1, CONV_WIDTH):
                y = y + ext_ref[pl.ds(off + w, tm), :] * wdw_ref[w:w + 1, :]
        else:
            y = u_ref[...] * wdw_ref[CONV_WIDTH - 1:CONV_WIDTH, :]
            for w in range(CONV_WIDTH - 1):
                y = y + st_ref[w] * wdw_ref[w:w + 1, :]
        a = _silu(_layer_norm(y, cg_ref[...], cb_ref[...]))

    if three_pass:
        f = _dot3(a.astype(F32), w_ref[...])
    else:
        @pl.when(t == 0)
        def _():
            w16[...] = w_ref[...].astype(MXU_DTYPE)

        f = jnp.dot(a.astype(MXU_DTYPE), w16[...], preferred_element_type=F32)
    x1 = _layer_norm(alpha * x_ref[...] + g1_ref[0] * f, lng_ref[...], lnb_ref[...])
    h2 = x1 * (1 + sc2_ref[0]) + sh2_ref[0]
    x1_ref[...] = x1
    h2_ref[...] = h2
    lg_ref[...] = _dot3(h2, wr_ref[...])


def _post(mode, mixer_in, w, x, gate1, ln_g, ln_b, shift2, scale2, w_router, *, tm, rows_per_mod, three_pass,
          alpha, seq=None, out_rows=None, into=None):
    n, d = x.shape
    out_rows = n if out_rows is None else out_rows
    r = gate1.shape[1]
    tps = None if seq is None else seq // tm
    n_tiles = n // tm
    n_steps = -(-out_rows // tm)
    last = n_tiles - 1
    row = lambda t: (jnp.minimum(t, last), 0)
    const = lambda t: (0, 0)
    mod_spec = pl.BlockSpec((1, r, d), lambda t: (jnp.minimum(t, last) // rows_per_mod, 0, 0))
    vec_spec = pl.BlockSpec((1, d), const)
    if mode == "attn":
        mixer_specs = [pl.BlockSpec((tm, d), row)]
    elif mode == "conv":
        u, w_dw, cg, cb = mixer_in
        per = tm // CONV_HALO
        mixer_in = (u, u, w_dw, cg, cb)
        mixer_specs = [pl.BlockSpec((tm, d), row),
                       pl.BlockSpec((CONV_HALO, d), lambda t: (jnp.maximum(jnp.minimum(t, last) * per - 1, 0), 0)),
                       pl.BlockSpec(w_dw.shape, const), vec_spec, vec_spec]
    else:
        u, state, w_dw, cg, cb = mixer_in
        mixer_specs = [pl.BlockSpec((tm, d), row), pl.BlockSpec(state.shape, lambda t: (0, 0, 0)),
                       pl.BlockSpec(w_dw.shape, const), vec_spec, vec_spec]
    if mode == "attn":
        mixer_in = (mixer_in,)
    in_specs = mixer_specs + [pl.BlockSpec(w.shape, const), pl.BlockSpec((tm, d), row), mod_spec, vec_spec, vec_spec,
                              mod_spec, mod_spec, pl.BlockSpec(w_router.shape, const)]
    args = list(mixer_in) + [w, x, gate1, ln_g, ln_b, shift2, scale2, w_router]
    out_shape = [jax.ShapeDtypeStruct((n, d), F32)]
    out_specs = [pl.BlockSpec((tm, d), row)]
    aliases = {}
    if into is None:
        out_shape += [jax.ShapeDtypeStruct((out_rows, d), F32), jax.ShapeDtypeStruct((out_rows, LANES), F32)]
        out_specs += [pl.BlockSpec((tm, d), lambda t: (t, 0)), pl.BlockSpec((tm, LANES), lambda t: (t, 0))]
    else:
        h2_all, lg_all, row_block = into
        out_shape += [jax.ShapeDtypeStruct(h2_all.shape, F32), jax.ShapeDtypeStruct(lg_all.shape, F32)]
        out_specs += [pl.BlockSpec((tm, d), lambda t: (row_block + t, 0)),
                      pl.BlockSpec((tm, LANES), lambda t: (row_block + t, 0))]
        aliases = {len(args): 1, len(args) + 1: 2}
        in_specs += [pl.BlockSpec(memory_space=pl.ANY), pl.BlockSpec(memory_space=pl.ANY)]
        args += [h2_all, lg_all]
    scratch = [pltpu.VMEM(w.shape if not three_pass else (8, LANES), MXU_DTYPE)]
    if mode == "conv":
        scratch.append(pltpu.VMEM((CONV_HALO + tm, d), F32))

    def body(*refs):
        if into is not None:
            n_in = len(args)
            refs = refs[:n_in - 2] + refs[n_in:]
        _post_body(*refs, n_tiles=n_tiles, mode=mode, three_pass=three_pass, alpha=alpha, tiles_per_seq=tps)

    return pl.pallas_call(
        body,
        grid=(n_steps,),
        in_specs=in_specs,
        out_specs=out_specs,
        out_shape=out_shape,
        scratch_shapes=scratch,
        input_output_aliases=aliases,
        compiler_params=_params("arbitrary"),
        name="post_" + mode,
    )(*args)


def _gather_rows(idx_ref, base, src_hbm, dst, sem, n):
    def issue(r, c):
        pltpu.make_async_copy(src_hbm.at[pl.ds(idx_ref[base + r], 1)], dst.at[pl.ds(r, 1)], sem).start()
        return c

    lax.fori_loop(0, n, issue, 0, unroll=GATHER_UNROLL)


def _wait_rows(src_hbm, dst, sem, n):
    def drain(r, c):
        pltpu.make_async_copy(src_hbm.at[pl.ds(0, 1)], dst.at[pl.ds(r, 1)], sem).wait()
        return c

    lax.fori_loop(0, n, drain, 0, unroll=GATHER_UNROLL)


def _expert_body(tok_ref, te_ref, nu_ref, h_hbm, wg_ref, wu_ref, wd_ref, o_ref,
                 xbuf, wg16, wu16, wd16, sem):
    i = pl.program_id(0)
    tm = o_ref.shape[0]
    n_used = nu_ref[0]
    slot = i % 2

    @pl.when(i == 0)
    def _():
        _gather_rows(tok_ref, 0, h_hbm, xbuf.at[0], sem.at[0], tm)

    @pl.when(i + 1 < n_used)
    def _():
        _gather_rows(tok_ref, (i + 1) * tm, h_hbm, xbuf.at[1 - slot], sem.at[1 - slot], tm)

    @pl.when(i < n_used)
    def _():
        changed = jnp.logical_or(i == 0, te_ref[i] != te_ref[jnp.maximum(i - 1, 0)])

        @pl.when(changed)
        def _():
            wg16[...] = wg_ref[0].astype(MXU_DTYPE)
            wu16[...] = wu_ref[0].astype(MXU_DTYPE)
            wd16[...] = wd_ref[0].astype(MXU_DTYPE)

        _wait_rows(h_hbm, xbuf.at[slot], sem.at[slot], tm)
        x = xbuf[slot].astype(MXU_DTYPE)
        g = jnp.dot(x, wg16[...], preferred_element_type=F32)
        u = jnp.dot(x, wu16[...], preferred_element_type=F32)
        o_ref[...] = jnp.dot((_silu(g) * u).astype(MXU_DTYPE), wd16[...], preferred_element_type=F32)

    @pl.when(i >= n_used)
    def _():
        o_ref[...] = jnp.zeros(o_ref.shape, F32)


def _experts(row_tok, tile_e, n_used, h_all, w_gate, w_up, w_down, layer):
    n_rows = row_tok.shape[0]
    tm = TM_EXPERT
    d = h_all.shape[1]
    de = w_gate.shape[-1]
    wspec_in = pl.BlockSpec((None, 1, d, de), lambda i, tok, te, nu: (layer, te[i], 0, 0))
    wspec_out = pl.BlockSpec((None, 1, de, d), lambda i, tok, te, nu: (layer, te[i], 0, 0))
    return pl.pallas_call(
        _expert_body,
        grid_spec=pltpu.PrefetchScalarGridSpec(
            num_scalar_prefetch=3,
            grid=(n_rows // tm,),
            in_specs=[pl.BlockSpec(memory_space=pl.ANY), wspec_in, wspec_in, wspec_out],
            out_specs=pl.BlockSpec((tm, d), lambda i, tok, te, nu: (i, 0)),
            scratch_shapes=[pltpu.VMEM((2, tm, d), F32),
                            pltpu.VMEM((d, de), MXU_DTYPE), pltpu.VMEM((d, de), MXU_DTYPE),
                            pltpu.VMEM((de, d), MXU_DTYPE), pltpu.SemaphoreType.DMA((2,))]),
        out_shape=jax.ShapeDtypeStruct((n_rows, d), F32),
        compiler_params=_params("arbitrary", row_gather=True),
        name="experts",
    )(row_tok, tile_e, n_used, h_all, w_gate, w_up, w_down)


def _combine_body(dest_ref, ys_hbm, x_ref, w_ref, g_ref, lng_ref, lnb_ref, o_ref, buf, sem, *, alpha, tok0, k_stride):
    t = pl.program_id(0)
    tm = x_ref.shape[0]
    slot = t % 2

    def gather(tile, slot_):
        for k in range(TOP_K):
            _gather_rows(dest_ref, k * k_stride + tok0 + tile * tm, ys_hbm, buf.at[slot_, k], sem.at[slot_, k], tm)

    @pl.when(t == 0)
    def _():
        gather(0, 0)

    @pl.when(t + 1 < pl.num_programs(0))
    def _():
        gather(t + 1, 1 - slot)

    for k in range(TOP_K):
        _wait_rows(ys_hbm, buf.at[slot, k], sem.at[slot, k], tm)
    f = buf[slot, 0] * w_ref[:, 0:1]
    for k in range(1, TOP_K):
        f = f + buf[slot, k] * w_ref[:, k:k + 1]
    o_ref[...] = _layer_norm(alpha * x_ref[...] + g_ref[0] * f, lng_ref[...], lnb_ref[...])


def _combine(dest, ys, x, wts, gate2, ln_g, ln_b, *, tm, rows_per_mod, alpha, tok0):
    n, d = x.shape
    r = gate2.shape[1]
    vec_spec = pl.BlockSpec((1, d), lambda t, ds_: (0, 0))
    return pl.pallas_call(
        functools.partial(_combine_body, alpha=alpha, tok0=tok0, k_stride=dest.shape[0] // TOP_K),
        grid_spec=pltpu.PrefetchScalarGridSpec(
            num_scalar_prefetch=1,
            grid=(n // tm,),
            in_specs=[pl.BlockSpec(memory_space=pl.ANY),
                      pl.BlockSpec((tm, d), lambda t, ds_: (t, 0)),
                      pl.BlockSpec((tm, TOP_K), lambda t, ds_: (t, 0)),
                      pl.BlockSpec((1, r, d), lambda t, ds_: (t // rows_per_mod, 0, 0)),
                      vec_spec, vec_spec],
            out_specs=pl.BlockSpec((tm, d), lambda t, ds_: (t, 0)),
            scratch_shapes=[pltpu.VMEM((2, TOP_K, tm, d), F32), pltpu.SemaphoreType.DMA((2, TOP_K))]),
        out_shape=jax.ShapeDtypeStruct((n, d), F32),
        compiler_params=_params("arbitrary", row_gather=True),
        name="combine",
    )(dest, ys, x, wts, gate2, ln_g, ln_b)


def _top2_of_group(x, sub):
    m1 = jnp.max(x, axis=0, keepdims=True)
    i1 = jnp.min(jnp.where(x == m1, sub, EXPERTS_PER_GROUP), axis=0, keepdims=True)
    rest = jnp.where(sub == i1, -jnp.inf, x)
    m2 = jnp.max(rest, axis=0, keepdims=True)
    i2 = jnp.min(jnp.where(rest == m2, sub, EXPERTS_PER_GROUP), axis=0, keepdims=True)
    return m1 + m2, i1, i2


def _route_body(lg_ref, b_ref, e_ref, w_ref, r_ref, cnt_ref, tri_ref, base_ref, *, n_tok):
    t = pl.program_id(0)
    tt = lg_ref.shape[0]

    @pl.when(t == 0)
    def _():
        earlier = lax.broadcasted_iota(I32, (tt, tt), 0) < lax.broadcasted_iota(I32, (tt, tt), 1)
        tri_ref[...] = jnp.where(earlier, 1.0, 0.0).astype(MXU_DTYPE)
        base_ref[...] = jnp.zeros(base_ref.shape, F32)

    scores = jax.nn.sigmoid(lg_ref[...].T[:N_EXPERTS])
    biased = scores + b_ref[...]
    sub = lax.broadcasted_iota(I32, (EXPERTS_PER_GROUP, tt), 0)
    best, i1, i2 = _top2_of_group(biased[:EXPERTS_PER_GROUP], sub)
    g_sel = jnp.zeros((1, tt), I32)
    for g in range(1, N_GROUPS):
        score_g, i1_g, i2_g = _top2_of_group(biased[g * EXPERTS_PER_GROUP:(g + 1) * EXPERTS_PER_GROUP], sub)
        better = score_g > best
        best = jnp.where(better, score_g, best)
        g_sel = jnp.where(better, g, g_sel)
        i1 = jnp.where(better, i1_g, i1)
        i2 = jnp.where(better, i2_g, i2)
    eid = lax.broadcasted_iota(I32, (N_EXPERTS, tt), 0)
    valid = t * tt + lax.broadcasted_iota(I32, (1, tt), 1) < n_tok
    picks = [g_sel * EXPERTS_PER_GROUP + i1, g_sel * EXPERTS_PER_GROUP + i2]
    hit = [eid == e for e in picks]
    raw = [jnp.sum(jnp.where(h, scores, 0.0), axis=0, keepdims=True) for h in hit]
    denom = raw[0] + raw[1]
    base = base_ref[...]
    for k in range(TOP_K):
        onehot = jnp.where(hit[k] & valid, 1.0, 0.0)
        before = jnp.dot(onehot.astype(MXU_DTYPE), tri_ref[...], preferred_element_type=F32)
        rank = jnp.sum(onehot * (base + before), axis=0, keepdims=True)
        base = base + jnp.sum(onehot, axis=1, keepdims=True)
        e_ref[k:k + 1, :] = picks[k]
        w_ref[k:k + 1, :] = raw[k] / denom
        r_ref[k:k + 1, :] = rank.astype(I32)
    base_ref[...] = base
    cnt_ref[...] = jnp.broadcast_to(base, cnt_ref.shape)


def _row_tokens_body(dest_ref, o_ref, *, n_tok, k_stride):
    def clear(r, c):
        o_ref[r] = 0
        return c

    def put(tok, c):
        for k in range(TOP_K):
            o_ref[dest_ref[k * k_stride + tok]] = tok
        return c

    lax.fori_loop(0, o_ref.shape[0], clear, 0, unroll=GATHER_UNROLL)
    lax.fori_loop(0, n_tok, put, 0, unroll=GATHER_UNROLL)


def _row_tokens(dest, n_tok, n_rows):
    return pl.pallas_call(
        functools.partial(_row_tokens_body, n_tok=n_tok, k_stride=dest.shape[0] // TOP_K),
        in_specs=[pl.BlockSpec(memory_space=pltpu.SMEM)],
        out_specs=pl.BlockSpec(memory_space=pltpu.SMEM),
        out_shape=jax.ShapeDtypeStruct((n_rows,), I32),
        name="row_tokens",
    )(dest)


def _route(logits, b_router, n_tok, n_rows):
    tm = TM_EXPERT
    tt = ROUTE_TILE
    n_steps = -(-n_tok // tt)
    n_pad = n_steps * tt
    pick_spec = pl.BlockSpec((TOP_K, tt), lambda t: (0, t))
    e_idx, wts, rank, cnt = pl.pallas_call(
        functools.partial(_route_body, n_tok=n_tok),
        grid=(n_steps,),
        in_specs=[pl.BlockSpec((tt, LANES), lambda t: (t, 0)), pl.BlockSpec((N_EXPERTS, 1), lambda t: (0, 0))],
        out_specs=[pick_spec, pick_spec, pick_spec, pl.BlockSpec((N_EXPERTS, LANES), lambda t: (0, 0))],
        out_shape=[jax.ShapeDtypeStruct((TOP_K, n_pad), I32), jax.ShapeDtypeStruct((TOP_K, n_pad), F32),
                   jax.ShapeDtypeStruct((TOP_K, n_pad), I32), jax.ShapeDtypeStruct((N_EXPERTS, LANES), F32)],
        scratch_shapes=[pltpu.VMEM((tt, tt), MXU_DTYPE), pltpu.VMEM((N_EXPERTS, 1), F32)],
        compiler_params=_params("arbitrary"),
        name="route",
    )(logits, b_router.astype(F32).reshape(N_EXPERTS, 1))
    counts = cnt[:, 0].astype(I32)
    padded = (counts + tm - 1) // tm * tm
    pend = jnp.cumsum(padded)
    pstart = pend - padded
    expert_ids = jnp.arange(N_EXPERTS, dtype=I32)[:, None, None]
    dest = rank + jnp.sum(jnp.where(e_idx[None] == expert_ids, pstart[:, None, None], 0), axis=0)
    row_tok = _row_tokens(dest.reshape(-1), n_tok, n_rows)
    n_tiles = n_rows // tm
    n_used = (pend[-1] // tm).astype(I32)
    tile_start = jnp.minimum(jnp.arange(n_tiles, dtype=I32), n_used - 1) * tm
    tile_e = jnp.minimum(jnp.searchsorted(pend, tile_start, side="right"), N_EXPERTS - 1).astype(I32)
    return row_tok, tile_e, n_used.reshape(1), dest.reshape(-1), wts[:, :n_tok].T


def _rotary_tables(pos):
    inv_freq = ROPE_THETA ** (-jnp.arange(ROT_HALF, dtype=F32) / ROT_HALF)
    ang = pos.astype(F32)[:, None] * inv_freq[None, :]
    ones = jnp.ones((pos.shape[0], LANES - ROT_DIM), F32)
    cos_t = jnp.concatenate([jnp.cos(ang), jnp.cos(ang), ones], axis=1)
    sin_t = jnp.concatenate([jnp.sin(ang), jnp.sin(ang), 0.0 * ones], axis=1)
    return cos_t, sin_t


def kernel(x_prompt, x_sample, cache_k, cache_v, state_conv, page_table, c_prompt, c_sample,
           w_ada, b_ada, ln_g, ln_b, w_qkv, w_o, conv_w_in, conv_w_dw, conv_ln_g, conv_ln_b,
           conv_w_out, w_router, b_router, w_gate, w_up, w_down):
    batch, seq, d = x_prompt.shape
    db, dec_seq, _ = x_sample.shape
    assert dec_seq == 1 and seq % TM == 0 and TM % MOBA_BLOCK == 0 and d == N_HEADS * HEAD_DIM
    depth = w_ada.shape[0]
    n_pages = page_table.shape[1]
    past_len = n_pages * PAGE_SIZE
    assert n_pages % KMEAN_PAGES == 0 and n_pages % PAGES_PER_BLOCK == 0
    n_full = n_pages // PAGES_PER_BLOCK
    topk_s = min(MOBA_TOPK, n_full)
    alpha = (2 * depth) ** 0.25
    n_p = batch * seq
    n_all = n_p + db
    assert n_p % db == 0 and n_p % ROUTE_TILE == 0 and ROUTE_TILE % TM == 0
    n_buf = -(-n_all // ROUTE_TILE) * ROUTE_TILE
    tps = seq // TM

    c_rows = -(-(batch + db) // 8) * 8
    c_all = jnp.concatenate([c_prompt, c_sample, jnp.zeros((c_rows - batch - db, d), F32)], axis=0)
    mod = _ada(c_all, w_ada, b_ada)

    wr_pad = jnp.pad(w_router, ((0, 0), (0, LANES - N_EXPERTS)))
    n_assign = n_all * TOP_K
    n_rows = -(-(n_assign + N_EXPERTS * (TM_EXPERT - 1)) // TM_EXPERT) * TM_EXPERT

    cos_p, sin_p = _rotary_tables(jnp.arange(seq))
    cos_s, sin_s = _rotary_tables(past_len + jnp.zeros((db,), I32))

    xp = x_prompt.reshape(n_p, d)
    xs = x_sample.reshape(db, d)
    kp_pages, vp_pages, ks_rows, vs_rows, conv_p, conv_s = [], [], [], [], [], []
    for i in range(depth):
        mp = [m[:, None, :] for m in jnp.split(mod[i, :batch], 6, axis=-1)]
        ms = [m[None] for m in jnp.split(mod[i, batch:batch + db], 6, axis=-1)]
        if i % 2 == 0:
            ia = i // 2
            q, kp, vp, kx, v16, km = _qkv_prompt(xp, mp[0], mp[1], w_qkv[ia], cos_p, sin_p, batch, seq)
            km = km.transpose(0, 2, 1, 3, 4).reshape(batch, N_KV_HEADS, seq // MOBA_BLOCK, HEAD_DIM)
            attn_p = _moba_prompt(q, kx, v16, km, batch, seq)
            kp_pages.append(kp)
            vp_pages.append(vp)

            qkv_s = _qkv_sample(xs, ms[0][0], ms[1][0], w_qkv[ia], cos_s, sin_s)
            nq = N_HEADS * HEAD_DIM
            nk = N_KV_HEADS * HEAD_DIM
            q_s = qkv_s[:, :nq].reshape(db, N_HEADS, HEAD_DIM)
            k_s = qkv_s[:, nq:nq + nk].reshape(db, N_KV_HEADS, HEAD_DIM)
            v_s = qkv_s[:, nq + nk:].reshape(db, N_KV_HEADS, HEAD_DIM)
            ks_rows.append(k_s[:, :, None, :])
            vs_rows.append(v_s[:, :, None, :])
            if topk_s > 0:
                kmean_s = _kmean_sample(page_table, cache_k, ia)
                kvh = jnp.arange(N_HEADS) // GQA_GROUP
                gate_s = jnp.einsum("bhd,bnhd->bhn", q_s, kmean_s[:, :, kvh], precision=lax.Precision.HIGHEST)
                _, sel = lax.top_k(gate_s, topk_s)
                sel_pages = (sel[..., None] * PAGES_PER_BLOCK + jnp.arange(PAGES_PER_BLOCK)).reshape(db, N_HEADS, -1)
                phys = jnp.take_along_axis(page_table[:, None, :], sel_pages, axis=2).astype(I32)
                attn_s = _attn_sample(phys.reshape(-1), q_s.reshape(db * N_HEADS, 1, HEAD_DIM),
                                      k_s.reshape(db * N_KV_HEADS, 1, HEAD_DIM),
                                      v_s.reshape(db * N_KV_HEADS, 1, HEAD_DIM), cache_k, cache_v, ia)
                attn_s = attn_s.reshape(db, nq)
            else:
                attn_s = jnp.repeat(v_s, GQA_GROUP, axis=1).reshape(db, nq)
            mixer_p, mixer_s, w_mix = attn_p, attn_s, w_o[ia]
            mode_p, mode_s = "attn", "attn"
        else:
            ic = i // 2
            u_p = _glu(xp, mp[0], mp[1], conv_w_in[ic], TM, tps, False)
            u_s = _glu(xs, ms[0], ms[1], conv_w_in[ic], db, 1, True)
            cg, cb = conv_ln_g[ic][None], conv_ln_b[ic][None]
            mixer_p = (u_p, conv_w_dw[ic], cg, cb)
            mixer_s = (u_s, state_conv[ic].transpose(1, 0, 2), conv_w_dw[ic], cg, cb)
            w_mix = conv_w_out[ic]
            mode_p, mode_s = "conv", "conv_step"
            conv_p.append(u_p.reshape(batch, seq, d)[:, seq - (CONV_WIDTH - 1):])
            conv_s.append(jnp.concatenate([state_conv[ic][:, 1:], u_s[:, None, :]], axis=1))

        x1p, h2_all, lg_all = _post(mode_p, mixer_p, w_mix, xp, mp[2], ln_g[i, 0][None], ln_b[i, 0][None],
                                    mp[3], mp[4], wr_pad, tm=TM, rows_per_mod=tps, three_pass=False, alpha=alpha,
                                    seq=seq, out_rows=n_buf)
        x1s, h2_all, lg_all = _post(mode_s, mixer_s, w_mix, xs, ms[2], ln_g[i, 0][None], ln_b[i, 0][None],
                                    ms[3], ms[4], wr_pad, tm=db, rows_per_mod=1, three_pass=True, alpha=alpha,
                                    into=(h2_all, lg_all, n_p // db))
        row_tok, tile_e, n_used, dest, wts = _route(lg_all, b_router, n_all, n_rows)
        ys = _experts(row_tok, tile_e, n_used, h2_all, w_gate, w_up, w_down, i)
        xp = _combine(dest, ys, x1p, wts[:n_p], mp[5], ln_g[i, 1][None], ln_b[i, 1][None],
                      tm=TM_COMBINE, rows_per_mod=seq // TM_COMBINE, alpha=alpha, tok0=0)
        xs = _combine(dest, ys, x1s, wts[n_p:], ms[5], ln_g[i, 1][None], ln_b[i, 1][None],
                      tm=db, rows_per_mod=1, alpha=alpha, tok0=n_p)

    k_prompt = jnp.concatenate(kp_pages, axis=2)
    v_prompt = jnp.concatenate(vp_pages, axis=2)
    return (xp.reshape(batch, seq, d), xs.reshape(db, 1, d), k_prompt, v_prompt, jnp.stack(conv_p, axis=0),
            jnp.stack(ks_rows, axis=1), jnp.stack(vs_rows, axis=1), jnp.stack(conv_s, axis=0))
```

```python
import functools
import math

import jax
import jax.numpy as jnp
from jax import lax
from jax.experimental import pallas as pl
from jax.experimental.pallas import tpu as pltpu

F32 = jnp.float32
I32 = jnp.int32
MXU_DTYPE = jnp.bfloat16

N_HEADS = 8
N_KV_HEADS = 2
GQA_GROUP = N_HEADS // N_KV_HEADS
HEAD_DIM = 128
ROT_DIM = HEAD_DIM // 4
ROT_HALF = ROT_DIM // 2
ROPE_THETA = 500000.0
ATTN_SCALE = HEAD_DIM ** -0.5
MOBA_BLOCK = 256
MOBA_TOPK = 3
MOBA_CHUNK = 128
EXP2_SCALE = ATTN_SCALE * math.log2(math.e)
PAGE_SIZE = 128
PAGES_PER_BLOCK = MOBA_BLOCK // PAGE_SIZE
CONV_WIDTH = 31
CONV_HALO = 32
N_EXPERTS = 32
N_GROUPS = 4
EXPERTS_PER_GROUP = N_EXPERTS // N_GROUPS
TOP_K = 2
LN_EPS = 1e-5
MASK_VALUE = -1e30
LANES = 128
VMEM_LIMIT = 56 * 1024 * 1024

TM = 512
TM_EXPERT = 256
TM_COMBINE = 256
KMEAN_PAGES = 32
ROUTE_TILE = 512
GATHER_UNROLL = 8


def _params(*sem, row_gather=False):
    return pltpu.CompilerParams(dimension_semantics=sem, vmem_limit_bytes=VMEM_LIMIT,
                                disable_bounds_checks=row_gather)


def _dot(a, b):
    return jnp.dot(a.astype(MXU_DTYPE), b.astype(MXU_DTYPE), preferred_element_type=F32)


def _dot_nt(a, b):
    return lax.dot_general(a.astype(MXU_DTYPE), b.astype(MXU_DTYPE), (((1,), (1,)), ((), ())),
                           preferred_element_type=F32)


def _store_row_tiles(ref, val):
    rows, d = val.shape
    per_row = d // LANES
    for s in range(per_row):
        ref[pl.ds(s, rows, stride=per_row), :] = val[:, s * LANES:(s + 1) * LANES]


def _load_row_tiles(ref, per_row):
    rows = ref.shape[0] // per_row
    return jnp.concatenate([ref[pl.ds(s, rows, stride=per_row), :] for s in range(per_row)], axis=1)


def _layer_norm(z, g, b):
    mu = jnp.mean(z, axis=-1, keepdims=True)
    zc = z - mu
    var = jnp.mean(zc * zc, axis=-1, keepdims=True)
    return zc * lax.rsqrt(var + LN_EPS) * g + b


def _silu(x):
    return x * jax.nn.sigmoid(x)


def _rotary(xc, cos, sin, lane):
    x_up = pltpu.roll(xc, LANES - ROT_HALF, axis=1)
    x_dn = pltpu.roll(xc, ROT_HALF, axis=1)
    first = xc * cos - x_up * sin
    second = xc * cos + x_dn * sin
    return jnp.where(lane < ROT_HALF, first, jnp.where(lane < ROT_DIM, second, xc))


def _ada_body(c_ref, w_ref, b_ref, o_ref):
    o_ref[0] = _dot(_silu(c_ref[...]), w_ref[0]) + b_ref[0]


def _ada(c_all, w_ada, b_ada):
    depth, d, n6 = w_ada.shape
    rows = c_all.shape[0]
    tn = 1536
    return pl.pallas_call(
        _ada_body,
        grid=(depth, n6 // tn),
        in_specs=[pl.BlockSpec((rows, d), lambda i, j: (0, 0)),
                  pl.BlockSpec((1, d, tn), lambda i, j: (i, 0, j)),
                  pl.BlockSpec((1, 1, tn), lambda i, j: (i, 0, j))],
        out_specs=pl.BlockSpec((1, rows, tn), lambda i, j: (i, 0, j)),
        out_shape=jax.ShapeDtypeStruct((depth, rows, n6), F32),
        compiler_params=_params("arbitrary", "arbitrary"),
        name="ada",
    )(c_all, w_ada, b_ada.reshape(depth, 1, n6))


def _qkv_prompt_body(x_ref, sh_ref, sc_ref, w_ref, cos_ref, sin_ref,
                     q_ref, kp_ref, vp_ref, kx_ref, v16_ref, km_ref, w16, *, tiles_per_seq):
    t = pl.program_id(0)
    tm = x_ref.shape[0]

    @pl.when(t == 0)
    def _():
        w16[...] = w_ref[...].astype(MXU_DTYPE)

    h = x_ref[...] * (1 + sc_ref[0]) + sh_ref[0]
    qkv = jnp.dot(h.astype(MXU_DTYPE), w16[...], preferred_element_type=F32)
    cos = cos_ref[...]
    sin = sin_ref[...]
    lane = lax.broadcasted_iota(I32, (tm, LANES), 1)
    nq = N_HEADS * HEAD_DIM
    nk = N_KV_HEADS * HEAD_DIM
    for hh in range(N_HEADS):
        sl = slice(hh * HEAD_DIM, (hh + 1) * HEAD_DIM)
        q_ref[:, sl] = _rotary(qkv[:, sl], cos, sin, lane).astype(q_ref.dtype)
    row = lax.broadcasted_iota(I32, (tm, LANES), 0)
    blk = ((t % tiles_per_seq) * tm + row) // MOBA_BLOCK
    onehot = jnp.where(lane == blk, 1.0, 0.0).astype(kx_ref.dtype)
    for n in range(N_KV_HEADS):
        kc = _rotary(qkv[:, nq + n * HEAD_DIM:nq + (n + 1) * HEAD_DIM], cos, sin, lane)
        vc = qkv[:, nq + nk + n * HEAD_DIM:nq + nk + (n + 1) * HEAD_DIM]
        kp_ref[0, :, 0, n] = kc.reshape(tm // PAGE_SIZE, PAGE_SIZE, HEAD_DIM)
        vp_ref[0, :, 0, n] = vc.reshape(tm // PAGE_SIZE, PAGE_SIZE, HEAD_DIM)
        kx_ref[0, n, :, 0:HEAD_DIM] = kc.astype(kx_ref.dtype)
        kx_ref[0, n, :, HEAD_DIM:2 * HEAD_DIM] = onehot
        v16_ref[0, n] = vc.astype(v16_ref.dtype)
        km_ref[0, 0, n] = jnp.sum(kc.reshape(tm // MOBA_BLOCK, MOBA_BLOCK, HEAD_DIM), axis=1) * (1.0 / MOBA_BLOCK)


def _qkv_prompt(x, shift, scale, w_qkv, cos_t, sin_t, batch, seq):
    n, d = x.shape
    tm = TM
    tps = seq // tm
    width = w_qkv.shape[1]
    npg = seq // PAGE_SIZE
    mod_spec = pl.BlockSpec((1, 1, d), lambda t: (t // tps, 0, 0))
    rot_spec = pl.BlockSpec((tm, LANES), lambda t: (t % tps, 0))
    page_spec = pl.BlockSpec((1, tm // PAGE_SIZE, 1, N_KV_HEADS, PAGE_SIZE, HEAD_DIM),
                             lambda t: (t // tps, t % tps, 0, 0, 0, 0))
    page_shape = jax.ShapeDtypeStruct((batch, npg, 1, N_KV_HEADS, PAGE_SIZE, HEAD_DIM), F32)
    return pl.pallas_call(
        functools.partial(_qkv_prompt_body, tiles_per_seq=tps),
        grid=(n // tm,),
        in_specs=[pl.BlockSpec((tm, d), lambda t: (t, 0)), mod_spec, mod_spec,
                  pl.BlockSpec((d, width), lambda t: (0, 0)), rot_spec, rot_spec],
        out_specs=[pl.BlockSpec((tm, N_HEADS * HEAD_DIM), lambda t: (t, 0)),
                   page_spec, page_spec,
                   pl.BlockSpec((1, N_KV_HEADS, tm, 2 * HEAD_DIM), lambda t: (t // tps, 0, t % tps, 0)),
                   pl.BlockSpec((1, N_KV_HEADS, tm, HEAD_DIM), lambda t: (t // tps, 0, t % tps, 0)),
                   pl.BlockSpec((1, 1, N_KV_HEADS, tm // MOBA_BLOCK, HEAD_DIM), lambda t: (t // tps, t % tps, 0, 0, 0))],
        out_shape=[jax.ShapeDtypeStruct((n, N_HEADS * HEAD_DIM), MXU_DTYPE),
                   page_shape, page_shape,
                   jax.ShapeDtypeStruct((batch, N_KV_HEADS, seq, 2 * HEAD_DIM), MXU_DTYPE),
                   jax.ShapeDtypeStruct((batch, N_KV_HEADS, seq, HEAD_DIM), MXU_DTYPE),
                   jax.ShapeDtypeStruct((batch, tps, N_KV_HEADS, tm // MOBA_BLOCK, HEAD_DIM), F32)],
        scratch_shapes=[pltpu.VMEM((d, width), MXU_DTYPE)],
        compiler_params=_params("arbitrary"),
        name="qkv_prompt",
    )(x, shift, scale, w_qkv, cos_t, sin_t)


def _qkv_sample_body(x_ref, sh_ref, sc_ref, w_ref, cos_ref, sin_ref, o_ref):
    rows = x_ref.shape[0]
    h = x_ref[...] * (1 + sc_ref[...]) + sh_ref[...]
    qkv = _dot(h, w_ref[...])
    lane = lax.broadcasted_iota(I32, (rows, LANES), 1)
    n_rot = N_HEADS + N_KV_HEADS
    for c in range(n_rot):
        sl = slice(c * HEAD_DIM, (c + 1) * HEAD_DIM)
        o_ref[:, sl] = _rotary(qkv[:, sl], cos_ref[...], sin_ref[...], lane)
    o_ref[:, n_rot * HEAD_DIM:] = qkv[:, n_rot * HEAD_DIM:]


def _qkv_sample(x, shift, scale, w_qkv, cos_t, sin_t):
    rows, d = x.shape
    width = w_qkv.shape[1]
    full = lambda shape: pl.BlockSpec(shape, lambda: tuple(0 for _ in shape))
    return pl.pallas_call(
        _qkv_sample_body,
        in_specs=[full((rows, d)), full((rows, d)), full((rows, d)), full((d, width)),
                  full((rows, LANES)), full((rows, LANES))],
        out_specs=full((rows, width)),
        out_shape=jax.ShapeDtypeStruct((rows, width), F32),
        compiler_params=pltpu.CompilerParams(vmem_limit_bytes=VMEM_LIMIT),
        name="qkv_sample",
    )(x, shift, scale, w_qkv, cos_t, sin_t)


def _moba_prompt_body(q_ref, kx_ref, v_ref, km_ref, o_ref, qx_ref, *state):
    i = pl.program_id(2)
    rows = GQA_GROUP * MOBA_BLOCK
    n_chunks = rows // MOBA_CHUNK
    s_refs, p_refs, a_refs, acc_ref = state[0:2], state[2:4], state[4:6], state[6]
    m_refs, l_refs = state[7:7 + n_chunks], state[7 + n_chunks:]
    n_blk = km_ref.shape[2]
    for h in range(GQA_GROUP):
        qx_ref[h * MOBA_BLOCK:(h + 1) * MOBA_BLOCK, 0:HEAD_DIM] = q_ref[:, h * HEAD_DIM:(h + 1) * HEAD_DIM]

    gate = _dot_nt(km_ref[0, 0], qx_ref[:, 0:HEAD_DIM])
    blk = lax.broadcasted_iota(I32, (n_blk, rows), 0)
    valid = blk < i
    cand = jnp.where(valid, gate, -jnp.inf)
    sel = blk == i
    for _ in range(MOBA_TOPK):
        top = jnp.max(cand, axis=0, keepdims=True)
        idx = jnp.min(jnp.where(cand == top, blk, n_blk), axis=0, keepdims=True)
        pick = blk == idx
        sel = sel | (pick & valid)
        cand = jnp.where(pick, -jnp.inf, cand)
    bias = jnp.where(sel, 0.0, MASK_VALUE)
    if n_blk < LANES:
        bias = jnp.concatenate([bias, jnp.zeros((LANES - n_blk, rows), F32)], axis=0)
    qx_ref[:, HEAD_DIM:2 * HEAD_DIM] = bias.T.astype(qx_ref.dtype)

    half = MOBA_BLOCK // 2

    def issue_scores(j, slot):
        start = pl.multiple_of(j * MOBA_BLOCK, MOBA_BLOCK)
        s_refs[slot][...] = _dot_nt(qx_ref[...], kx_ref[0, 0, pl.ds(start, MOBA_BLOCK), :])

    def softmax(slot, own):
        for c in range(n_chunks):
            rs = slice(c * MOBA_CHUNK, (c + 1) * MOBA_CHUNK)
            s = s_refs[slot][rs, :]
            if own:
                qpos = (c * MOBA_CHUNK) % MOBA_BLOCK + lax.broadcasted_iota(I32, (MOBA_CHUNK, MOBA_BLOCK), 0)
                kpos = lax.broadcasted_iota(I32, (MOBA_CHUNK, MOBA_BLOCK), 1)
                s = jnp.where(kpos <= qpos, s, MASK_VALUE)
            sa, sb = s[:, :half], s[:, half:]
            top = jnp.broadcast_to(jnp.max(jnp.maximum(sa, sb), axis=1, keepdims=True), (MOBA_CHUNK, half))
            if own:
                m_new = top
            else:
                m_old = m_refs[c][...]
                m_new = jnp.maximum(m_old, top)
                alpha = jnp.exp2((m_old - m_new) * EXP2_SCALE)
                a_refs[slot][rs, :] = alpha
            pa = jnp.exp2((sa - m_new) * EXP2_SCALE)
            pb = jnp.exp2((sb - m_new) * EXP2_SCALE)
            p_refs[slot][rs, :half] = pa.astype(MXU_DTYPE)
            p_refs[slot][rs, half:] = pb.astype(MXU_DTYPE)
            if own:
                l_refs[c][...] = pa + pb
            else:
                l_refs[c][...] = alpha * l_refs[c][...] + (pa + pb)
            m_refs[c][...] = m_new

    def accumulate(j, slot, own):
        start = pl.multiple_of(j * MOBA_BLOCK, MOBA_BLOCK)
        pv = jnp.dot(p_refs[slot][...], v_ref[0, 0, pl.ds(start, MOBA_BLOCK), :], preferred_element_type=F32)
        if own:
            acc_ref[...] = pv
        else:
            acc_ref[...] = a_refs[slot][...] * acc_ref[...] + pv

    issue_scores(i, 1)
    issue_scores(0, 0)
    softmax(1, True)
    accumulate(i, 1, True)

    def pair(t, carry):
        j = 2 * t
        issue_scores(j + 1, 1)
        softmax(0, False)
        accumulate(j, 0, False)
        issue_scores(jnp.minimum(j + 2, i - 1), 0)
        softmax(1, False)
        accumulate(j + 1, 1, False)
        return carry

    lax.fori_loop(0, i // 2, pair, 0)

    @pl.when(i % 2 == 1)
    def _():
        softmax(0, False)
        accumulate(i - 1, 0, False)

    per_head = MOBA_BLOCK // MOBA_CHUNK
    for c in range(n_chunks):
        rs = slice(c * MOBA_CHUNK, (c + 1) * MOBA_CHUNK)
        out = acc_ref[rs, :] / jnp.sum(l_refs[c][...], axis=1, keepdims=True)
        h, part = divmod(c, per_head)
        o_ref[part * MOBA_CHUNK:(part + 1) * MOBA_CHUNK, h * HEAD_DIM:(h + 1) * HEAD_DIM] = out.astype(o_ref.dtype)


def _moba_prompt(q, kx, v16, k_mean, batch, seq):
    n = q.shape[0]
    nq = seq // MOBA_BLOCK
    rows = GQA_GROUP * MOBA_BLOCK
    gw = GQA_GROUP * HEAD_DIM
    return pl.pallas_call(
        _moba_prompt_body,
        grid=(batch, N_KV_HEADS, nq),
        in_specs=[pl.BlockSpec((MOBA_BLOCK, gw), lambda b, g, i: (b * nq + i, g)),
                  pl.BlockSpec((1, 1, seq, 2 * HEAD_DIM), lambda b, g, i: (b, g, 0, 0)),
                  pl.BlockSpec((1, 1, seq, HEAD_DIM), lambda b, g, i: (b, g, 0, 0)),
                  pl.BlockSpec((1, 1, nq, HEAD_DIM), lambda b, g, i: (b, g, 0, 0))],
        out_specs=pl.BlockSpec((MOBA_BLOCK, gw), lambda b, g, i: (b * nq + i, g)),
        out_shape=jax.ShapeDtypeStruct((n, N_HEADS * HEAD_DIM), MXU_DTYPE),
        scratch_shapes=([pltpu.VMEM((rows, 2 * HEAD_DIM), MXU_DTYPE)]
                        + [pltpu.VMEM((rows, MOBA_BLOCK), F32)] * 2
                        + [pltpu.VMEM((rows, MOBA_BLOCK), MXU_DTYPE)] * 2
                        + [pltpu.VMEM((rows, MOBA_BLOCK // 2), F32)] * 2
                        + [pltpu.VMEM((rows, HEAD_DIM), F32)]
                        + [pltpu.VMEM((MOBA_CHUNK, MOBA_BLOCK // 2), F32)] * (2 * (rows // MOBA_CHUNK))),
        compiler_params=_params("arbitrary", "arbitrary", "arbitrary"),
        name="moba_prompt",
    )(q, kx, v16, k_mean)


def _kmean_copy(pt_ref, ck_hbm, buf, sem, layer, step, slot, p):
    phys = pt_ref[step * KMEAN_PAGES + p]
    return pltpu.make_async_copy(ck_hbm.at[phys, layer], buf.at[slot, p], sem.at[slot])


def _kmean_sample_body(pt_ref, ck_hbm, o_ref, buf, sem, *, layer):
    nc = pl.num_programs(1)
    step = pl.program_id(0) * nc + pl.program_id(1)
    total = pl.num_programs(0) * nc
    slot = step % 2

    def start(step_, slot_):
        for p in range(KMEAN_PAGES):
            _kmean_copy(pt_ref, ck_hbm, buf, sem, layer, step_, slot_, p).start()

    @pl.when(step == 0)
    def _():
        start(step, slot)

    @pl.when(step + 1 < total)
    def _():
        start(step + 1, 1 - slot)

    for p in range(KMEAN_PAGES):
        _kmean_copy(pt_ref, ck_hbm, buf, sem, layer, step, slot, p).wait()
    for jb in range(KMEAN_PAGES // PAGES_PER_BLOCK):
        for n in range(N_KV_HEADS):
            acc = jnp.zeros((1, HEAD_DIM), F32)
            for pp in range(PAGES_PER_BLOCK):
                acc = acc + jnp.sum(buf[slot, jb * PAGES_PER_BLOCK + pp, n], axis=0, keepdims=True)
            o_ref[0, jb, pl.ds(n, 1), :] = acc * (1.0 / MOBA_BLOCK)


def _kmean_sample(page_table, cache_k, layer):
    db, n_pages = page_table.shape
    n_full = n_pages // PAGES_PER_BLOCK
    page_shape = cache_k.shape[2:]
    bps = KMEAN_PAGES // PAGES_PER_BLOCK
    return pl.pallas_call(
        functools.partial(_kmean_sample_body, layer=layer),
        grid_spec=pltpu.PrefetchScalarGridSpec(
            num_scalar_prefetch=1,
            grid=(db, n_pages // KMEAN_PAGES),
            in_specs=[pl.BlockSpec(memory_space=pl.ANY)],
            out_specs=pl.BlockSpec((1, bps, N_KV_HEADS, HEAD_DIM), lambda b, c, pt: (b, c, 0, 0)),
            scratch_shapes=[pltpu.VMEM((2, KMEAN_PAGES) + page_shape, F32),
                            pltpu.SemaphoreType.DMA((2,))]),
        out_shape=jax.ShapeDtypeStruct((db, n_full, N_KV_HEADS, HEAD_DIM), F32),
        compiler_params=_params("arbitrary", "arbitrary"),
        name="kmean_sample",
    )(page_table.reshape(-1), cache_k)


def _attn_sample_body(pages_ref, q_ref, kn_ref, vn_ref, *refs):
    n_sel = (len(refs) - 1) // 2
    k_refs, v_refs, o_ref = refs[:n_sel], refs[n_sel:2 * n_sel], refs[-1]
    q = q_ref[0]
    q8 = jnp.broadcast_to(q, (8, HEAD_DIM))
    s_new = jnp.sum(q * kn_ref[0], axis=1, keepdims=True) * ATTN_SCALE
    s_sel = [_dot_nt(q8, k_ref[0, 0, 0])[0:1] * ATTN_SCALE for k_ref in k_refs]
    m = s_new
    for s in s_sel:
        m = jnp.maximum(m, jnp.max(s, axis=1, keepdims=True))
    p_new = jnp.exp(s_new - m)
    denom = p_new
    out = p_new * vn_ref[0]
    for s, v_ref in zip(s_sel, v_refs):
        p = jnp.exp(s - m)
        denom = denom + jnp.sum(p, axis=1, keepdims=True)
        out = out + _dot(jnp.broadcast_to(p, (8, PAGE_SIZE)), v_ref[0, 0, 0])[0:1]
    o_ref[0] = out / denom


def _attn_sample(phys, q, k_new, v_new, cache_k, cache_v, layer):
    db = q.shape[0] // N_HEADS
    n_sel = phys.shape[0] // (db * N_HEADS)
    vec = lambda f: pl.BlockSpec((1, 1, HEAD_DIM), f)

    def page_spec(p):
        return pl.BlockSpec((1, 1, 1, PAGE_SIZE, HEAD_DIM),
                            lambda b, h, pg: (pg[(b * N_HEADS + h) * n_sel + p], layer, h // GQA_GROUP, 0, 0))

    page_specs = [page_spec(p) for p in range(n_sel)]
    return pl.pallas_call(
        _attn_sample_body,
        grid_spec=pltpu.PrefetchScalarGridSpec(
            num_scalar_prefetch=1,
            grid=(db, N_HEADS),
            in_specs=[vec(lambda b, h, pg: (b * N_HEADS + h, 0, 0)),
                      vec(lambda b, h, pg: (b * N_KV_HEADS + h // GQA_GROUP, 0, 0)),
                      vec(lambda b, h, pg: (b * N_KV_HEADS + h // GQA_GROUP, 0, 0))] + page_specs + page_specs,
            out_specs=vec(lambda b, h, pg: (b * N_HEADS + h, 0, 0))),
        out_shape=jax.ShapeDtypeStruct((db * N_HEADS, 1, HEAD_DIM), F32),
        compiler_params=_params("arbitrary", "arbitrary"),
        name="attn_sample",
    )(phys, q, k_new, v_new, *([cache_k] * n_sel), *([cache_v] * n_sel))


def _glu_body(x_ref, sh_ref, sc_ref, w_ref, o_ref, w16):
    c = o_ref.shape[1]
    h = x_ref[...] * (1 + sc_ref[0]) + sh_ref[0]

    @pl.when(pl.program_id(0) == 0)
    def _():
        w16[...] = w_ref[...].astype(MXU_DTYPE)

    ag = jnp.dot(h.astype(MXU_DTYPE), w16[...], preferred_element_type=F32)
    o_ref[...] = ag[:, :c] * jax.nn.sigmoid(ag[:, c:])


def _glu(x, shift, scale, w_in, tm, rows_per_mod):
    n, d = x.shape
    c2 = w_in.shape[1]
    r = shift.shape[1]
    mod_spec = pl.BlockSpec((1, r, d), lambda t: (t // rows_per_mod, 0, 0))
    return pl.pallas_call(
        _glu_body,
        grid=(n // tm,),
        in_specs=[pl.BlockSpec((tm, d), lambda t: (t, 0)), mod_spec, mod_spec,
                  pl.BlockSpec((d, c2), lambda t: (0, 0))],
        out_specs=pl.BlockSpec((tm, c2 // 2), lambda t: (t, 0)),
        out_shape=jax.ShapeDtypeStruct((n, c2 // 2), F32),
        scratch_shapes=[pltpu.VMEM((d, c2), MXU_DTYPE)],
        compiler_params=_params("arbitrary"),
        name="glu",
    )(x, shift, scale, w_in)


def _post_body(*refs, n_tiles, **kw):
    n_mixer = 1 if kw["mode"] == "attn" else 5
    h2_ref, lg_ref = refs[n_mixer + 9], refs[n_mixer + 10]
    t = pl.program_id(0)

    @pl.when(t < n_tiles)
    def _():
        _post_tile(*refs, **kw)

    @pl.when(t >= n_tiles)
    def _():
        h2_ref[...] = jnp.zeros(h2_ref.shape, F32)
        lg_ref[...] = jnp.zeros(lg_ref.shape, F32)


def _post_tile(*refs, mode, alpha, tiles_per_seq):
    if mode == "attn":
        a_ref, rest = refs[0], refs[1:]
    elif mode == "conv":
        u_ref, prev_ref, wdw_ref, cg_ref, cb_ref = refs[:5]
        rest = refs[5:]
    else:
        u_ref, st_ref, wdw_ref, cg_ref, cb_ref = refs[:5]
        rest = refs[5:]
    (w_ref, x_ref, g1_ref, lng_ref, lnb_ref, sh2_ref, sc2_ref, wr_ref,
     x1_ref, h2_ref, lg_ref, w16) = rest[:12]
    t = pl.program_id(0)
    tm = x_ref.shape[0]

    if mode == "attn":
        a = a_ref[...]
    else:
        if mode == "conv":
            ext_ref = rest[12]
            first = (t % tiles_per_seq) == 0
            ext_ref[0:CONV_HALO] = jnp.where(first, 0.0, prev_ref[...])
            ext_ref[CONV_HALO:] = u_ref[...]
            off = CONV_HALO - (CONV_WIDTH - 1)
            y = ext_ref[pl.ds(off, tm), :] * wdw_ref[0:1, :]
            for w in range(1, CONV_WIDTH):
                y = y + ext_ref[pl.ds(off + w, tm), :] * wdw_ref[w:w + 1, :]
        else:
            y = u_ref[...] * wdw_ref[CONV_WIDTH - 1:CONV_WIDTH, :]
            for w in range(CONV_WIDTH - 1):
                y = y + st_ref[w] * wdw_ref[w:w + 1, :]
        a = _silu(_layer_norm(y, cg_ref[...], cb_ref[...]))

    @pl.when(t == 0)
    def _():
        w16[...] = w_ref[...].astype(MXU_DTYPE)

    f = jnp.dot(a.astype(MXU_DTYPE), w16[...], preferred_element_type=F32)
    x1 = _layer_norm(alpha * x_ref[...] + g1_ref[0] * f, lng_ref[...], lnb_ref[...])
    h2 = x1 * (1 + sc2_ref[0]) + sh2_ref[0]
    x1_ref[...] = x1
    _store_row_tiles(h2_ref, h2)
    lg_ref[...] = _dot(h2, wr_ref[...])


def _post(mode, mixer_in, w, x, gate1, ln_g, ln_b, shift2, scale2, w_router, *, tm, rows_per_mod,
          alpha, seq=None, out_rows=None, into=None):
    n, d = x.shape
    out_rows = n if out_rows is None else out_rows
    r = gate1.shape[1]
    tps = None if seq is None else seq // tm
    n_tiles = n // tm
    n_steps = -(-out_rows // tm)
    last = n_tiles - 1
    row = lambda t: (jnp.minimum(t, last), 0)
    const = lambda t: (0, 0)
    mod_spec = pl.BlockSpec((1, r, d), lambda t: (jnp.minimum(t, last) // rows_per_mod, 0, 0))
    vec_spec = pl.BlockSpec((1, d), const)
    if mode == "attn":
        mixer_specs = [pl.BlockSpec((tm, d), row)]
    elif mode == "conv":
        u, w_dw, cg, cb = mixer_in
        per = tm // CONV_HALO
        mixer_in = (u, u, w_dw, cg, cb)
        mixer_specs = [pl.BlockSpec((tm, d), row),
                       pl.BlockSpec((CONV_HALO, d), lambda t: (jnp.maximum(jnp.minimum(t, last) * per - 1, 0), 0)),
                       pl.BlockSpec(w_dw.shape, const), vec_spec, vec_spec]
    else:
        u, state, w_dw, cg, cb = mixer_in
        mixer_specs = [pl.BlockSpec((tm, d), row), pl.BlockSpec(state.shape, lambda t: (0, 0, 0)),
                       pl.BlockSpec(w_dw.shape, const), vec_spec, vec_spec]
    if mode == "attn":
        mixer_in = (mixer_in,)
    in_specs = mixer_specs + [pl.BlockSpec(w.shape, const), pl.BlockSpec((tm, d), row), mod_spec, vec_spec, vec_spec,
                              mod_spec, mod_spec, pl.BlockSpec(w_router.shape, const)]
    args = list(mixer_in) + [w, x, gate1, ln_g, ln_b, shift2, scale2, w_router]
    out_shape = [jax.ShapeDtypeStruct((n, d), F32)]
    out_specs = [pl.BlockSpec((tm, d), row)]
    aliases = {}
    if into is None:
        out_shape += [jax.ShapeDtypeStruct((out_rows * (d // LANES), LANES), F32),
                      jax.ShapeDtypeStruct((out_rows, LANES), F32)]
        out_specs += [pl.BlockSpec((tm * (d // LANES), LANES), lambda t: (t, 0)),
                      pl.BlockSpec((tm, LANES), lambda t: (t, 0))]
    else:
        h2_all, lg_all, row_block = into
        out_shape += [jax.ShapeDtypeStruct(h2_all.shape, F32), jax.ShapeDtypeStruct(lg_all.shape, F32)]
        out_specs += [pl.BlockSpec((tm * (d // LANES), LANES), lambda t: (row_block + t, 0)),
                      pl.BlockSpec((tm, LANES), lambda t: (row_block + t, 0))]
        aliases = {len(args): 1, len(args) + 1: 2}
        in_specs += [pl.BlockSpec(memory_space=pl.ANY), pl.BlockSpec(memory_space=pl.ANY)]
        args += [h2_all, lg_all]
    scratch = [pltpu.VMEM(w.shape, MXU_DTYPE)]
    if mode == "conv":
        scratch.append(pltpu.VMEM((CONV_HALO + tm, d), F32))

    def body(*refs):
        if into is not None:
            n_in = len(args)
            refs = refs[:n_in - 2] + refs[n_in:]
        _post_body(*refs, n_tiles=n_tiles, mode=mode, alpha=alpha, tiles_per_seq=tps)

    return pl.pallas_call(
        body,
        grid=(n_steps,),
        in_specs=in_specs,
        out_specs=out_specs,
        out_shape=out_shape,
        scratch_shapes=scratch,
        input_output_aliases=aliases,
        compiler_params=_params("arbitrary"),
        name="post_" + mode,
    )(*args)


def _row_tile(ref, r, per_row):
    return ref.at[pl.ds(pl.multiple_of(r * per_row, per_row), per_row)]


def _gather_rows(idx_ref, base, src_hbm, dst, sem, n, per_row):
    def issue(r, c):
        pltpu.make_async_copy(_row_tile(src_hbm, idx_ref[base + r], per_row), _row_tile(dst, r, per_row), sem).start()
        return c

    lax.fori_loop(0, n, issue, 0, unroll=GATHER_UNROLL)


def _wait_rows(src_hbm, dst, sem, n, per_row):
    def drain(r, c):
        pltpu.make_async_copy(_row_tile(src_hbm, 0, per_row), _row_tile(dst, r, per_row), sem).wait()
        return c

    lax.fori_loop(0, n, drain, 0, unroll=GATHER_UNROLL)


def _expert_body(tok_ref, te_ref, nu_ref, h_hbm, wg_ref, wu_ref, wd_ref, o_ref,
                 xbuf, wg16, wu16, wd16, sem):
    i = pl.program_id(0)
    per_row = wg16.shape[0] // LANES
    tm = o_ref.shape[0] // per_row
    n_used = nu_ref[0]
    slot = i % 2

    @pl.when(i == 0)
    def _():
        _gather_rows(tok_ref, 0, h_hbm, xbuf.at[0], sem.at[0], tm, per_row)

    @pl.when(i + 1 < n_used)
    def _():
        _gather_rows(tok_ref, (i + 1) * tm, h_hbm, xbuf.at[1 - slot], sem.at[1 - slot], tm, per_row)

    @pl.when(i < n_used)
    def _():
        changed = jnp.logical_or(i == 0, te_ref[i] != te_ref[jnp.maximum(i - 1, 0)])

        @pl.when(changed)
        def _():
            wg16[...] = wg_ref[0].astype(MXU_DTYPE)
            wu16[...] = wu_ref[0].astype(MXU_DTYPE)
            wd16[...] = wd_ref[0].astype(MXU_DTYPE)

        _wait_rows(h_hbm, xbuf.at[slot], sem.at[slot], tm, per_row)
        x = _load_row_tiles(xbuf.at[slot], per_row).astype(MXU_DTYPE)
        g = jnp.dot(x, wg16[...], preferred_element_type=F32)
        u = jnp.dot(x, wu16[...], preferred_element_type=F32)
        _store_row_tiles(o_ref, jnp.dot((_silu(g) * u).astype(MXU_DTYPE), wd16[...], preferred_element_type=F32))

    @pl.when(i >= n_used)
    def _():
        o_ref[...] = jnp.zeros(o_ref.shape, F32)


def _experts(row_tok, tile_e, n_used, h_all, w_gate, w_up, w_down, layer):
    n_rows = row_tok.shape[0]
    tm = TM_EXPERT
    d, de = w_gate.shape[-2:]
    per_row = d // LANES
    wspec_in = pl.BlockSpec((None, 1, d, de), lambda i, tok, te, nu: (layer, te[i], 0, 0))
    wspec_out = pl.BlockSpec((None, 1, de, d), lambda i, tok, te, nu: (layer, te[i], 0, 0))
    return pl.pallas_call(
        _expert_body,
        grid_spec=pltpu.PrefetchScalarGridSpec(
            num_scalar_prefetch=3,
            grid=(n_rows // tm,),
            in_specs=[pl.BlockSpec(memory_space=pl.ANY), wspec_in, wspec_in, wspec_out],
            out_specs=pl.BlockSpec((tm * per_row, LANES), lambda i, tok, te, nu: (i, 0)),
            scratch_shapes=[pltpu.VMEM((2, tm * per_row, LANES), F32),
                            pltpu.VMEM((d, de), MXU_DTYPE), pltpu.VMEM((d, de), MXU_DTYPE),
                            pltpu.VMEM((de, d), MXU_DTYPE), pltpu.SemaphoreType.DMA((2,))]),
        out_shape=jax.ShapeDtypeStruct((n_rows * per_row, LANES), F32),
        compiler_params=_params("arbitrary", row_gather=True),
        name="experts",
    )(row_tok, tile_e, n_used, h_all, w_gate, w_up, w_down)


def _combine_body(dest_ref, ys_hbm, x_ref, w_ref, g_ref, lng_ref, lnb_ref, o_ref, buf, sem, *, alpha, tok0, k_stride):
    t = pl.program_id(0)
    tm, d = x_ref.shape
    per_row = d // LANES
    slot = t % 2

    def gather(tile, slot_):
        for k in range(TOP_K):
            _gather_rows(dest_ref, k * k_stride + tok0 + tile * tm, ys_hbm, buf.at[slot_, k], sem.at[slot_, k], tm,
                         per_row)

    @pl.when(t == 0)
    def _():
        gather(0, 0)

    @pl.when(t + 1 < pl.num_programs(0))
    def _():
        gather(t + 1, 1 - slot)

    for k in range(TOP_K):
        _wait_rows(ys_hbm, buf.at[slot, k], sem.at[slot, k], tm, per_row)
    f = _load_row_tiles(buf.at[slot, 0], per_row) * w_ref[:, 0:1]
    for k in range(1, TOP_K):
        f = f + _load_row_tiles(buf.at[slot, k], per_row) * w_ref[:, k:k + 1]
    o_ref[...] = _layer_norm(alpha * x_ref[...] + g_ref[0] * f, lng_ref[...], lnb_ref[...])


def _combine(dest, ys, x, wts, gate2, ln_g, ln_b, *, tm, rows_per_mod, alpha, tok0):
    n, d = x.shape
    r = gate2.shape[1]
    vec_spec = pl.BlockSpec((1, d), lambda t, ds_: (0, 0))
    return pl.pallas_call(
        functools.partial(_combine_body, alpha=alpha, tok0=tok0, k_stride=dest.shape[0] // TOP_K),
        grid_spec=pltpu.PrefetchScalarGridSpec(
            num_scalar_prefetch=1,
            grid=(n // tm,),
            in_specs=[pl.BlockSpec(memory_space=pl.ANY),
                      pl.BlockSpec((tm, d), lambda t, ds_: (t, 0)),
                      pl.BlockSpec((tm, TOP_K), lambda t, ds_: (t, 0)),
                      pl.BlockSpec((1, r, d), lambda t, ds_: (t // rows_per_mod, 0, 0)),
                      vec_spec, vec_spec],
            out_specs=pl.BlockSpec((tm, d), lambda t, ds_: (t, 0)),
            scratch_shapes=[pltpu.VMEM((2, TOP_K, tm * (d // LANES), LANES), F32),
                            pltpu.SemaphoreType.DMA((2, TOP_K))]),
        out_shape=jax.ShapeDtypeStruct((n, d), F32),
        compiler_params=_params("arbitrary", row_gather=True),
        name="combine",
    )(dest, ys, x, wts, gate2, ln_g, ln_b)


def _top2_of_group(x, sub):
    m1 = jnp.max(x, axis=0, keepdims=True)
    i1 = jnp.min(jnp.where(x == m1, sub, EXPERTS_PER_GROUP), axis=0, keepdims=True)
    rest = jnp.where(sub == i1, -jnp.inf, x)
    m2 = jnp.max(rest, axis=0, keepdims=True)
    i2 = jnp.min(jnp.where(rest == m2, sub, EXPERTS_PER_GROUP), axis=0, keepdims=True)
    return m1 + m2, i1, i2


def _route_body(lg_ref, b_ref, e_ref, w_ref, r_ref, cnt_ref, tri_ref, base_ref, *, n_tok):
    t = pl.program_id(0)
    tt = lg_ref.shape[0]

    @pl.when(t == 0)
    def _():
        earlier = lax.broadcasted_iota(I32, (tt, tt), 0) < lax.broadcasted_iota(I32, (tt, tt), 1)
        tri_ref[...] = jnp.where(earlier, 1.0, 0.0).astype(MXU_DTYPE)
        base_ref[...] = jnp.zeros(base_ref.shape, F32)

    scores = jax.nn.sigmoid(lg_ref[...].T[:N_EXPERTS])
    biased = scores + b_ref[...]
    sub = lax.broadcasted_iota(I32, (EXPERTS_PER_GROUP, tt), 0)
    best, i1, i2 = _top2_of_group(biased[:EXPERTS_PER_GROUP], sub)
    g_sel = jnp.zeros((1, tt), I32)
    for g in range(1, N_GROUPS):
        score_g, i1_g, i2_g = _top2_of_group(biased[g * EXPERTS_PER_GROUP:(g + 1) * EXPERTS_PER_GROUP], sub)
        better = score_g > best
        best = jnp.where(better, score_g, best)
        g_sel = jnp.where(better, g, g_sel)
        i1 = jnp.where(better, i1_g, i1)
        i2 = jnp.where(better, i2_g, i2)
    eid = lax.broadcasted_iota(I32, (N_EXPERTS, tt), 0)
    valid = t * tt + lax.broadcasted_iota(I32, (1, tt), 1) < n_tok
    picks = [g_sel * EXPERTS_PER_GROUP + i1, g_sel * EXPERTS_PER_GROUP + i2]
    hit = [eid == e for e in picks]
    raw = [jnp.sum(jnp.where(h, scores, 0.0), axis=0, keepdims=True) for h in hit]
    denom = raw[0] + raw[1]
    base = base_ref[...]
    for k in range(TOP_K):
        onehot = jnp.where(hit[k] & valid, 1.0, 0.0)
        before = jnp.dot(onehot.astype(MXU_DTYPE), tri_ref[...], preferred_element_type=F32)
        rank = jnp.sum(onehot * (base + before), axis=0, keepdims=True)
        base = base + jnp.sum(onehot, axis=1, keepdims=True)
        e_ref[k:k + 1, :] = picks[k]
        w_ref[k:k + 1, :] = raw[k] / denom
        r_ref[k:k + 1, :] = rank.astype(I32)
    base_ref[...] = base
    cnt_ref[...] = jnp.broadcast_to(base, cnt_ref.shape)


def _row_tokens_body(dest_ref, o_ref, *, n_tok, k_stride):
    def clear(r, c):
        o_ref[r] = 0
        return c

    def put(tok, c):
        for k in range(TOP_K):
            o_ref[dest_ref[k * k_stride + tok]] = tok
        return c

    lax.fori_loop(0, o_ref.shape[0], clear, 0, unroll=GATHER_UNROLL)
    lax.fori_loop(0, n_tok, put, 0, unroll=GATHER_UNROLL)


def _row_tokens(dest, n_tok, n_rows):
    return pl.pallas_call(
        functools.partial(_row_tokens_body, n_tok=n_tok, k_stride=dest.shape[0] // TOP_K),
        in_specs=[pl.BlockSpec(memory_space=pltpu.SMEM)],
        out_specs=pl.BlockSpec(memory_space=pltpu.SMEM),
        out_shape=jax.ShapeDtypeStruct((n_rows,), I32),
        name="row_tokens",
    )(dest)


def _route(logits, b_router, n_tok, n_rows):
    tm = TM_EXPERT
    tt = ROUTE_TILE
    n_steps = -(-n_tok // tt)
    n_pad = n_steps * tt
    pick_spec = pl.BlockSpec((TOP_K, tt), lambda t: (0, t))
    e_idx, wts, rank, cnt = pl.pallas_call(
        functools.partial(_route_body, n_tok=n_tok),
        grid=(n_steps,),
        in_specs=[pl.BlockSpec((tt, LANES), lambda t: (t, 0)), pl.BlockSpec((N_EXPERTS, 1), lambda t: (0, 0))],
        out_specs=[pick_spec, pick_spec, pick_spec, pl.BlockSpec((N_EXPERTS, LANES), lambda t: (0, 0))],
        out_shape=[jax.ShapeDtypeStruct((TOP_K, n_pad), I32), jax.ShapeDtypeStruct((TOP_K, n_pad), F32),
                   jax.ShapeDtypeStruct((TOP_K, n_pad), I32), jax.ShapeDtypeStruct((N_EXPERTS, LANES), F32)],
        scratch_shapes=[pltpu.VMEM((tt, tt), MXU_DTYPE), pltpu.VMEM((N_EXPERTS, 1), F32)],
        compiler_params=_params("arbitrary"),
        name="route",
    )(logits, b_router.astype(F32).reshape(N_EXPERTS, 1))
    counts = cnt[:, 0].astype(I32)
    padded = (counts + tm - 1) // tm * tm
    pend = jnp.cumsum(padded)
    pstart = pend - padded
    expert_ids = jnp.arange(N_EXPERTS, dtype=I32)[:, None, None]
    dest = rank + jnp.sum(jnp.where(e_idx[None] == expert_ids, pstart[:, None, None], 0), axis=0)
    row_tok = _row_tokens(dest.reshape(-1), n_tok, n_rows)
    n_tiles = n_rows // tm
    n_used = (pend[-1] // tm).astype(I32)
    tile_start = jnp.minimum(jnp.arange(n_tiles, dtype=I32), n_used - 1) * tm
    tile_e = jnp.minimum(jnp.sum(pend[None, :] <= tile_start[:, None], axis=1), N_EXPERTS - 1).astype(I32)
    return row_tok, tile_e, n_used.reshape(1), dest.reshape(-1), wts[:, :n_tok].T


def _rotary_tables(pos):
    inv_freq = ROPE_THETA ** (-jnp.arange(ROT_HALF, dtype=F32) / ROT_HALF)
    ang = pos.astype(F32)[:, None] * inv_freq[None, :]
    ones = jnp.ones((pos.shape[0], LANES - ROT_DIM), F32)
    cos_t = jnp.concatenate([jnp.cos(ang), jnp.cos(ang), ones], axis=1)
    sin_t = jnp.concatenate([jnp.sin(ang), jnp.sin(ang), 0.0 * ones], axis=1)
    return cos_t, sin_t


def kernel(x_prompt, x_sample, cache_k, cache_v, state_conv, page_table, c_prompt, c_sample,
           w_ada, b_ada, ln_g, ln_b, w_qkv, w_o, conv_w_in, conv_w_dw, conv_ln_g, conv_ln_b,
           conv_w_out, w_router, b_router, w_gate, w_up, w_down):
    batch, seq, d = x_prompt.shape
    db, dec_seq, _ = x_sample.shape
    assert dec_seq == 1 and seq % TM == 0 and TM % MOBA_BLOCK == 0 and d == N_HEADS * HEAD_DIM
    depth = w_ada.shape[0]
    n_pages = page_table.shape[1]
    past_len = n_pages * PAGE_SIZE
    assert n_pages % KMEAN_PAGES == 0 and n_pages % PAGES_PER_BLOCK == 0
    n_full = n_pages // PAGES_PER_BLOCK
    topk_s = min(MOBA_TOPK, n_full)
    alpha = (2 * depth) ** 0.25
    n_p = batch * seq
    n_all = n_p + db
    assert n_p % db == 0 and n_p % ROUTE_TILE == 0 and ROUTE_TILE % TM == 0
    n_buf = -(-n_all // ROUTE_TILE) * ROUTE_TILE
    tps = seq // TM

    c_rows = -(-(batch + db) // 8) * 8
    c_all = jnp.concatenate([c_prompt, c_sample, jnp.zeros((c_rows - batch - db, d), F32)], axis=0)
    mod = _ada(c_all, w_ada, b_ada)

    wr_pad = jnp.pad(w_router, ((0, 0), (0, LANES - N_EXPERTS))).astype(MXU_DTYPE)
    n_assign = n_all * TOP_K
    n_rows = -(-(n_assign + N_EXPERTS * (TM_EXPERT - 1)) // TM_EXPERT) * TM_EXPERT

    cos_p, sin_p = _rotary_tables(jnp.arange(seq))
    cos_s, sin_s = _rotary_tables(past_len + jnp.zeros((db,), I32))

    xp = x_prompt.reshape(n_p, d)
    xs = x_sample.reshape(db, d)
    kp_pages, vp_pages, ks_rows, vs_rows, conv_p, conv_s = [], [], [], [], [], []
    for i in range(depth):
        mp = [m[:, None, :] for m in jnp.split(mod[i, :batch], 6, axis=-1)]
        ms = [m[None] for m in jnp.split(mod[i, batch:batch + db], 6, axis=-1)]
        if i % 2 == 0:
            ia = i // 2
            q, kp, vp, kx, v16, km = _qkv_prompt(xp, mp[0], mp[1], w_qkv[ia], cos_p, sin_p, batch, seq)
            km = km.transpose(0, 2, 1, 3, 4).reshape(batch, N_KV_HEADS, seq // MOBA_BLOCK, HEAD_DIM)
            attn_p = _moba_prompt(q, kx, v16, km, batch, seq)
            kp_pages.append(kp)
            vp_pages.append(vp)

            qkv_s = _qkv_sample(xs, ms[0][0], ms[1][0], w_qkv[ia], cos_s, sin_s)
            nq = N_HEADS * HEAD_DIM
            nk = N_KV_HEADS * HEAD_DIM
            q_s = qkv_s[:, :nq].reshape(db, N_HEADS, HEAD_DIM)
            k_s = qkv_s[:, nq:nq + nk].reshape(db, N_KV_HEADS, HEAD_DIM)
            v_s = qkv_s[:, nq + nk:].reshape(db, N_KV_HEADS, HEAD_DIM)
            ks_rows.append(k_s[:, :, None, :])
            vs_rows.append(v_s[:, :, None, :])
            if topk_s > 0:
                kmean_s = _kmean_sample(page_table, cache_k, ia)
                kvh = jnp.arange(N_HEADS) // GQA_GROUP
                gate_s = jnp.einsum("bhd,bnhd->bhn", q_s, kmean_s[:, :, kvh])
                _, sel = lax.top_k(gate_s, topk_s)
                sel_pages = (sel[..., None] * PAGES_PER_BLOCK + jnp.arange(PAGES_PER_BLOCK)).reshape(db, N_HEADS, -1)
                phys = jnp.take_along_axis(page_table[:, None, :], sel_pages, axis=2).astype(I32)
                attn_s = _attn_sample(phys.reshape(-1), q_s.reshape(db * N_HEADS, 1, HEAD_DIM),
                                      k_s.reshape(db * N_KV_HEADS, 1, HEAD_DIM),
                                      v_s.reshape(db * N_KV_HEADS, 1, HEAD_DIM), cache_k, cache_v, ia)
                attn_s = attn_s.reshape(db, nq)
            else:
                attn_s = jnp.repeat(v_s, GQA_GROUP, axis=1).reshape(db, nq)
            mixer_p, mixer_s, w_mix = attn_p, attn_s, w_o[ia]
            mode_p, mode_s = "attn", "attn"
        else:
            ic = i // 2
            u_p = _glu(xp, mp[0], mp[1], conv_w_in[ic], TM, tps)
            u_s = _glu(xs, ms[0], ms[1], conv_w_in[ic], db, 1)
            cg, cb = conv_ln_g[ic][None], conv_ln_b[ic][None]
            mixer_p = (u_p, conv_w_dw[ic], cg, cb)
            mixer_s = (u_s, state_conv[ic].transpose(1, 0, 2), conv_w_dw[ic], cg, cb)
            w_mix = conv_w_out[ic]
            mode_p, mode_s = "conv", "conv_step"
            conv_p.append(u_p.reshape(batch, seq, d)[:, seq - (CONV_WIDTH - 1):])
            conv_s.append(jnp.concatenate([state_conv[ic][:, 1:], u_s[:, None, :]], axis=1))

        x1p, h2_all, lg_all = _post(mode_p, mixer_p, w_mix, xp, mp[2], ln_g[i, 0][None], ln_b[i, 0][None],
                                    mp[3], mp[4], wr_pad, tm=TM, rows_per_mod=tps, alpha=alpha,
                                    seq=seq, out_rows=n_buf)
        x1s, h2_all, lg_all = _post(mode_s, mixer_s, w_mix, xs, ms[2], ln_g[i, 0][None], ln_b[i, 0][None],
                                    ms[3], ms[4], wr_pad, tm=db, rows_per_mod=1, alpha=alpha,
                                    into=(h2_all, lg_all, n_p // db))
        row_tok, tile_e, n_used, dest, wts = _route(lg_all, b_router, n_all, n_rows)
        ys = _experts(row_tok, tile_e, n_used, h2_all, w_gate, w_up, w_down, i)
        xp = _combine(dest, ys, x1p, wts[:n_p], mp[5], ln_g[i, 1][None], ln_b[i, 1][None],
                      tm=TM_COMBINE, rows_per_mod=seq // TM_COMBINE, alpha=alpha, tok0=0)
        xs = _combine(dest, ys, x1s, wts[n_p:], ms[5], ln_g[i, 1][None], ln_b[i, 1][None],
                      tm=db, rows_per_mod=1, alpha=alpha, tok0=n_p)

    k_prompt = jnp.concatenate(kp_pages, axis=2)
    v_prompt = jnp.concatenate(vp_pages, axis=2)
    return (xp.reshape(batch, seq, d), xs.reshape(db, 1, d), k_prompt, v_prompt, jnp.stack(conv_p, axis=0),
            jnp.stack(ks_rows, axis=1), jnp.stack(vs_rows, axis=1), jnp.stack(conv_s, axis=0))
```

```python
import functools
import math

import jax
import jax.numpy as jnp
from jax import lax
from jax.experimental import pallas as pl
from jax.experimental.pallas import tpu as pltpu

F32 = jnp.float32
I32 = jnp.int32
MXU_DTYPE = jnp.bfloat16

N_HEADS = 8
N_KV_HEADS = 2
GQA_GROUP = N_HEADS // N_KV_HEADS
HEAD_DIM = 128
ROT_DIM = HEAD_DIM // 4
ROT_HALF = ROT_DIM // 2
ROPE_THETA = 500000.0
ATTN_SCALE = HEAD_DIM ** -0.5
MOBA_BLOCK = 256
MOBA_TOPK = 3
MOBA_CHUNK = 128
EXP2_SCALE = ATTN_SCALE * math.log2(math.e)
PAGE_SIZE = 128
PAGES_PER_BLOCK = MOBA_BLOCK // PAGE_SIZE
CONV_WIDTH = 31
CONV_HALO = 32
N_EXPERTS = 32
N_GROUPS = 4
EXPERTS_PER_GROUP = N_EXPERTS // N_GROUPS
TOP_K = 2
LN_EPS = 1e-5
MASK_VALUE = -1e30
LANES = 128
VMEM_LIMIT = 56 * 1024 * 1024

TM = 512
TM_EXPERT = 256
TM_COMBINE = 256
KMEAN_PAGES = 32
ROUTE_TILE = 512
GATHER_UNROLL = 8
EXPERT_GATHER_PRIORITIES = (1,)
COMBINE_GATHER_PRIORITIES = (0, 1)


def _params(*sem, row_gather=False):
    return pltpu.CompilerParams(dimension_semantics=sem, vmem_limit_bytes=VMEM_LIMIT,
                                disable_bounds_checks=row_gather)


def _dot(a, b):
    return jnp.dot(a.astype(MXU_DTYPE), b.astype(MXU_DTYPE), preferred_element_type=F32)


def _dot_nt(a, b):
    return lax.dot_general(a.astype(MXU_DTYPE), b.astype(MXU_DTYPE), (((1,), (1,)), ((), ())),
                           preferred_element_type=F32)


def _mxu_round(x):
    return x.astype(MXU_DTYPE).astype(F32)


def _store_row_tiles(ref, val):
    rows, d = val.shape
    per_row = d // LANES
    for s in range(per_row):
        ref[pl.ds(s, rows, stride=per_row), :] = val[:, s * LANES:(s + 1) * LANES]


def _load_row_tiles(ref, per_row):
    rows = ref.shape[0] // per_row
    return jnp.concatenate([ref[pl.ds(s, rows, stride=per_row), :] for s in range(per_row)], axis=1)


def _layer_norm(z, g, b):
    mu = jnp.mean(z, axis=-1, keepdims=True)
    zc = z - mu
    var = jnp.mean(zc * zc, axis=-1, keepdims=True)
    return zc * lax.rsqrt(var + LN_EPS) * g + b


def _silu(x):
    return x * jax.nn.sigmoid(x)


def _rotary(xc, cos, sin, lane):
    x_up = pltpu.roll(xc, LANES - ROT_HALF, axis=1)
    x_dn = pltpu.roll(xc, ROT_HALF, axis=1)
    first = xc * cos - x_up * sin
    second = xc * cos + x_dn * sin
    return jnp.where(lane < ROT_HALF, first, jnp.where(lane < ROT_DIM, second, xc))


def _ada_body(c_ref, w_ref, b_ref, o_ref):
    o_ref[0] = _dot(_silu(c_ref[...]), w_ref[0]) + b_ref[0]


def _ada(c_all, w_ada, b_ada):
    depth, d, n6 = w_ada.shape
    rows = c_all.shape[0]
    tn = 1536
    return pl.pallas_call(
        _ada_body,
        grid=(depth, n6 // tn),
        in_specs=[pl.BlockSpec((rows, d), lambda i, j: (0, 0)),
                  pl.BlockSpec((1, d, tn), lambda i, j: (i, 0, j)),
                  pl.BlockSpec((1, 1, tn), lambda i, j: (i, 0, j))],
        out_specs=pl.BlockSpec((1, rows, tn), lambda i, j: (i, 0, j)),
        out_shape=jax.ShapeDtypeStruct((depth, rows, n6), F32),
        compiler_params=_params("arbitrary", "arbitrary"),
        name="ada",
    )(c_all, w_ada, b_ada.reshape(depth, 1, n6))


def _qkv_prompt_body(x_ref, sh_ref, sc_ref, w_ref, cos_ref, sin_ref,
                     q_ref, kp_ref, vp_ref, kx_ref, v16_ref, km_ref, w16, *, tiles_per_seq):
    t = pl.program_id(0)
    tm = x_ref.shape[0]

    @pl.when(t == 0)
    def _():
        w16[...] = w_ref[...].astype(MXU_DTYPE)

    h = x_ref[...] * (1 + sc_ref[0]) + sh_ref[0]
    qkv = jnp.dot(h.astype(MXU_DTYPE), w16[...], preferred_element_type=F32)
    cos = cos_ref[...]
    sin = sin_ref[...]
    lane = lax.broadcasted_iota(I32, (tm, LANES), 1)
    nq = N_HEADS * HEAD_DIM
    nk = N_KV_HEADS * HEAD_DIM
    for hh in range(N_HEADS):
        sl = slice(hh * HEAD_DIM, (hh + 1) * HEAD_DIM)
        q_ref[:, sl] = _rotary(qkv[:, sl], cos, sin, lane).astype(q_ref.dtype)
    row = lax.broadcasted_iota(I32, (tm, LANES), 0)
    blk = ((t % tiles_per_seq) * tm + row) // MOBA_BLOCK
    onehot = jnp.where(lane == blk, 1.0, 0.0).astype(kx_ref.dtype)
    for n in range(N_KV_HEADS):
        kc = _rotary(qkv[:, nq + n * HEAD_DIM:nq + (n + 1) * HEAD_DIM], cos, sin, lane)
        vc = qkv[:, nq + nk + n * HEAD_DIM:nq + nk + (n + 1) * HEAD_DIM]
        kp_ref[0, :, 0, n] = kc.reshape(tm // PAGE_SIZE, PAGE_SIZE, HEAD_DIM)
        vp_ref[0, :, 0, n] = vc.reshape(tm // PAGE_SIZE, PAGE_SIZE, HEAD_DIM)
        kx_ref[0, n, :, 0:HEAD_DIM] = kc.astype(kx_ref.dtype)
        kx_ref[0, n, :, HEAD_DIM:2 * HEAD_DIM] = onehot
        v16_ref[0, n] = vc.astype(v16_ref.dtype)
        km_ref[0, 0, n] = jnp.sum(kc.reshape(tm // MOBA_BLOCK, MOBA_BLOCK, HEAD_DIM), axis=1) * (1.0 / MOBA_BLOCK)


def _qkv_prompt(x, shift, scale, w_qkv, cos_t, sin_t, batch, seq):
    n, d = x.shape
    tm = TM
    tps = seq // tm
    width = w_qkv.shape[1]
    npg = seq // PAGE_SIZE
    mod_spec = pl.BlockSpec((1, 1, d), lambda t: (t // tps, 0, 0))
    rot_spec = pl.BlockSpec((tm, LANES), lambda t: (t % tps, 0))
    page_spec = pl.BlockSpec((1, tm // PAGE_SIZE, 1, N_KV_HEADS, PAGE_SIZE, HEAD_DIM),
                             lambda t: (t // tps, t % tps, 0, 0, 0, 0))
    page_shape = jax.ShapeDtypeStruct((batch, npg, 1, N_KV_HEADS, PAGE_SIZE, HEAD_DIM), F32)
    return pl.pallas_call(
        functools.partial(_qkv_prompt_body, tiles_per_seq=tps),
        grid=(n // tm,),
        in_specs=[pl.BlockSpec((tm, d), lambda t: (t, 0)), mod_spec, mod_spec,
                  pl.BlockSpec((d, width), lambda t: (0, 0)), rot_spec, rot_spec],
        out_specs=[pl.BlockSpec((tm, N_HEADS * HEAD_DIM), lambda t: (t, 0)),
                   page_spec, page_spec,
                   pl.BlockSpec((1, N_KV_HEADS, tm, 2 * HEAD_DIM), lambda t: (t // tps, 0, t % tps, 0)),
                   pl.BlockSpec((1, N_KV_HEADS, tm, HEAD_DIM), lambda t: (t // tps, 0, t % tps, 0)),
                   pl.BlockSpec((1, 1, N_KV_HEADS, tm // MOBA_BLOCK, HEAD_DIM), lambda t: (t // tps, t % tps, 0, 0, 0))],
        out_shape=[jax.ShapeDtypeStruct((n, N_HEADS * HEAD_DIM), MXU_DTYPE),
                   page_shape, page_shape,
                   jax.ShapeDtypeStruct((batch, N_KV_HEADS, seq, 2 * HEAD_DIM), MXU_DTYPE),
                   jax.ShapeDtypeStruct((batch, N_KV_HEADS, seq, HEAD_DIM), MXU_DTYPE),
                   jax.ShapeDtypeStruct((batch, tps, N_KV_HEADS, tm // MOBA_BLOCK, HEAD_DIM), F32)],
        scratch_shapes=[pltpu.VMEM((d, width), MXU_DTYPE)],
        compiler_params=_params("arbitrary"),
        name="qkv_prompt",
    )(x, shift, scale, w_qkv, cos_t, sin_t)


def _qkv_sample_body(x_ref, sh_ref, sc_ref, w_ref, cos_ref, sin_ref, o_ref):
    rows = x_ref.shape[0]
    h = x_ref[...] * (1 + sc_ref[...]) + sh_ref[...]
    qkv = _dot(h, w_ref[...])
    lane = lax.broadcasted_iota(I32, (rows, LANES), 1)
    n_rot = N_HEADS + N_KV_HEADS
    for c in range(n_rot):
        sl = slice(c * HEAD_DIM, (c + 1) * HEAD_DIM)
        o_ref[:, sl] = _rotary(qkv[:, sl], cos_ref[...], sin_ref[...], lane)
    o_ref[:, n_rot * HEAD_DIM:] = qkv[:, n_rot * HEAD_DIM:]


def _qkv_sample(x, shift, scale, w_qkv, cos_t, sin_t):
    rows, d = x.shape
    width = w_qkv.shape[1]
    full = lambda shape: pl.BlockSpec(shape, lambda: tuple(0 for _ in shape))
    return pl.pallas_call(
        _qkv_sample_body,
        in_specs=[full((rows, d)), full((rows, d)), full((rows, d)), full((d, width)),
                  full((rows, LANES)), full((rows, LANES))],
        out_specs=full((rows, width)),
        out_shape=jax.ShapeDtypeStruct((rows, width), F32),
        compiler_params=pltpu.CompilerParams(vmem_limit_bytes=VMEM_LIMIT),
        name="qkv_sample",
    )(x, shift, scale, w_qkv, cos_t, sin_t)


def _moba_prompt_body(q_ref, kx_ref, v_ref, km_ref, o_ref, qx_ref, *state):
    i = pl.program_id(2)
    rows = GQA_GROUP * MOBA_BLOCK
    n_chunks = rows // MOBA_CHUNK
    s_refs, p_refs, a_refs, acc_ref = state[0:2], state[2:4], state[4:6], state[6]
    m_refs, l_refs = state[7:7 + n_chunks], state[7 + n_chunks:]
    n_blk = km_ref.shape[2]
    for h in range(GQA_GROUP):
        qx_ref[h * MOBA_BLOCK:(h + 1) * MOBA_BLOCK, 0:HEAD_DIM] = q_ref[:, h * HEAD_DIM:(h + 1) * HEAD_DIM]

    gate = _dot_nt(km_ref[0, 0], qx_ref[:, 0:HEAD_DIM])
    blk = lax.broadcasted_iota(I32, (n_blk, rows), 0)
    valid = blk < i
    cand = jnp.where(valid, gate, -jnp.inf)
    sel = blk == i
    for _ in range(MOBA_TOPK):
        top = jnp.max(cand, axis=0, keepdims=True)
        idx = jnp.min(jnp.where(cand == top, blk, n_blk), axis=0, keepdims=True)
        pick = blk == idx
        sel = sel | (pick & valid)
        cand = jnp.where(pick, -jnp.inf, cand)
    bias = jnp.where(sel, 0.0, MASK_VALUE)
    if n_blk < LANES:
        bias = jnp.concatenate([bias, jnp.zeros((LANES - n_blk, rows), F32)], axis=0)
    qx_ref[:, HEAD_DIM:2 * HEAD_DIM] = bias.T.astype(qx_ref.dtype)

    half = MOBA_BLOCK // 2

    def issue_scores(j, slot):
        start = pl.multiple_of(j * MOBA_BLOCK, MOBA_BLOCK)
        s_refs[slot][...] = _dot_nt(qx_ref[...], kx_ref[0, 0, pl.ds(start, MOBA_BLOCK), :])

    def softmax(slot, own):
        for c in range(n_chunks):
            rs = slice(c * MOBA_CHUNK, (c + 1) * MOBA_CHUNK)
            s = s_refs[slot][rs, :]
            if own:
                qpos = (c * MOBA_CHUNK) % MOBA_BLOCK + lax.broadcasted_iota(I32, (MOBA_CHUNK, MOBA_BLOCK), 0)
                kpos = lax.broadcasted_iota(I32, (MOBA_CHUNK, MOBA_BLOCK), 1)
                s = jnp.where(kpos <= qpos, s, MASK_VALUE)
            sa, sb = s[:, :half], s[:, half:]
            top = jnp.broadcast_to(jnp.max(jnp.maximum(sa, sb), axis=1, keepdims=True), (MOBA_CHUNK, half))
            if own:
                m_new = top
            else:
                m_old = m_refs[c][...]
                m_new = jnp.maximum(m_old, top)
                alpha = jnp.exp2((m_old - m_new) * EXP2_SCALE)
                a_refs[slot][rs, :] = alpha
            pa = jnp.exp2((sa - m_new) * EXP2_SCALE)
            pb = jnp.exp2((sb - m_new) * EXP2_SCALE)
            p_refs[slot][rs, :half] = pa.astype(MXU_DTYPE)
            p_refs[slot][rs, half:] = pb.astype(MXU_DTYPE)
            if own:
                l_refs[c][...] = pa + pb
            else:
                l_refs[c][...] = alpha * l_refs[c][...] + (pa + pb)
            m_refs[c][...] = m_new

    def accumulate(j, slot, own):
        start = pl.multiple_of(j * MOBA_BLOCK, MOBA_BLOCK)
        pv = jnp.dot(p_refs[slot][...], v_ref[0, 0, pl.ds(start, MOBA_BLOCK), :], preferred_element_type=F32)
        if own:
            acc_ref[...] = pv
        else:
            acc_ref[...] = a_refs[slot][...] * acc_ref[...] + pv

    issue_scores(i, 1)
    issue_scores(0, 0)
    softmax(1, True)
    accumulate(i, 1, True)

    def pair(t, carry):
        j = 2 * t
        issue_scores(j + 1, 1)
        softmax(0, False)
        accumulate(j, 0, False)
        issue_scores(jnp.minimum(j + 2, i - 1), 0)
        softmax(1, False)
        accumulate(j + 1, 1, False)
        return carry

    lax.fori_loop(0, i // 2, pair, 0)

    @pl.when(i % 2 == 1)
    def _():
        softmax(0, False)
        accumulate(i - 1, 0, False)

    per_head = MOBA_BLOCK // MOBA_CHUNK
    for c in range(n_chunks):
        rs = slice(c * MOBA_CHUNK, (c + 1) * MOBA_CHUNK)
        out = acc_ref[rs, :] / jnp.sum(l_refs[c][...], axis=1, keepdims=True)
        h, part = divmod(c, per_head)
        o_ref[part * MOBA_CHUNK:(part + 1) * MOBA_CHUNK, h * HEAD_DIM:(h + 1) * HEAD_DIM] = out.astype(o_ref.dtype)


def _moba_prompt(q, kx, v16, k_mean, batch, seq):
    n = q.shape[0]
    nq = seq // MOBA_BLOCK
    rows = GQA_GROUP * MOBA_BLOCK
    gw = GQA_GROUP * HEAD_DIM
    return pl.pallas_call(
        _moba_prompt_body,
        grid=(batch, N_KV_HEADS, nq),
        in_specs=[pl.BlockSpec((MOBA_BLOCK, gw), lambda b, g, i: (b * nq + i, g)),
                  pl.BlockSpec((1, 1, seq, 2 * HEAD_DIM), lambda b, g, i: (b, g, 0, 0)),
                  pl.BlockSpec((1, 1, seq, HEAD_DIM), lambda b, g, i: (b, g, 0, 0)),
                  pl.BlockSpec((1, 1, nq, HEAD_DIM), lambda b, g, i: (b, g, 0, 0))],
        out_specs=pl.BlockSpec((MOBA_BLOCK, gw), lambda b, g, i: (b * nq + i, g)),
        out_shape=jax.ShapeDtypeStruct((n, N_HEADS * HEAD_DIM), MXU_DTYPE),
        scratch_shapes=([pltpu.VMEM((rows, 2 * HEAD_DIM), MXU_DTYPE)]
                        + [pltpu.VMEM((rows, MOBA_BLOCK), F32)] * 2
                        + [pltpu.VMEM((rows, MOBA_BLOCK), MXU_DTYPE)] * 2
                        + [pltpu.VMEM((rows, MOBA_BLOCK // 2), F32)] * 2
                        + [pltpu.VMEM((rows, HEAD_DIM), F32)]
                        + [pltpu.VMEM((MOBA_CHUNK, MOBA_BLOCK // 2), F32)] * (2 * (rows // MOBA_CHUNK))),
        compiler_params=_params("arbitrary", "arbitrary", "arbitrary"),
        name="moba_prompt",
    )(q, kx, v16, k_mean)


def _kmean_copy(pt_ref, ck_hbm, buf, sem, layer, step, slot, p):
    phys = pt_ref[step * KMEAN_PAGES + p]
    return pltpu.make_async_copy(ck_hbm.at[phys, layer], buf.at[slot, p], sem.at[slot])


def _kmean_sample_body(pt_ref, ck_hbm, o_ref, buf, sem, *, layer):
    nc = pl.num_programs(1)
    step = pl.program_id(0) * nc + pl.program_id(1)
    total = pl.num_programs(0) * nc
    slot = step % 2

    def start(step_, slot_):
        for p in range(KMEAN_PAGES):
            _kmean_copy(pt_ref, ck_hbm, buf, sem, layer, step_, slot_, p).start()

    @pl.when(step == 0)
    def _():
        start(step, slot)

    @pl.when(step + 1 < total)
    def _():
        start(step + 1, 1 - slot)

    for p in range(KMEAN_PAGES):
        _kmean_copy(pt_ref, ck_hbm, buf, sem, layer, step, slot, p).wait()
    for jb in range(KMEAN_PAGES // PAGES_PER_BLOCK):
        for n in range(N_KV_HEADS):
            acc = jnp.zeros((1, HEAD_DIM), F32)
            for pp in range(PAGES_PER_BLOCK):
                acc = acc + jnp.sum(buf[slot, jb * PAGES_PER_BLOCK + pp, n], axis=0, keepdims=True)
            o_ref[0, jb, pl.ds(n, 1), :] = acc * (1.0 / MOBA_BLOCK)


def _kmean_sample(page_table, cache_k, layer):
    db, n_pages = page_table.shape
    n_full = n_pages // PAGES_PER_BLOCK
    page_shape = cache_k.shape[2:]
    bps = KMEAN_PAGES // PAGES_PER_BLOCK
    return pl.pallas_call(
        functools.partial(_kmean_sample_body, layer=layer),
        grid_spec=pltpu.PrefetchScalarGridSpec(
            num_scalar_prefetch=1,
            grid=(db, n_pages // KMEAN_PAGES),
            in_specs=[pl.BlockSpec(memory_space=pl.ANY)],
            out_specs=pl.BlockSpec((1, bps, N_KV_HEADS, HEAD_DIM), lambda b, c, pt: (b, c, 0, 0)),
            scratch_shapes=[pltpu.VMEM((2, KMEAN_PAGES) + page_shape, F32),
                            pltpu.SemaphoreType.DMA((2,))]),
        out_shape=jax.ShapeDtypeStruct((db, n_full, N_KV_HEADS, HEAD_DIM), F32),
        compiler_params=_params("arbitrary", "arbitrary"),
        name="kmean_sample",
    )(page_table.reshape(-1), cache_k)


def _attn_sample_body(pages_ref, q_ref, kn_ref, vn_ref, *refs):
    n_sel = (len(refs) - 1) // 2
    k_refs, v_refs, o_ref = refs[:n_sel], refs[n_sel:2 * n_sel], refs[-1]
    q = q_ref[0]
    q8 = jnp.broadcast_to(q, (8, HEAD_DIM))
    s_new = jnp.sum(q * kn_ref[0], axis=1, keepdims=True) * ATTN_SCALE
    s_sel = [_dot_nt(q8, k_ref[0, 0, 0])[0:1] * ATTN_SCALE for k_ref in k_refs]
    m = s_new
    for s in s_sel:
        m = jnp.maximum(m, jnp.max(s, axis=1, keepdims=True))
    p_new = jnp.exp(s_new - m)
    denom = p_new
    out = p_new * vn_ref[0]
    for s, v_ref in zip(s_sel, v_refs):
        p = jnp.exp(s - m)
        denom = denom + jnp.sum(p, axis=1, keepdims=True)
        out = out + _dot(jnp.broadcast_to(p, (8, PAGE_SIZE)), v_ref[0, 0, 0])[0:1]
    o_ref[0] = out / denom


def _attn_sample(phys, q, k_new, v_new, cache_k, cache_v, layer):
    db = q.shape[0] // N_HEADS
    n_sel = phys.shape[0] // (db * N_HEADS)
    vec = lambda f: pl.BlockSpec((1, 1, HEAD_DIM), f)

    def page_spec(p):
        return pl.BlockSpec((1, 1, 1, PAGE_SIZE, HEAD_DIM),
                            lambda b, h, pg: (pg[(b * N_HEADS + h) * n_sel + p], layer, h // GQA_GROUP, 0, 0))

    page_specs = [page_spec(p) for p in range(n_sel)]
    return pl.pallas_call(
        _attn_sample_body,
        grid_spec=pltpu.PrefetchScalarGridSpec(
            num_scalar_prefetch=1,
            grid=(db, N_HEADS),
            in_specs=[vec(lambda b, h, pg: (b * N_HEADS + h, 0, 0)),
                      vec(lambda b, h, pg: (b * N_KV_HEADS + h // GQA_GROUP, 0, 0)),
                      vec(lambda b, h, pg: (b * N_KV_HEADS + h // GQA_GROUP, 0, 0))] + page_specs + page_specs,
            out_specs=vec(lambda b, h, pg: (b * N_HEADS + h, 0, 0))),
        out_shape=jax.ShapeDtypeStruct((db * N_HEADS, 1, HEAD_DIM), F32),
        compiler_params=_params("arbitrary", "arbitrary"),
        name="attn_sample",
    )(phys, q, k_new, v_new, *([cache_k] * n_sel), *([cache_v] * n_sel))


def _glu_body(x_ref, sh_ref, sc_ref, w_ref, o_ref, w16):
    c = o_ref.shape[1]
    h = x_ref[...] * (1 + sc_ref[0]) + sh_ref[0]

    @pl.when(pl.program_id(0) == 0)
    def _():
        w16[...] = w_ref[...].astype(MXU_DTYPE)

    ag = jnp.dot(h.astype(MXU_DTYPE), w16[...], preferred_element_type=F32)
    o_ref[...] = ag[:, :c] * jax.nn.sigmoid(ag[:, c:])


def _glu(x, shift, scale, w_in, tm, rows_per_mod):
    n, d = x.shape
    c2 = w_in.shape[1]
    r = shift.shape[1]
    mod_spec = pl.BlockSpec((1, r, d), lambda t: (t // rows_per_mod, 0, 0))
    return pl.pallas_call(
        _glu_body,
        grid=(n // tm,),
        in_specs=[pl.BlockSpec((tm, d), lambda t: (t, 0)), mod_spec, mod_spec,
                  pl.BlockSpec((d, c2), lambda t: (0, 0))],
        out_specs=pl.BlockSpec((tm, c2 // 2), lambda t: (t, 0)),
        out_shape=jax.ShapeDtypeStruct((n, c2 // 2), F32),
        scratch_shapes=[pltpu.VMEM((d, c2), MXU_DTYPE)],
        compiler_params=_params("arbitrary"),
        name="glu",
    )(x, shift, scale, w_in)


def _post_body(*refs, n_tiles, **kw):
    n_mixer = 1 if kw["mode"] == "attn" else 5
    h2_ref, lg_ref = refs[n_mixer + 9], refs[n_mixer + 10]
    t = pl.program_id(0)

    @pl.when(t < n_tiles)
    def _():
        _post_tile(*refs, **kw)

    @pl.when(t >= n_tiles)
    def _():
        h2_ref[...] = jnp.zeros(h2_ref.shape, F32)
        lg_ref[...] = jnp.zeros(lg_ref.shape, F32)


def _post_tile(*refs, mode, alpha, tiles_per_seq):
    if mode == "attn":
        a_ref, rest = refs[0], refs[1:]
    elif mode == "conv":
        u_ref, prev_ref, wdw_ref, cg_ref, cb_ref = refs[:5]
        rest = refs[5:]
    else:
        u_ref, st_ref, wdw_ref, cg_ref, cb_ref = refs[:5]
        rest = refs[5:]
    (w_ref, x_ref, g1_ref, lng_ref, lnb_ref, sh2_ref, sc2_ref, wr_ref,
     x1_ref, h2_ref, lg_ref, w16) = rest[:12]
    t = pl.program_id(0)
    tm = x_ref.shape[0]

    if mode == "attn":
        a = a_ref[...]
    else:
        wdw = _mxu_round(wdw_ref[...])
        if mode == "conv":
            ext_ref = rest[12]
            first = (t % tiles_per_seq) == 0
            ext_ref[0:CONV_HALO] = _mxu_round(jnp.where(first, 0.0, prev_ref[...]))
            ext_ref[CONV_HALO:] = _mxu_round(u_ref[...])
            off = CONV_HALO - (CONV_WIDTH - 1)
            y = ext_ref[pl.ds(off, tm), :] * wdw[0:1, :]
            for w in range(1, CONV_WIDTH):
                y = y + ext_ref[pl.ds(off + w, tm), :] * wdw[w:w + 1, :]
        else:
            y = _mxu_round(u_ref[...]) * wdw[CONV_WIDTH - 1:CONV_WIDTH, :]
            for w in range(CONV_WIDTH - 1):
                y = y + _mxu_round(st_ref[w]) * wdw[w:w + 1, :]
        a = _silu(_layer_norm(y, cg_ref[...], cb_ref[...]))

    @pl.when(t == 0)
    def _():
        w16[...] = w_ref[...].astype(MXU_DTYPE)

    f = jnp.dot(a.astype(MXU_DTYPE), w16[...], preferred_element_type=F32)
    x1 = _layer_norm(alpha * x_ref[...] + g1_ref[0] * f, lng_ref[...], lnb_ref[...])
    h2 = x1 * (1 + sc2_ref[0]) + sh2_ref[0]
    x1_ref[...] = x1
    _store_row_tiles(h2_ref, h2)
    lg_ref[...] = _dot(h2, wr_ref[...])


def _post(mode, mixer_in, w, x, gate1, ln_g, ln_b, shift2, scale2, w_router, *, tm, rows_per_mod,
          alpha, seq=None, out_rows=None, into=None):
    n, d = x.shape
    out_rows = n if out_rows is None else out_rows
    r = gate1.shape[1]
    tps = None if seq is None else seq // tm
    n_tiles = n // tm
    n_steps = -(-out_rows // tm)
    last = n_tiles - 1
    row = lambda t: (jnp.minimum(t, last), 0)
    const = lambda t: (0, 0)
    mod_spec = pl.BlockSpec((1, r, d), lambda t: (jnp.minimum(t, last) // rows_per_mod, 0, 0))
    vec_spec = pl.BlockSpec((1, d), const)
    if mode == "attn":
        mixer_specs = [pl.BlockSpec((tm, d), row)]
    elif mode == "conv":
        u, w_dw, cg, cb = mixer_in
        per = tm // CONV_HALO
        mixer_in = (u, u, w_dw, cg, cb)
        mixer_specs = [pl.BlockSpec((tm, d), row),
                       pl.BlockSpec((CONV_HALO, d), lambda t: (jnp.maximum(jnp.minimum(t, last) * per - 1, 0), 0)),
                       pl.BlockSpec(w_dw.shape, const), vec_spec, vec_spec]
    else:
        u, state, w_dw, cg, cb = mixer_in
        mixer_specs = [pl.BlockSpec((tm, d), row), pl.BlockSpec(state.shape, lambda t: (0, 0, 0)),
                       pl.BlockSpec(w_dw.shape, const), vec_spec, vec_spec]
    if mode == "attn":
        mixer_in = (mixer_in,)
    in_specs = mixer_specs + [pl.BlockSpec(w.shape, const), pl.BlockSpec((tm, d), row), mod_spec, vec_spec, vec_spec,
                              mod_spec, mod_spec, pl.BlockSpec(w_router.shape, const)]
    args = list(mixer_in) + [w, x, gate1, ln_g, ln_b, shift2, scale2, w_router]
    out_shape = [jax.ShapeDtypeStruct((n, d), F32)]
    out_specs = [pl.BlockSpec((tm, d), row)]
    aliases = {}
    if into is None:
        out_shape += [jax.ShapeDtypeStruct((out_rows * (d // LANES), LANES), F32),
                      jax.ShapeDtypeStruct((out_rows, LANES), F32)]
        out_specs += [pl.BlockSpec((tm * (d // LANES), LANES), lambda t: (t, 0)),
                      pl.BlockSpec((tm, LANES), lambda t: (t, 0))]
    else:
        h2_all, lg_all, row_block = into
        out_shape += [jax.ShapeDtypeStruct(h2_all.shape, F32), jax.ShapeDtypeStruct(lg_all.shape, F32)]
        out_specs += [pl.BlockSpec((tm * (d // LANES), LANES), lambda t: (row_block + t, 0)),
                      pl.BlockSpec((tm, LANES), lambda t: (row_block + t, 0))]
        aliases = {len(args): 1, len(args) + 1: 2}
        in_specs += [pl.BlockSpec(memory_space=pl.ANY), pl.BlockSpec(memory_space=pl.ANY)]
        args += [h2_all, lg_all]
    scratch = [pltpu.VMEM(w.shape, MXU_DTYPE)]
    if mode == "conv":
        scratch.append(pltpu.VMEM((CONV_HALO + tm, d), F32))

    def body(*refs):
        if into is not None:
            n_in = len(args)
            refs = refs[:n_in - 2] + refs[n_in:]
        _post_body(*refs, n_tiles=n_tiles, mode=mode, alpha=alpha, tiles_per_seq=tps)

    return pl.pallas_call(
        body,
        grid=(n_steps,),
        in_specs=in_specs,
        out_specs=out_specs,
        out_shape=out_shape,
        scratch_shapes=scratch,
        input_output_aliases=aliases,
        compiler_params=_params("arbitrary"),
        name="post_" + mode,
    )(*args)


def _row_tile(ref, r, per_row):
    return ref.at[pl.ds(pl.multiple_of(r * per_row, per_row), per_row)]


def _gather_rows(idx_ref, base, src_hbm, dst, sem, n, per_row, priorities):
    def issue(g, c):
        for k in range(GATHER_UNROLL):
            r = g * GATHER_UNROLL + k
            copy = pltpu.make_async_copy(_row_tile(src_hbm, idx_ref[base + r], per_row), _row_tile(dst, r, per_row), sem)
            copy.start(priority=priorities[k % len(priorities)])
        return c

    lax.fori_loop(0, n // GATHER_UNROLL, issue, 0)


def _wait_rows(src_hbm, dst, sem, n, per_row):
    def drain(r, c):
        pltpu.make_async_copy(_row_tile(src_hbm, 0, per_row), _row_tile(dst, r, per_row), sem).wait()
        return c

    lax.fori_loop(0, n, drain, 0, unroll=GATHER_UNROLL)


def _expert_body(tok_ref, te_ref, nu_ref, h_hbm, wg_ref, wu_ref, wd_ref, o_ref,
                 xbuf, wg16, wu16, wd16, sem):
    i = pl.program_id(0)
    per_row = wg16.shape[0] // LANES
    tm = o_ref.shape[0] // per_row
    n_used = nu_ref[0]
    slot = i % 2

    @pl.when(i == 0)
    def _():
        _gather_rows(tok_ref, 0, h_hbm, xbuf.at[0], sem.at[0], tm, per_row, EXPERT_GATHER_PRIORITIES)

    @pl.when(i + 1 < n_used)
    def _():
        _gather_rows(tok_ref, (i + 1) * tm, h_hbm, xbuf.at[1 - slot], sem.at[1 - slot], tm, per_row,
                     EXPERT_GATHER_PRIORITIES)

    @pl.when(i < n_used)
    def _():
        changed = jnp.logical_or(i == 0, te_ref[i] != te_ref[jnp.maximum(i - 1, 0)])

        @pl.when(changed)
        def _():
            wg16[...] = wg_ref[0].astype(MXU_DTYPE)
            wu16[...] = wu_ref[0].astype(MXU_DTYPE)
            wd16[...] = wd_ref[0].astype(MXU_DTYPE)

        _wait_rows(h_hbm, xbuf.at[slot], sem.at[slot], tm, per_row)
        x = _load_row_tiles(xbuf.at[slot], per_row).astype(MXU_DTYPE)
        g = jnp.dot(x, wg16[...], preferred_element_type=F32)
        u = jnp.dot(x, wu16[...], preferred_element_type=F32)
        _store_row_tiles(o_ref, jnp.dot((_silu(g) * u).astype(MXU_DTYPE), wd16[...], preferred_element_type=F32))

    @pl.when(i >= n_used)
    def _():
        o_ref[...] = jnp.zeros(o_ref.shape, F32)


def _experts(row_tok, tile_e, n_used, h_all, w_gate, w_up, w_down, layer):
    n_rows = row_tok.shape[0]
    tm = TM_EXPERT
    d, de = w_gate.shape[-2:]
    per_row = d // LANES
    wspec_in = pl.BlockSpec((None, 1, d, de), lambda i, tok, te, nu: (layer, te[i], 0, 0))
    wspec_out = pl.BlockSpec((None, 1, de, d), lambda i, tok, te, nu: (layer, te[i], 0, 0))
    return pl.pallas_call(
        _expert_body,
        grid_spec=pltpu.PrefetchScalarGridSpec(
            num_scalar_prefetch=3,
            grid=(n_rows // tm,),
            in_specs=[pl.BlockSpec(memory_space=pl.ANY), wspec_in, wspec_in, wspec_out],
            out_specs=pl.BlockSpec((tm * per_row, LANES), lambda i, tok, te, nu: (i, 0)),
            scratch_shapes=[pltpu.VMEM((2, tm * per_row, LANES), F32),
                            pltpu.VMEM((d, de), MXU_DTYPE), pltpu.VMEM((d, de), MXU_DTYPE),
                            pltpu.VMEM((de, d), MXU_DTYPE), pltpu.SemaphoreType.DMA((2,))]),
        out_shape=jax.ShapeDtypeStruct((n_rows * per_row, LANES), F32),
        compiler_params=_params("arbitrary", row_gather=True),
        name="experts",
    )(row_tok, tile_e, n_used, h_all, w_gate, w_up, w_down)


def _combine_body(dest_ref, ys_hbm, x_ref, w_ref, g_ref, lng_ref, lnb_ref, o_ref, buf, sem, *, alpha, tok0, k_stride):
    t = pl.program_id(0)
    tm, d = x_ref.shape
    per_row = d // LANES
    slot = t % 2

    def gather(tile, slot_):
        for k in range(TOP_K):
            _gather_rows(dest_ref, k * k_stride + tok0 + tile * tm, ys_hbm, buf.at[slot_, k], sem.at[slot_, k], tm,
                         per_row, COMBINE_GATHER_PRIORITIES)

    @pl.when(t == 0)
    def _():
        gather(0, 0)

    @pl.when(t + 1 < pl.num_programs(0))
    def _():
        gather(t + 1, 1 - slot)

    for k in range(TOP_K):
        _wait_rows(ys_hbm, buf.at[slot, k], sem.at[slot, k], tm, per_row)
    wts = _mxu_round(w_ref[...])
    f = _mxu_round(_load_row_tiles(buf.at[slot, 0], per_row)) * wts[:, 0:1]
    for k in range(1, TOP_K):
        f = f + _mxu_round(_load_row_tiles(buf.at[slot, k], per_row)) * wts[:, k:k + 1]
    o_ref[...] = _layer_norm(alpha * x_ref[...] + g_ref[0] * f, lng_ref[...], lnb_ref[...])


def _combine(dest, ys, x, wts, gate2, ln_g, ln_b, *, tm, rows_per_mod, alpha, tok0):
    n, d = x.shape
    r = gate2.shape[1]
    vec_spec = pl.BlockSpec((1, d), lambda t, ds_: (0, 0))
    return pl.pallas_call(
        functools.partial(_combine_body, alpha=alpha, tok0=tok0, k_stride=dest.shape[0] // TOP_K),
        grid_spec=pltpu.PrefetchScalarGridSpec(
            num_scalar_prefetch=1,
            grid=(n // tm,),
            in_specs=[pl.BlockSpec(memory_space=pl.ANY),
                      pl.BlockSpec((tm, d), lambda t, ds_: (t, 0)),
                      pl.BlockSpec((tm, TOP_K), lambda t, ds_: (t, 0)),
                      pl.BlockSpec((1, r, d), lambda t, ds_: (t // rows_per_mod, 0, 0)),
                      vec_spec, vec_spec],
            out_specs=pl.BlockSpec((tm, d), lambda t, ds_: (t, 0)),
            scratch_shapes=[pltpu.VMEM((2, TOP_K, tm * (d // LANES), LANES), F32),
                            pltpu.SemaphoreType.DMA((2, TOP_K))]),
        out_shape=jax.ShapeDtypeStruct((n, d), F32),
        compiler_params=_params("arbitrary", row_gather=True),
        name="combine",
    )(dest, ys, x, wts, gate2, ln_g, ln_b)


def _top2_of_group(x, sub):
    m1 = jnp.max(x, axis=0, keepdims=True)
    i1 = jnp.min(jnp.where(x == m1, sub, EXPERTS_PER_GROUP), axis=0, keepdims=True)
    rest = jnp.where(sub == i1, -jnp.inf, x)
    m2 = jnp.max(rest, axis=0, keepdims=True)
    i2 = jnp.min(jnp.where(rest == m2, sub, EXPERTS_PER_GROUP), axis=0, keepdims=True)
    return m1 + m2, i1, i2


def _route_body(lg_ref, b_ref, e_ref, w_ref, r_ref, cnt_ref, tri_ref, base_ref, *, n_tok):
    t = pl.program_id(0)
    tt = lg_ref.shape[0]

    @pl.when(t == 0)
    def _():
        earlier = lax.broadcasted_iota(I32, (tt, tt), 0) < lax.broadcasted_iota(I32, (tt, tt), 1)
        tri_ref[...] = jnp.where(earlier, 1.0, 0.0).astype(MXU_DTYPE)
        base_ref[...] = jnp.zeros(base_ref.shape, F32)

    scores = jax.nn.sigmoid(lg_ref[...].T[:N_EXPERTS])
    biased = scores + b_ref[...]
    sub = lax.broadcasted_iota(I32, (EXPERTS_PER_GROUP, tt), 0)
    best, i1, i2 = _top2_of_group(biased[:EXPERTS_PER_GROUP], sub)
    g_sel = jnp.zeros((1, tt), I32)
    for g in range(1, N_GROUPS):
        score_g, i1_g, i2_g = _top2_of_group(biased[g * EXPERTS_PER_GROUP:(g + 1) * EXPERTS_PER_GROUP], sub)
        better = score_g > best
        best = jnp.where(better, score_g, best)
        g_sel = jnp.where(better, g, g_sel)
        i1 = jnp.where(better, i1_g, i1)
        i2 = jnp.where(better, i2_g, i2)
    eid = lax.broadcasted_iota(I32, (N_EXPERTS, tt), 0)
    valid = t * tt + lax.broadcasted_iota(I32, (1, tt), 1) < n_tok
    picks = [g_sel * EXPERTS_PER_GROUP + i1, g_sel * EXPERTS_PER_GROUP + i2]
    hit = [eid == e for e in picks]
    raw = [jnp.sum(jnp.where(h, scores, 0.0), axis=0, keepdims=True) for h in hit]
    denom = raw[0] + raw[1]
    base = base_ref[...]
    for k in range(TOP_K):
        onehot = jnp.where(hit[k] & valid, 1.0, 0.0)
        before = jnp.dot(onehot.astype(MXU_DTYPE), tri_ref[...], preferred_element_type=F32)
        rank = jnp.sum(onehot * (base + before), axis=0, keepdims=True)
        base = base + jnp.sum(onehot, axis=1, keepdims=True)
        e_ref[k:k + 1, :] = picks[k]
        w_ref[k:k + 1, :] = raw[k] / denom
        r_ref[k:k + 1, :] = rank.astype(I32)
    base_ref[...] = base
    cnt_ref[...] = jnp.broadcast_to(base, cnt_ref.shape)


def _row_tokens_body(dest_ref, o_ref, *, n_tok, k_stride):
    def clear(r, c):
        o_ref[r] = 0
        return c

    def put(tok, c):
        for k in range(TOP_K):
            o_ref[dest_ref[k * k_stride + tok]] = tok
        return c

    lax.fori_loop(0, o_ref.shape[0], clear, 0, unroll=GATHER_UNROLL)
    lax.fori_loop(0, n_tok, put, 0, unroll=GATHER_UNROLL)


def _row_tokens(dest, n_tok, n_rows):
    return pl.pallas_call(
        functools.partial(_row_tokens_body, n_tok=n_tok, k_stride=dest.shape[0] // TOP_K),
        in_specs=[pl.BlockSpec(memory_space=pltpu.SMEM)],
        out_specs=pl.BlockSpec(memory_space=pltpu.SMEM),
        out_shape=jax.ShapeDtypeStruct((n_rows,), I32),
        name="row_tokens",
    )(dest)


def _route(logits, b_router, n_tok, n_rows):
    tm = TM_EXPERT
    tt = ROUTE_TILE
    n_steps = -(-n_tok // tt)
    n_pad = n_steps * tt
    pick_spec = pl.BlockSpec((TOP_K, tt), lambda t: (0, t))
    e_idx, wts, rank, cnt = pl.pallas_call(
        functools.partial(_route_body, n_tok=n_tok),
        grid=(n_steps,),
        in_specs=[pl.BlockSpec((tt, LANES), lambda t: (t, 0)), pl.BlockSpec((N_EXPERTS, 1), lambda t: (0, 0))],
        out_specs=[pick_spec, pick_spec, pick_spec, pl.BlockSpec((N_EXPERTS, LANES), lambda t: (0, 0))],
        out_shape=[jax.ShapeDtypeStruct((TOP_K, n_pad), I32), jax.ShapeDtypeStruct((TOP_K, n_pad), F32),
                   jax.ShapeDtypeStruct((TOP_K, n_pad), I32), jax.ShapeDtypeStruct((N_EXPERTS, LANES), F32)],
        scratch_shapes=[pltpu.VMEM((tt, tt), MXU_DTYPE), pltpu.VMEM((N_EXPERTS, 1), F32)],
        compiler_params=_params("arbitrary"),
        name="route",
    )(logits, b_router.astype(F32).reshape(N_EXPERTS, 1))
    counts = cnt[:, 0].astype(I32)
    padded = (counts + tm - 1) // tm * tm
    pend = jnp.cumsum(padded)
    pstart = pend - padded
    expert_ids = jnp.arange(N_EXPERTS, dtype=I32)[:, None, None]
    dest = rank + jnp.sum(jnp.where(e_idx[None] == expert_ids, pstart[:, None, None], 0), axis=0)
    row_tok = _row_tokens(dest.reshape(-1), n_tok, n_rows)
    n_tiles = n_rows // tm
    n_used = (pend[-1] // tm).astype(I32)
    tile_start = jnp.minimum(jnp.arange(n_tiles, dtype=I32), n_used - 1) * tm
    tile_e = jnp.minimum(jnp.sum(pend[None, :] <= tile_start[:, None], axis=1), N_EXPERTS - 1).astype(I32)
    return row_tok, tile_e, n_used.reshape(1), dest.reshape(-1), wts[:, :n_tok].T


def _rotary_tables(pos):
    inv_freq = ROPE_THETA ** (-jnp.arange(ROT_HALF, dtype=F32) / ROT_HALF)
    ang = pos.astype(F32)[:, None] * inv_freq[None, :]
    ones = jnp.ones((pos.shape[0], LANES - ROT_DIM), F32)
    cos_t = jnp.concatenate([jnp.cos(ang), jnp.cos(ang), ones], axis=1)
    sin_t = jnp.concatenate([jnp.sin(ang), jnp.sin(ang), 0.0 * ones], axis=1)
    return cos_t, sin_t


def kernel(x_prompt, x_sample, cache_k, cache_v, state_conv, page_table, c_prompt, c_sample,
           w_ada, b_ada, ln_g, ln_b, w_qkv, w_o, conv_w_in, conv_w_dw, conv_ln_g, conv_ln_b,
           conv_w_out, w_router, b_router, w_gate, w_up, w_down):
    batch, seq, d = x_prompt.shape
    db, dec_seq, _ = x_sample.shape
    assert dec_seq == 1 and seq % TM == 0 and TM % MOBA_BLOCK == 0 and d == N_HEADS * HEAD_DIM
    depth = w_ada.shape[0]
    n_pages = page_table.shape[1]
    past_len = n_pages * PAGE_SIZE
    assert n_pages % KMEAN_PAGES == 0 and n_pages % PAGES_PER_BLOCK == 0
    n_full = n_pages // PAGES_PER_BLOCK
    topk_s = min(MOBA_TOPK, n_full)
    alpha = (2 * depth) ** 0.25
    n_p = batch * seq
    n_all = n_p + db
    assert n_p % db == 0 and n_p % ROUTE_TILE == 0 and ROUTE_TILE % TM == 0
    n_buf = -(-n_all // ROUTE_TILE) * ROUTE_TILE
    tps = seq // TM

    c_rows = -(-(batch + db) // 8) * 8
    c_all = jnp.concatenate([c_prompt, c_sample, jnp.zeros((c_rows - batch - db, d), F32)], axis=0)
    mod = _ada(c_all, w_ada, b_ada)

    wr_pad = jnp.pad(w_router, ((0, 0), (0, LANES - N_EXPERTS))).astype(MXU_DTYPE)
    n_assign = n_all * TOP_K
    n_rows = -(-(n_assign + N_EXPERTS * (TM_EXPERT - 1)) // TM_EXPERT) * TM_EXPERT

    cos_p, sin_p = _rotary_tables(jnp.arange(seq))
    cos_s, sin_s = _rotary_tables(past_len + jnp.zeros((db,), I32))

    xp = x_prompt.reshape(n_p, d)
    xs = x_sample.reshape(db, d)
    kp_pages, vp_pages, ks_rows, vs_rows, conv_p, conv_s = [], [], [], [], [], []
    for i in range(depth):
        mp = [m[:, None, :] for m in jnp.split(mod[i, :batch], 6, axis=-1)]
        ms = [m[None] for m in jnp.split(mod[i, batch:batch + db], 6, axis=-1)]
        if i % 2 == 0:
            ia = i // 2
            q, kp, vp, kx, v16, km = _qkv_prompt(xp, mp[0], mp[1], w_qkv[ia], cos_p, sin_p, batch, seq)
            km = km.transpose(0, 2, 1, 3, 4).reshape(batch, N_KV_HEADS, seq // MOBA_BLOCK, HEAD_DIM)
            attn_p = _moba_prompt(q, kx, v16, km, batch, seq)
            kp_pages.append(kp)
            vp_pages.append(vp)

            qkv_s = _qkv_sample(xs, ms[0][0], ms[1][0], w_qkv[ia], cos_s, sin_s)
            nq = N_HEADS * HEAD_DIM
            nk = N_KV_HEADS * HEAD_DIM
            q_s = qkv_s[:, :nq].reshape(db, N_HEADS, HEAD_DIM)
            k_s = qkv_s[:, nq:nq + nk].reshape(db, N_KV_HEADS, HEAD_DIM)
            v_s = qkv_s[:, nq + nk:].reshape(db, N_KV_HEADS, HEAD_DIM)
            ks_rows.append(k_s[:, :, None, :])
            vs_rows.append(v_s[:, :, None, :])
            if topk_s > 0:
                kmean_s = _kmean_sample(page_table, cache_k, ia)
                kvh = jnp.arange(N_HEADS) // GQA_GROUP
                gate_s = jnp.einsum("bhd,bnhd->bhn", q_s, kmean_s[:, :, kvh])
                _, sel = lax.top_k(gate_s, topk_s)
                sel_pages = (sel[..., None] * PAGES_PER_BLOCK + jnp.arange(PAGES_PER_BLOCK)).reshape(db, N_HEADS, -1)
                phys = jnp.take_along_axis(page_table[:, None, :], sel_pages, axis=2).astype(I32)
                attn_s = _attn_sample(phys.reshape(-1), q_s.reshape(db * N_HEADS, 1, HEAD_DIM),
                                      k_s.reshape(db * N_KV_HEADS, 1, HEAD_DIM),
                                      v_s.reshape(db * N_KV_HEADS, 1, HEAD_DIM), cache_k, cache_v, ia)
                attn_s = attn_s.reshape(db, nq)
            else:
                attn_s = jnp.repeat(v_s, GQA_GROUP, axis=1).reshape(db, nq)
            mixer_p, mixer_s, w_mix = attn_p, attn_s, w_o[ia]
            mode_p, mode_s = "attn", "attn"
        else:
            ic = i // 2
            u_p = _glu(xp, mp[0], mp[1], conv_w_in[ic], TM, tps)
            u_s = _glu(xs, ms[0], ms[1], conv_w_in[ic], db, 1)
            cg, cb = conv_ln_g[ic][None], conv_ln_b[ic][None]
            mixer_p = (u_p, conv_w_dw[ic], cg, cb)
            mixer_s = (u_s, state_conv[ic].transpose(1, 0, 2), conv_w_dw[ic], cg, cb)
            w_mix = conv_w_out[ic]
            mode_p, mode_s = "conv", "conv_step"
            conv_p.append(u_p.reshape(batch, seq, d)[:, seq - (CONV_WIDTH - 1):])
            conv_s.append(jnp.concatenate([state_conv[ic][:, 1:], u_s[:, None, :]], axis=1))

        x1p, h2_all, lg_all = _post(mode_p, mixer_p, w_mix, xp, mp[2], ln_g[i, 0][None], ln_b[i, 0][None],
                                    mp[3], mp[4], wr_pad, tm=TM, rows_per_mod=tps, alpha=alpha,
                                    seq=seq, out_rows=n_buf)
        x1s, h2_all, lg_all = _post(mode_s, mixer_s, w_mix, xs, ms[2], ln_g[i, 0][None], ln_b[i, 0][None],
                                    ms[3], ms[4], wr_pad, tm=db, rows_per_mod=1, alpha=alpha,
                                    into=(h2_all, lg_all, n_p // db))
        row_tok, tile_e, n_used, dest, wts = _route(lg_all, b_router, n_all, n_rows)
        ys = _experts(row_tok, tile_e, n_used, h2_all, w_gate, w_up, w_down, i)
        xp = _combine(dest, ys, x1p, wts[:n_p], mp[5], ln_g[i, 1][None], ln_b[i, 1][None],
                      tm=TM_COMBINE, rows_per_mod=seq // TM_COMBINE, alpha=alpha, tok0=0)
        xs = _combine(dest, ys, x1s, wts[n_p:], ms[5], ln_g[i, 1][None], ln_b[i, 1][None],
                      tm=db, rows_per_mod=1, alpha=alpha, tok0=n_p)

    k_prompt = jnp.concatenate(kp_pages, axis=2)
    v_prompt = jnp.concatenate(vp_pages, axis=2)
    return (xp.reshape(batch, seq, d), xs.reshape(db, 1, d), k_prompt, v_prompt, jnp.stack(conv_p, axis=0),
            jnp.stack(ks_rows, axis=1), jnp.stack(vs_rows, axis=1), jnp.stack(conv_s, axis=0))
```

```python
import functools
import math

import jax
import jax.numpy as jnp
from jax import lax
from jax.experimental import pallas as pl
from jax.experimental.pallas import tpu as pltpu

F32 = jnp.float32
I32 = jnp.int32
MXU_DTYPE = jnp.bfloat16

N_HEADS = 8
N_KV_HEADS = 2
GQA_GROUP = N_HEADS // N_KV_HEADS
HEAD_DIM = 128
ROT_DIM = HEAD_DIM // 4
ROT_HALF = ROT_DIM // 2
ROPE_THETA = 500000.0
ATTN_SCALE = HEAD_DIM ** -0.5
MOBA_BLOCK = 256
MOBA_TOPK = 3
MOBA_CHUNK = 128
EXP2_SCALE = ATTN_SCALE * math.log2(math.e)
PAGE_SIZE = 128
PAGES_PER_BLOCK = MOBA_BLOCK // PAGE_SIZE
CONV_WIDTH = 31
CONV_HALO = 32
N_EXPERTS = 32
N_GROUPS = 4
EXPERTS_PER_GROUP = N_EXPERTS // N_GROUPS
TOP_K = 2
LN_EPS = 1e-5
MASK_VALUE = -1e30
LANES = 128
VMEM_LIMIT = 56 * 1024 * 1024

TM = 512
TM_EXPERT = 256
TM_COMBINE = 256
KMEAN_PAGES = 32
ROUTE_TILE = 512
GATHER_UNROLL = 8


def _params(*sem, row_gather=False):
    return pltpu.CompilerParams(dimension_semantics=sem, vmem_limit_bytes=VMEM_LIMIT,
                                disable_bounds_checks=row_gather)


def _dot(a, b):
    return jnp.dot(a.astype(MXU_DTYPE), b.astype(MXU_DTYPE), preferred_element_type=F32)


def _dot_nt(a, b):
    return lax.dot_general(a.astype(MXU_DTYPE), b.astype(MXU_DTYPE), (((1,), (1,)), ((), ())),
                           preferred_element_type=F32)


def _mxu_round(x):
    return x.astype(MXU_DTYPE).astype(F32)


def _store_row_tiles(ref, val):
    rows, d = val.shape
    per_row = d // LANES
    for s in range(per_row):
        ref[pl.ds(s, rows, stride=per_row), :] = val[:, s * LANES:(s + 1) * LANES]


def _load_row_tiles(ref, per_row):
    rows = ref.shape[0] // per_row
    return jnp.concatenate([ref[pl.ds(s, rows, stride=per_row), :] for s in range(per_row)], axis=1)


def _layer_norm(z, g, b):
    mu = jnp.mean(z, axis=-1, keepdims=True)
    zc = z - mu
    var = jnp.mean(zc * zc, axis=-1, keepdims=True)
    return zc * lax.rsqrt(var + LN_EPS) * g + b


def _silu(x):
    return x * jax.nn.sigmoid(x)


def _rotary(xc, cos, sin, lane):
    x_up = pltpu.roll(xc, LANES - ROT_HALF, axis=1)
    x_dn = pltpu.roll(xc, ROT_HALF, axis=1)
    first = xc * cos - x_up * sin
    second = xc * cos + x_dn * sin
    return jnp.where(lane < ROT_HALF, first, jnp.where(lane < ROT_DIM, second, xc))


def _ada_body(c_ref, w_ref, b_ref, o_ref):
    o_ref[0] = _dot(_silu(c_ref[...]), w_ref[0]) + b_ref[0]


def _ada(c_all, w_ada, b_ada):
    depth, d, n6 = w_ada.shape
    rows = c_all.shape[0]
    tn = 1536
    return pl.pallas_call(
        _ada_body,
        grid=(depth, n6 // tn),
        in_specs=[pl.BlockSpec((rows, d), lambda i, j: (0, 0)),
                  pl.BlockSpec((1, d, tn), lambda i, j: (i, 0, j)),
                  pl.BlockSpec((1, 1, tn), lambda i, j: (i, 0, j))],
        out_specs=pl.BlockSpec((1, rows, tn), lambda i, j: (i, 0, j)),
        out_shape=jax.ShapeDtypeStruct((depth, rows, n6), F32),
        compiler_params=_params("arbitrary", "arbitrary"),
        name="ada",
    )(c_all, w_ada, b_ada.reshape(depth, 1, n6))


def _qkv_prompt_body(x_ref, sh_ref, sc_ref, w_ref, cos_ref, sin_ref,
                     q_ref, kp_ref, vp_ref, kx_ref, v16_ref, km_ref, w16, *, tiles_per_seq):
    t = pl.program_id(0)
    tm = x_ref.shape[0]

    @pl.when(t == 0)
    def _():
        w16[...] = w_ref[...].astype(MXU_DTYPE)

    h = x_ref[...] * (1 + sc_ref[0]) + sh_ref[0]
    qkv = jnp.dot(h.astype(MXU_DTYPE), w16[...], preferred_element_type=F32)
    cos = cos_ref[...]
    sin = sin_ref[...]
    lane = lax.broadcasted_iota(I32, (tm, LANES), 1)
    nq = N_HEADS * HEAD_DIM
    nk = N_KV_HEADS * HEAD_DIM
    for hh in range(N_HEADS):
        sl = slice(hh * HEAD_DIM, (hh + 1) * HEAD_DIM)
        q_ref[:, sl] = _rotary(qkv[:, sl], cos, sin, lane).astype(q_ref.dtype)
    row = lax.broadcasted_iota(I32, (tm, LANES), 0)
    blk = ((t % tiles_per_seq) * tm + row) // MOBA_BLOCK
    onehot = jnp.where(lane == blk, 1.0, 0.0).astype(kx_ref.dtype)
    for n in range(N_KV_HEADS):
        kc = _rotary(qkv[:, nq + n * HEAD_DIM:nq + (n + 1) * HEAD_DIM], cos, sin, lane)
        vc = qkv[:, nq + nk + n * HEAD_DIM:nq + nk + (n + 1) * HEAD_DIM]
        kp_ref[0, :, 0, n] = kc.reshape(tm // PAGE_SIZE, PAGE_SIZE, HEAD_DIM)
        vp_ref[0, :, 0, n] = vc.reshape(tm // PAGE_SIZE, PAGE_SIZE, HEAD_DIM)
        kx_ref[0, n, :, 0:HEAD_DIM] = kc.astype(kx_ref.dtype)
        kx_ref[0, n, :, HEAD_DIM:2 * HEAD_DIM] = onehot
        v16_ref[0, n] = vc.astype(v16_ref.dtype)
        km_ref[0, 0, n] = jnp.sum(kc.reshape(tm // MOBA_BLOCK, MOBA_BLOCK, HEAD_DIM), axis=1) * (1.0 / MOBA_BLOCK)


def _qkv_prompt(x, shift, scale, w_qkv, cos_t, sin_t, batch, seq):
    n, d = x.shape
    tm = TM
    tps = seq // tm
    width = w_qkv.shape[1]
    npg = seq // PAGE_SIZE
    mod_spec = pl.BlockSpec((1, 1, d), lambda t: (t // tps, 0, 0))
    rot_spec = pl.BlockSpec((tm, LANES), lambda t: (t % tps, 0))
    page_spec = pl.BlockSpec((1, tm // PAGE_SIZE, 1, N_KV_HEADS, PAGE_SIZE, HEAD_DIM),
                             lambda t: (t // tps, t % tps, 0, 0, 0, 0))
    page_shape = jax.ShapeDtypeStruct((batch, npg, 1, N_KV_HEADS, PAGE_SIZE, HEAD_DIM), F32)
    return pl.pallas_call(
        functools.partial(_qkv_prompt_body, tiles_per_seq=tps),
        grid=(n // tm,),
        in_specs=[pl.BlockSpec((tm, d), lambda t: (t, 0)), mod_spec, mod_spec,
                  pl.BlockSpec((d, width), lambda t: (0, 0)), rot_spec, rot_spec],
        out_specs=[pl.BlockSpec((tm, N_HEADS * HEAD_DIM), lambda t: (t, 0)),
                   page_spec, page_spec,
                   pl.BlockSpec((1, N_KV_HEADS, tm, 2 * HEAD_DIM), lambda t: (t // tps, 0, t % tps, 0)),
                   pl.BlockSpec((1, N_KV_HEADS, tm, HEAD_DIM), lambda t: (t // tps, 0, t % tps, 0)),
                   pl.BlockSpec((1, 1, N_KV_HEADS, tm // MOBA_BLOCK, HEAD_DIM), lambda t: (t // tps, t % tps, 0, 0, 0))],
        out_shape=[jax.ShapeDtypeStruct((n, N_HEADS * HEAD_DIM), MXU_DTYPE),
                   page_shape, page_shape,
                   jax.ShapeDtypeStruct((batch, N_KV_HEADS, seq, 2 * HEAD_DIM), MXU_DTYPE),
                   jax.ShapeDtypeStruct((batch, N_KV_HEADS, seq, HEAD_DIM), MXU_DTYPE),
                   jax.ShapeDtypeStruct((batch, tps, N_KV_HEADS, tm // MOBA_BLOCK, HEAD_DIM), F32)],
        scratch_shapes=[pltpu.VMEM((d, width), MXU_DTYPE)],
        compiler_params=_params("arbitrary"),
        name="qkv_prompt",
    )(x, shift, scale, w_qkv, cos_t, sin_t)


def _qkv_sample_body(x_ref, sh_ref, sc_ref, w_ref, cos_ref, sin_ref, o_ref):
    rows = x_ref.shape[0]
    h = x_ref[...] * (1 + sc_ref[...]) + sh_ref[...]
    qkv = _dot(h, w_ref[...])
    lane = lax.broadcasted_iota(I32, (rows, LANES), 1)
    n_rot = N_HEADS + N_KV_HEADS
    for c in range(n_rot):
        sl = slice(c * HEAD_DIM, (c + 1) * HEAD_DIM)
        o_ref[:, sl] = _rotary(qkv[:, sl], cos_ref[...], sin_ref[...], lane)
    o_ref[:, n_rot * HEAD_DIM:] = qkv[:, n_rot * HEAD_DIM:]


def _qkv_sample(x, shift, scale, w_qkv, cos_t, sin_t):
    rows, d = x.shape
    width = w_qkv.shape[1]
    full = lambda shape: pl.BlockSpec(shape, lambda: tuple(0 for _ in shape))
    return pl.pallas_call(
        _qkv_sample_body,
        in_specs=[full((rows, d)), full((rows, d)), full((rows, d)), full((d, width)),
                  full((rows, LANES)), full((rows, LANES))],
        out_specs=full((rows, width)),
        out_shape=jax.ShapeDtypeStruct((rows, width), F32),
        compiler_params=pltpu.CompilerParams(vmem_limit_bytes=VMEM_LIMIT),
        name="qkv_sample",
    )(x, shift, scale, w_qkv, cos_t, sin_t)


def _moba_prompt_body(q_ref, kx_ref, v_ref, km_ref, o_ref, qx_ref, *state):
    i = pl.program_id(2)
    rows = GQA_GROUP * MOBA_BLOCK
    n_chunks = rows // MOBA_CHUNK
    s_refs, p_refs, a_refs, acc_ref = state[0:2], state[2:4], state[4:6], state[6]
    m_refs, l_refs = state[7:7 + n_chunks], state[7 + n_chunks:]
    n_blk = km_ref.shape[2]
    for h in range(GQA_GROUP):
        qx_ref[h * MOBA_BLOCK:(h + 1) * MOBA_BLOCK, 0:HEAD_DIM] = q_ref[:, h * HEAD_DIM:(h + 1) * HEAD_DIM]

    gate = _dot_nt(km_ref[0, 0], qx_ref[:, 0:HEAD_DIM])
    blk = lax.broadcasted_iota(I32, (n_blk, rows), 0)
    valid = blk < i
    cand = jnp.where(valid, gate, -jnp.inf)
    sel = blk == i
    for _ in range(MOBA_TOPK):
        top = jnp.max(cand, axis=0, keepdims=True)
        idx = jnp.min(jnp.where(cand == top, blk, n_blk), axis=0, keepdims=True)
        pick = blk == idx
        sel = sel | (pick & valid)
        cand = jnp.where(pick, -jnp.inf, cand)
    bias = jnp.where(sel, 0.0, MASK_VALUE)
    if n_blk < LANES:
        bias = jnp.concatenate([bias, jnp.zeros((LANES - n_blk, rows), F32)], axis=0)
    qx_ref[:, HEAD_DIM:2 * HEAD_DIM] = bias.T.astype(qx_ref.dtype)

    half = MOBA_BLOCK // 2

    def issue_scores(j, slot):
        start = pl.multiple_of(j * MOBA_BLOCK, MOBA_BLOCK)
        s_refs[slot][...] = _dot_nt(qx_ref[...], kx_ref[0, 0, pl.ds(start, MOBA_BLOCK), :])

    def softmax(slot, own):
        for c in range(n_chunks):
            rs = slice(c * MOBA_CHUNK, (c + 1) * MOBA_CHUNK)
            s = s_refs[slot][rs, :]
            if own:
                qpos = (c * MOBA_CHUNK) % MOBA_BLOCK + lax.broadcasted_iota(I32, (MOBA_CHUNK, MOBA_BLOCK), 0)
                kpos = lax.broadcasted_iota(I32, (MOBA_CHUNK, MOBA_BLOCK), 1)
                s = jnp.where(kpos <= qpos, s, MASK_VALUE)
            sa, sb = s[:, :half], s[:, half:]
            top = jnp.broadcast_to(jnp.max(jnp.maximum(sa, sb), axis=1, keepdims=True), (MOBA_CHUNK, half))
            if own:
                m_new = top
            else:
                m_old = m_refs[c][...]
                m_new = jnp.maximum(m_old, top)
                alpha = jnp.exp2((m_old - m_new) * EXP2_SCALE)
                a_refs[slot][rs, :] = alpha
            pa = jnp.exp2((sa - m_new) * EXP2_SCALE)
            pb = jnp.exp2((sb - m_new) * EXP2_SCALE)
            p_refs[slot][rs, :half] = pa.astype(MXU_DTYPE)
            p_refs[slot][rs, half:] = pb.astype(MXU_DTYPE)
            if own:
                l_refs[c][...] = pa + pb
            else:
                l_refs[c][...] = alpha * l_refs[c][...] + (pa + pb)
            m_refs[c][...] = m_new

    def accumulate(j, slot, own):
        start = pl.multiple_of(j * MOBA_BLOCK, MOBA_BLOCK)
        pv = jnp.dot(p_refs[slot][...], v_ref[0, 0, pl.ds(start, MOBA_BLOCK), :], preferred_element_type=F32)
        if own:
            acc_ref[...] = pv
        else:
            acc_ref[...] = a_refs[slot][...] * acc_ref[...] + pv

    issue_scores(i, 1)
    issue_scores(0, 0)
    softmax(1, True)
    accumulate(i, 1, True)

    def pair(t, carry):
        j = 2 * t
        issue_scores(j + 1, 1)
        softmax(0, False)
        accumulate(j, 0, False)
        issue_scores(jnp.minimum(j + 2, i - 1), 0)
        softmax(1, False)
        accumulate(j + 1, 1, False)
        return carry

    lax.fori_loop(0, i // 2, pair, 0)

    @pl.when(i % 2 == 1)
    def _():
        softmax(0, False)
        accumulate(i - 1, 0, False)

    per_head = MOBA_BLOCK // MOBA_CHUNK
    for c in range(n_chunks):
        rs = slice(c * MOBA_CHUNK, (c + 1) * MOBA_CHUNK)
        out = acc_ref[rs, :] / jnp.sum(l_refs[c][...], axis=1, keepdims=True)
        h, part = divmod(c, per_head)
        o_ref[part * MOBA_CHUNK:(part + 1) * MOBA_CHUNK, h * HEAD_DIM:(h + 1) * HEAD_DIM] = out.astype(o_ref.dtype)


def _moba_prompt(q, kx, v16, k_mean, batch, seq):
    n = q.shape[0]
    nq = seq // MOBA_BLOCK
    rows = GQA_GROUP * MOBA_BLOCK
    gw = GQA_GROUP * HEAD_DIM
    return pl.pallas_call(
        _moba_prompt_body,
        grid=(batch, N_KV_HEADS, nq),
        in_specs=[pl.BlockSpec((MOBA_BLOCK, gw), lambda b, g, i: (b * nq + i, g)),
                  pl.BlockSpec((1, 1, seq, 2 * HEAD_DIM), lambda b, g, i: (b, g, 0, 0)),
                  pl.BlockSpec((1, 1, seq, HEAD_DIM), lambda b, g, i: (b, g, 0, 0)),
                  pl.BlockSpec((1, 1, nq, HEAD_DIM), lambda b, g, i: (b, g, 0, 0))],
        out_specs=pl.BlockSpec((MOBA_BLOCK, gw), lambda b, g, i: (b * nq + i, g)),
        out_shape=jax.ShapeDtypeStruct((n, N_HEADS * HEAD_DIM), MXU_DTYPE),
        scratch_shapes=([pltpu.VMEM((rows, 2 * HEAD_DIM), MXU_DTYPE)]
                        + [pltpu.VMEM((rows, MOBA_BLOCK), F32)] * 2
                        + [pltpu.VMEM((rows, MOBA_BLOCK), MXU_DTYPE)] * 2
                        + [pltpu.VMEM((rows, MOBA_BLOCK // 2), F32)] * 2
                        + [pltpu.VMEM((rows, HEAD_DIM), F32)]
                        + [pltpu.VMEM((MOBA_CHUNK, MOBA_BLOCK // 2), F32)] * (2 * (rows // MOBA_CHUNK))),
        compiler_params=_params("arbitrary", "arbitrary", "arbitrary"),
        name="moba_prompt",
    )(q, kx, v16, k_mean)


def _kmean_copy(pt_ref, ck_hbm, buf, sem, layer, step, slot, p):
    phys = pt_ref[step * KMEAN_PAGES + p]
    return pltpu.make_async_copy(ck_hbm.at[phys, layer], buf.at[slot, p], sem.at[slot])


def _kmean_sample_body(pt_ref, ck_hbm, o_ref, buf, sem, *, layer):
    nc = pl.num_programs(1)
    step = pl.program_id(0) * nc + pl.program_id(1)
    total = pl.num_programs(0) * nc
    slot = step % 2

    def start(step_, slot_):
        for p in range(KMEAN_PAGES):
            _kmean_copy(pt_ref, ck_hbm, buf, sem, layer, step_, slot_, p).start()

    @pl.when(step == 0)
    def _():
        start(step, slot)

    @pl.when(step + 1 < total)
    def _():
        start(step + 1, 1 - slot)

    for p in range(KMEAN_PAGES):
        _kmean_copy(pt_ref, ck_hbm, buf, sem, layer, step, slot, p).wait()
    for jb in range(KMEAN_PAGES // PAGES_PER_BLOCK):
        for n in range(N_KV_HEADS):
            acc = jnp.zeros((1, HEAD_DIM), F32)
            for pp in range(PAGES_PER_BLOCK):
                acc = acc + jnp.sum(buf[slot, jb * PAGES_PER_BLOCK + pp, n], axis=0, keepdims=True)
            o_ref[0, jb, pl.ds(n, 1), :] = acc * (1.0 / MOBA_BLOCK)


def _kmean_sample(page_table, cache_k, layer):
    db, n_pages = page_table.shape
    n_full = n_pages // PAGES_PER_BLOCK
    page_shape = cache_k.shape[2:]
    bps = KMEAN_PAGES // PAGES_PER_BLOCK
    return pl.pallas_call(
        functools.partial(_kmean_sample_body, layer=layer),
        grid_spec=pltpu.PrefetchScalarGridSpec(
            num_scalar_prefetch=1,
            grid=(db, n_pages // KMEAN_PAGES),
            in_specs=[pl.BlockSpec(memory_space=pl.ANY)],
            out_specs=pl.BlockSpec((1, bps, N_KV_HEADS, HEAD_DIM), lambda b, c, pt: (b, c, 0, 0)),
            scratch_shapes=[pltpu.VMEM((2, KMEAN_PAGES) + page_shape, F32),
                            pltpu.SemaphoreType.DMA((2,))]),
        out_shape=jax.ShapeDtypeStruct((db, n_full, N_KV_HEADS, HEAD_DIM), F32),
        compiler_params=_params("arbitrary", "arbitrary"),
        name="kmean_sample",
    )(page_table.reshape(-1), cache_k)


def _attn_sample_body(pages_ref, q_ref, kn_ref, vn_ref, *refs):
    n_sel = (len(refs) - 1) // 2
    k_refs, v_refs, o_ref = refs[:n_sel], refs[n_sel:2 * n_sel], refs[-1]
    q = _mxu_round(q_ref[0])
    s_new = jnp.sum(q * _mxu_round(kn_ref[0]), axis=1, keepdims=True) * ATTN_SCALE
    s_sel = [jnp.sum(_mxu_round(k_ref[0, 0, 0]) * q, axis=1, keepdims=True) * ATTN_SCALE
             for k_ref in k_refs]
    m = s_new
    for s in s_sel:
        m = jnp.maximum(m, jnp.max(s, axis=0, keepdims=True))
    p_new = jnp.exp(s_new - m)
    denom = p_new
    out = _mxu_round(p_new) * _mxu_round(vn_ref[0])
    for s, v_ref in zip(s_sel, v_refs):
        p = jnp.exp(s - m)
        denom = denom + jnp.sum(p, axis=0, keepdims=True)
        out = out + jnp.sum(_mxu_round(p) * _mxu_round(v_ref[0, 0, 0]), axis=0, keepdims=True)
    o_ref[0] = out / denom


def _attn_sample(phys, q, k_new, v_new, cache_k, cache_v, layer):
    db = q.shape[0] // N_HEADS
    n_sel = phys.shape[0] // (db * N_HEADS)
    vec = lambda f: pl.BlockSpec((1, 1, HEAD_DIM), f)

    def page_spec(p):
        return pl.BlockSpec((1, 1, 1, PAGE_SIZE, HEAD_DIM),
                            lambda b, h, pg: (pg[(b * N_HEADS + h) * n_sel + p], layer, h // GQA_GROUP, 0, 0))

    page_specs = [page_spec(p) for p in range(n_sel)]
    return pl.pallas_call(
        _attn_sample_body,
        grid_spec=pltpu.PrefetchScalarGridSpec(
            num_scalar_prefetch=1,
            grid=(db, N_HEADS),
            in_specs=[vec(lambda b, h, pg: (b * N_HEADS + h, 0, 0)),
                      vec(lambda b, h, pg: (b * N_KV_HEADS + h // GQA_GROUP, 0, 0)),
                      vec(lambda b, h, pg: (b * N_KV_HEADS + h // GQA_GROUP, 0, 0))] + page_specs + page_specs,
            out_specs=vec(lambda b, h, pg: (b * N_HEADS + h, 0, 0))),
        out_shape=jax.ShapeDtypeStruct((db * N_HEADS, 1, HEAD_DIM), F32),
        compiler_params=_params("arbitrary", "arbitrary"),
        name="attn_sample",
    )(phys, q, k_new, v_new, *([cache_k] * n_sel), *([cache_v] * n_sel))


def _glu_body(x_ref, sh_ref, sc_ref, w_ref, o_ref, w16):
    c = o_ref.shape[1]
    h = x_ref[...] * (1 + sc_ref[0]) + sh_ref[0]

    @pl.when(pl.program_id(0) == 0)
    def _():
        w16[...] = w_ref[...].astype(MXU_DTYPE)

    ag = jnp.dot(h.astype(MXU_DTYPE), w16[...], preferred_element_type=F32)
    o_ref[...] = ag[:, :c] * jax.nn.sigmoid(ag[:, c:])


def _glu(x, shift, scale, w_in, tm, rows_per_mod):
    n, d = x.shape
    c2 = w_in.shape[1]
    r = shift.shape[1]
    mod_spec = pl.BlockSpec((1, r, d), lambda t: (t // rows_per_mod, 0, 0))
    return pl.pallas_call(
        _glu_body,
        grid=(n // tm,),
        in_specs=[pl.BlockSpec((tm, d), lambda t: (t, 0)), mod_spec, mod_spec,
                  pl.BlockSpec((d, c2), lambda t: (0, 0))],
        out_specs=pl.BlockSpec((tm, c2 // 2), lambda t: (t, 0)),
        out_shape=jax.ShapeDtypeStruct((n, c2 // 2), F32),
        scratch_shapes=[pltpu.VMEM((d, c2), MXU_DTYPE)],
        compiler_params=_params("arbitrary"),
        name="glu",
    )(x, shift, scale, w_in)


def _post_body(*refs, n_tiles, **kw):
    n_mixer = 1 if kw["mode"] == "attn" else 5
    h2_ref, lg_ref = refs[n_mixer + 9], refs[n_mixer + 10]
    t = pl.program_id(0)

    @pl.when(t < n_tiles)
    def _():
        _post_tile(*refs, **kw)

    @pl.when(t >= n_tiles)
    def _():
        h2_ref[...] = jnp.zeros(h2_ref.shape, F32)
        lg_ref[...] = jnp.zeros(lg_ref.shape, F32)


def _post_tile(*refs, mode, alpha, tiles_per_seq):
    if mode == "attn":
        a_ref, rest = refs[0], refs[1:]
    elif mode == "conv":
        u_ref, prev_ref, wdw_ref, cg_ref, cb_ref = refs[:5]
        rest = refs[5:]
    else:
        u_ref, st_ref, wdw_ref, cg_ref, cb_ref = refs[:5]
        rest = refs[5:]
    (w_ref, x_ref, g1_ref, lng_ref, lnb_ref, sh2_ref, sc2_ref, wr_ref,
     x1_ref, h2_ref, lg_ref, w16) = rest[:12]
    t = pl.program_id(0)
    tm = x_ref.shape[0]

    if mode == "attn":
        a = a_ref[...]
    else:
        wdw = _mxu_round(wdw_ref[...])
        if mode == "conv":
            ext_ref = rest[12]
            first = (t % tiles_per_seq) == 0
            ext_ref[0:CONV_HALO] = _mxu_round(jnp.where(first, 0.0, prev_ref[...]))
            ext_ref[CONV_HALO:] = _mxu_round(u_ref[...])
            off = CONV_HALO - (CONV_WIDTH - 1)
            y = ext_ref[pl.ds(off, tm), :] * wdw[0:1, :]
            for w in range(1, CONV_WIDTH):
                y = y + ext_ref[pl.ds(off + w, tm), :] * wdw[w:w + 1, :]
        else:
            y = _mxu_round(u_ref[...]) * wdw[CONV_WIDTH - 1:CONV_WIDTH, :]
            for w in range(CONV_WIDTH - 1):
                y = y + _mxu_round(st_ref[w]) * wdw[w:w + 1, :]
        a = _silu(_layer_norm(y, cg_ref[...], cb_ref[...]))

    @pl.when(t == 0)
    def _():
        w16[...] = w_ref[...].astype(MXU_DTYPE)

    f = jnp.dot(a.astype(MXU_DTYPE), w16[...], preferred_element_type=F32)
    x1 = _layer_norm(alpha * x_ref[...] + g1_ref[0] * f, lng_ref[...], lnb_ref[...])
    h2 = x1 * (1 + sc2_ref[0]) + sh2_ref[0]
    x1_ref[...] = x1
    _store_row_tiles(h2_ref, h2)
    lg_ref[...] = _dot(h2, wr_ref[...])


def _post(mode, mixer_in, w, x, gate1, ln_g, ln_b, shift2, scale2, w_router, *, tm, rows_per_mod,
          alpha, seq=None, out_rows=None, into=None):
    n, d = x.shape
    out_rows = n if out_rows is None else out_rows
    r = gate1.shape[1]
    tps = None if seq is None else seq // tm
    n_tiles = n // tm
    n_steps = -(-out_rows // tm)
    last = n_tiles - 1
    row = lambda t: (jnp.minimum(t, last), 0)
    const = lambda t: (0, 0)
    mod_spec = pl.BlockSpec((1, r, d), lambda t: (jnp.minimum(t, last) // rows_per_mod, 0, 0))
    vec_spec = pl.BlockSpec((1, d), const)
    if mode == "attn":
        mixer_specs = [pl.BlockSpec((tm, d), row)]
    elif mode == "conv":
        u, w_dw, cg, cb = mixer_in
        per = tm // CONV_HALO
        mixer_in = (u, u, w_dw, cg, cb)
        mixer_specs = [pl.BlockSpec((tm, d), row),
                       pl.BlockSpec((CONV_HALO, d), lambda t: (jnp.maximum(jnp.minimum(t, last) * per - 1, 0), 0)),
                       pl.BlockSpec(w_dw.shape, const), vec_spec, vec_spec]
    else:
        u, state, w_dw, cg, cb = mixer_in
        mixer_specs = [pl.BlockSpec((tm, d), row), pl.BlockSpec(state.shape, lambda t: (0, 0, 0)),
                       pl.BlockSpec(w_dw.shape, const), vec_spec, vec_spec]
    if mode == "attn":
        mixer_in = (mixer_in,)
    in_specs = mixer_specs + [pl.BlockSpec(w.shape, const), pl.BlockSpec((tm, d), row), mod_spec, vec_spec, vec_spec,
                              mod_spec, mod_spec, pl.BlockSpec(w_router.shape, const)]
    args = list(mixer_in) + [w, x, gate1, ln_g, ln_b, shift2, scale2, w_router]
    out_shape = [jax.ShapeDtypeStruct((n, d), F32)]
    out_specs = [pl.BlockSpec((tm, d), row)]
    aliases = {}
    if into is None:
        out_shape += [jax.ShapeDtypeStruct((out_rows * (d // LANES), LANES), F32),
                      jax.ShapeDtypeStruct((out_rows, LANES), F32)]
        out_specs += [pl.BlockSpec((tm * (d // LANES), LANES), lambda t: (t, 0)),
                      pl.BlockSpec((tm, LANES), lambda t: (t, 0))]
    else:
        h2_all, lg_all, row_block = into
        out_shape += [jax.ShapeDtypeStruct(h2_all.shape, F32), jax.ShapeDtypeStruct(lg_all.shape, F32)]
        out_specs += [pl.BlockSpec((tm * (d // LANES), LANES), lambda t: (row_block + t, 0)),
                      pl.BlockSpec((tm, LANES), lambda t: (row_block + t, 0))]
        aliases = {len(args): 1, len(args) + 1: 2}
        in_specs += [pl.BlockSpec(memory_space=pl.ANY), pl.BlockSpec(memory_space=pl.ANY)]
        args += [h2_all, lg_all]
    scratch = [pltpu.VMEM(w.shape, MXU_DTYPE)]
    if mode == "conv":
        scratch.append(pltpu.VMEM((CONV_HALO + tm, d), F32))

    def body(*refs):
        if into is not None:
            n_in = len(args)
            refs = refs[:n_in - 2] + refs[n_in:]
        _post_body(*refs, n_tiles=n_tiles, mode=mode, alpha=alpha, tiles_per_seq=tps)

    return pl.pallas_call(
        body,
        grid=(n_steps,),
        in_specs=in_specs,
        out_specs=out_specs,
        out_shape=out_shape,
        scratch_shapes=scratch,
        input_output_aliases=aliases,
        compiler_params=_params("arbitrary"),
        name="post_" + mode,
    )(*args)


def _row_tile(ref, r, per_row):
    return ref.at[pl.ds(pl.multiple_of(r * per_row, per_row), per_row)]


def _gather_rows(idx_ref, base, src_hbm, dst, sem, n_groups, per_row):
    def issue(g, c):
        for k in range(GATHER_UNROLL):
            r = g * GATHER_UNROLL + k
            pltpu.make_async_copy(_row_tile(src_hbm, idx_ref[base + r], per_row), _row_tile(dst, r, per_row), sem).start()
        return c

    lax.fori_loop(0, n_groups, issue, 0)


def _wait_rows(src_hbm, dst, sem, n_groups, per_row):
    def drain(g, c):
        for k in range(GATHER_UNROLL):
            r = g * GATHER_UNROLL + k
            pltpu.make_async_copy(_row_tile(src_hbm, 0, per_row), _row_tile(dst, r, per_row), sem).wait()
        return c

    lax.fori_loop(0, n_groups, drain, 0)


def _expert_body(tok_ref, te_ref, tv_ref, nu_ref, h_hbm, wg_ref, wu_ref, wd_ref, o_ref,
                 xbuf, wg16, wu16, wd16, sem):
    i = pl.program_id(0)
    per_row = wg16.shape[0] // LANES
    tm = o_ref.shape[0] // per_row
    n_used = nu_ref[0]
    slot = i % 2

    def groups(tile):
        return (tv_ref[tile] + GATHER_UNROLL - 1) // GATHER_UNROLL

    @pl.when(i == 0)
    def _():
        xbuf[...] = jnp.zeros(xbuf.shape, F32)
        _gather_rows(tok_ref, 0, h_hbm, xbuf.at[0], sem.at[0], groups(0), per_row)

    @pl.when(i + 1 < n_used)
    def _():
        _gather_rows(tok_ref, (i + 1) * tm, h_hbm, xbuf.at[1 - slot], sem.at[1 - slot], groups(i + 1), per_row)

    @pl.when(i < n_used)
    def _():
        changed = jnp.logical_or(i == 0, te_ref[i] != te_ref[jnp.maximum(i - 1, 0)])

        @pl.when(changed)
        def _():
            wg16[...] = wg_ref[0].astype(MXU_DTYPE)
            wu16[...] = wu_ref[0].astype(MXU_DTYPE)
            wd16[...] = wd_ref[0].astype(MXU_DTYPE)

        _wait_rows(h_hbm, xbuf.at[slot], sem.at[slot], groups(i), per_row)
        x = _load_row_tiles(xbuf.at[slot], per_row).astype(MXU_DTYPE)
        g = jnp.dot(x, wg16[...], preferred_element_type=F32)
        u = jnp.dot(x, wu16[...], preferred_element_type=F32)
        _store_row_tiles(o_ref, jnp.dot((_silu(g) * u).astype(MXU_DTYPE), wd16[...], preferred_element_type=F32))

    @pl.when(i >= n_used)
    def _():
        o_ref[...] = jnp.zeros(o_ref.shape, F32)


def _experts(row_tok, tile_e, tile_valid, n_used, h_all, w_gate, w_up, w_down, layer):
    n_rows = row_tok.shape[0]
    tm = TM_EXPERT
    d, de = w_gate.shape[-2:]
    per_row = d // LANES
    wspec_in = pl.BlockSpec((None, 1, d, de), lambda i, tok, te, tv, nu: (layer, te[i], 0, 0))
    wspec_out = pl.BlockSpec((None, 1, de, d), lambda i, tok, te, tv, nu: (layer, te[i], 0, 0))
    return pl.pallas_call(
        _expert_body,
        grid_spec=pltpu.PrefetchScalarGridSpec(
            num_scalar_prefetch=4,
            grid=(n_rows // tm,),
            in_specs=[pl.BlockSpec(memory_space=pl.ANY), wspec_in, wspec_in, wspec_out],
            out_specs=pl.BlockSpec((tm * per_row, LANES), lambda i, tok, te, tv, nu: (i, 0)),
            scratch_shapes=[pltpu.VMEM((2, tm * per_row, LANES), F32),
                            pltpu.VMEM((d, de), MXU_DTYPE), pltpu.VMEM((d, de), MXU_DTYPE),
                            pltpu.VMEM((de, d), MXU_DTYPE), pltpu.SemaphoreType.DMA((2,))]),
        out_shape=jax.ShapeDtypeStruct((n_rows * per_row, LANES), F32),
        compiler_params=_params("arbitrary", row_gather=True),
        name="experts",
    )(row_tok, tile_e, tile_valid, n_used, h_all, w_gate, w_up, w_down)


def _combine_body(dest_ref, ys_hbm, x_ref, w_ref, g_ref, lng_ref, lnb_ref, o_ref, buf, sem, *, alpha, tok0, k_stride):
    t = pl.program_id(0)
    tm, d = x_ref.shape
    per_row = d // LANES
    slot = t % 2

    def gather(tile, slot_):
        for k in range(TOP_K):
            _gather_rows(dest_ref, k * k_stride + tok0 + tile * tm, ys_hbm, buf.at[slot_, k], sem.at[slot_, k],
                         tm // GATHER_UNROLL, per_row)

    @pl.when(t == 0)
    def _():
        gather(0, 0)

    @pl.when(t + 1 < pl.num_programs(0))
    def _():
        gather(t + 1, 1 - slot)

    for k in range(TOP_K):
        _wait_rows(ys_hbm, buf.at[slot, k], sem.at[slot, k], tm // GATHER_UNROLL, per_row)
    wts = _mxu_round(w_ref[...])
    f = _mxu_round(_load_row_tiles(buf.at[slot, 0], per_row)) * wts[:, 0:1]
    for k in range(1, TOP_K):
        f = f + _mxu_round(_load_row_tiles(buf.at[slot, k], per_row)) * wts[:, k:k + 1]
    o_ref[...] = _layer_norm(alpha * x_ref[...] + g_ref[0] * f, lng_ref[...], lnb_ref[...])


def _combine(dest, ys, x, wts, gate2, ln_g, ln_b, *, tm, rows_per_mod, alpha, tok0):
    n, d = x.shape
    r = gate2.shape[1]
    vec_spec = pl.BlockSpec((1, d), lambda t, ds_: (0, 0))
    return pl.pallas_call(
        functools.partial(_combine_body, alpha=alpha, tok0=tok0, k_stride=dest.shape[0] // TOP_K),
        grid_spec=pltpu.PrefetchScalarGridSpec(
            num_scalar_prefetch=1,
            grid=(n // tm,),
            in_specs=[pl.BlockSpec(memory_space=pl.ANY),
                      pl.BlockSpec((tm, d), lambda t, ds_: (t, 0)),
                      pl.BlockSpec((tm, TOP_K), lambda t, ds_: (t, 0)),
                      pl.BlockSpec((1, r, d), lambda t, ds_: (t // rows_per_mod, 0, 0)),
                      vec_spec, vec_spec],
            out_specs=pl.BlockSpec((tm, d), lambda t, ds_: (t, 0)),
            scratch_shapes=[pltpu.VMEM((2, TOP_K, tm * (d // LANES), LANES), F32),
                            pltpu.SemaphoreType.DMA((2, TOP_K))]),
        out_shape=jax.ShapeDtypeStruct((n, d), F32),
        compiler_params=_params("arbitrary", row_gather=True),
        name="combine",
    )(dest, ys, x, wts, gate2, ln_g, ln_b)


def _top2_of_group(x, sub):
    m1 = jnp.max(x, axis=0, keepdims=True)
    i1 = jnp.min(jnp.where(x == m1, sub, EXPERTS_PER_GROUP), axis=0, keepdims=True)
    rest = jnp.where(sub == i1, -jnp.inf, x)
    m2 = jnp.max(rest, axis=0, keepdims=True)
    i2 = jnp.min(jnp.where(rest == m2, sub, EXPERTS_PER_GROUP), axis=0, keepdims=True)
    return m1 + m2, i1, i2


def _route_body(lg_ref, b_ref, e_ref, w_ref, r_ref, cnt_ref, tri_ref, base_ref, *, n_tok):
    t = pl.program_id(0)
    tt = lg_ref.shape[0]

    @pl.when(t == 0)
    def _():
        earlier = lax.broadcasted_iota(I32, (tt, tt), 0) < lax.broadcasted_iota(I32, (tt, tt), 1)
        tri_ref[...] = jnp.where(earlier, 1.0, 0.0).astype(MXU_DTYPE)
        base_ref[...] = jnp.zeros(base_ref.shape, F32)

    scores = jax.nn.sigmoid(lg_ref[...].T[:N_EXPERTS])
    biased = scores + b_ref[...]
    sub = lax.broadcasted_iota(I32, (EXPERTS_PER_GROUP, tt), 0)
    best, i1, i2 = _top2_of_group(biased[:EXPERTS_PER_GROUP], sub)
    g_sel = jnp.zeros((1, tt), I32)
    for g in range(1, N_GROUPS):
        score_g, i1_g, i2_g = _top2_of_group(biased[g * EXPERTS_PER_GROUP:(g + 1) * EXPERTS_PER_GROUP], sub)
        better = score_g > best
        best = jnp.where(better, score_g, best)
        g_sel = jnp.where(better, g, g_sel)
        i1 = jnp.where(better, i1_g, i1)
        i2 = jnp.where(better, i2_g, i2)
    eid = lax.broadcasted_iota(I32, (N_EXPERTS, tt), 0)
    valid = t * tt + lax.broadcasted_iota(I32, (1, tt), 1) < n_tok
    picks = [g_sel * EXPERTS_PER_GROUP + i1, g_sel * EXPERTS_PER_GROUP + i2]
    hit = [eid == e for e in picks]
    raw = [jnp.sum(jnp.where(h, scores, 0.0), axis=0, keepdims=True) for h in hit]
    denom = raw[0] + raw[1]
    base = base_ref[...]
    for k in range(TOP_K):
        onehot = jnp.where(hit[k] & valid, 1.0, 0.0)
        before = jnp.dot(onehot.astype(MXU_DTYPE), tri_ref[...], preferred_element_type=F32)
        rank = jnp.sum(onehot * (base + before), axis=0, keepdims=True)
        base = base + jnp.sum(onehot, axis=1, keepdims=True)
        e_ref[k:k + 1, :] = picks[k]
        w_ref[k:k + 1, :] = raw[k] / denom
        r_ref[k:k + 1, :] = rank.astype(I32)
    base_ref[...] = base
    cnt_ref[...] = jnp.broadcast_to(base, cnt_ref.shape)


def _row_tokens_body(dest_ref, gap_ref, o_ref, *, n_tok, k_stride):
    def clear(r, c):
        o_ref[r] = 0
        return c

    def put(tok, c):
        for k in range(TOP_K):
            o_ref[dest_ref[k * k_stride + tok]] = tok
        return c

    for g in range(gap_ref.shape[0] // 2):
        lax.fori_loop(gap_ref[2 * g], gap_ref[2 * g + 1], clear, 0)
    lax.fori_loop(0, n_tok, put, 0, unroll=GATHER_UNROLL)


def _row_tokens(dest, gaps, n_tok, n_rows):
    return pl.pallas_call(
        functools.partial(_row_tokens_body, n_tok=n_tok, k_stride=dest.shape[0] // TOP_K),
        in_specs=[pl.BlockSpec(memory_space=pltpu.SMEM), pl.BlockSpec(memory_space=pltpu.SMEM)],
        out_specs=pl.BlockSpec(memory_space=pltpu.SMEM),
        out_shape=jax.ShapeDtypeStruct((n_rows,), I32),
        name="row_tokens",
    )(dest, gaps)


def _route(logits, b_router, n_tok, n_rows):
    tm = TM_EXPERT
    tt = ROUTE_TILE
    n_steps = -(-n_tok // tt)
    n_pad = n_steps * tt
    pick_spec = pl.BlockSpec((TOP_K, tt), lambda t: (0, t))
    e_idx, wts, rank, cnt = pl.pallas_call(
        functools.partial(_route_body, n_tok=n_tok),
        grid=(n_steps,),
        in_specs=[pl.BlockSpec((tt, LANES), lambda t: (t, 0)), pl.BlockSpec((N_EXPERTS, 1), lambda t: (0, 0))],
        out_specs=[pick_spec, pick_spec, pick_spec, pl.BlockSpec((N_EXPERTS, LANES), lambda t: (0, 0))],
        out_shape=[jax.ShapeDtypeStruct((TOP_K, n_pad), I32), jax.ShapeDtypeStruct((TOP_K, n_pad), F32),
                   jax.ShapeDtypeStruct((TOP_K, n_pad), I32), jax.ShapeDtypeStruct((N_EXPERTS, LANES), F32)],
        scratch_shapes=[pltpu.VMEM((tt, tt), MXU_DTYPE), pltpu.VMEM((N_EXPERTS, 1), F32)],
        compiler_params=_params("arbitrary"),
        name="route",
    )(logits, b_router.astype(F32).reshape(N_EXPERTS, 1))
    counts = cnt[:, 0].astype(I32)
    padded = (counts + tm - 1) // tm * tm
    pend = jnp.cumsum(padded)
    pstart = pend - padded
    expert_ids = jnp.arange(N_EXPERTS, dtype=I32)[:, None, None]
    dest = rank + jnp.sum(jnp.where(e_idx[None] == expert_ids, pstart[:, None, None], 0), axis=0)
    gaps = jnp.stack([jnp.append(pstart + counts, pend[-1]), jnp.append(pend, n_rows)], axis=1).reshape(-1).astype(I32)
    row_tok = _row_tokens(dest.reshape(-1), gaps, n_tok, n_rows)
    n_tiles = n_rows // tm
    n_used = (pend[-1] // tm).astype(I32)
    tile_ids = jnp.arange(n_tiles, dtype=I32)
    tile_start = jnp.minimum(tile_ids, n_used - 1) * tm
    tile_e = jnp.minimum(jnp.sum(pend[None, :] <= tile_start[:, None], axis=1), N_EXPERTS - 1).astype(I32)
    tile_valid = jnp.clip((pstart + counts)[tile_e] - tile_ids * tm, 0, tm).astype(I32)
    return row_tok, tile_e, tile_valid, n_used.reshape(1), dest.reshape(-1), wts[:, :n_tok].T


def _rotary_tables(pos):
    inv_freq = ROPE_THETA ** (-jnp.arange(ROT_HALF, dtype=F32) / ROT_HALF)
    ang = pos.astype(F32)[:, None] * inv_freq[None, :]
    ones = jnp.ones((pos.shape[0], LANES - ROT_DIM), F32)
    cos_t = jnp.concatenate([jnp.cos(ang), jnp.cos(ang), ones], axis=1)
    sin_t = jnp.concatenate([jnp.sin(ang), jnp.sin(ang), 0.0 * ones], axis=1)
    return cos_t, sin_t


def kernel(x_prompt, x_sample, cache_k, cache_v, state_conv, page_table, c_prompt, c_sample,
           w_ada, b_ada, ln_g, ln_b, w_qkv, w_o, conv_w_in, conv_w_dw, conv_ln_g, conv_ln_b,
           conv_w_out, w_router, b_router, w_gate, w_up, w_down):
    batch, seq, d = x_prompt.shape
    db, dec_seq, _ = x_sample.shape
    assert dec_seq == 1 and seq % TM == 0 and TM % MOBA_BLOCK == 0 and d == N_HEADS * HEAD_DIM
    depth = w_ada.shape[0]
    n_pages = page_table.shape[1]
    past_len = n_pages * PAGE_SIZE
    assert n_pages % KMEAN_PAGES == 0 and n_pages % PAGES_PER_BLOCK == 0
    n_full = n_pages // PAGES_PER_BLOCK
    topk_s = min(MOBA_TOPK, n_full)
    alpha = (2 * depth) ** 0.25
    n_p = batch * seq
    n_all = n_p + db
    assert n_p % db == 0 and n_p % ROUTE_TILE == 0 and ROUTE_TILE % TM == 0
    n_buf = -(-n_all // ROUTE_TILE) * ROUTE_TILE
    tps = seq // TM

    c_rows = -(-(batch + db) // 8) * 8
    c_all = jnp.concatenate([c_prompt, c_sample, jnp.zeros((c_rows - batch - db, d), F32)], axis=0)
    mod = _ada(c_all, w_ada, b_ada)

    wr_pad = jnp.pad(w_router, ((0, 0), (0, LANES - N_EXPERTS))).astype(MXU_DTYPE)
    n_assign = n_all * TOP_K
    n_rows = -(-(n_assign + N_EXPERTS * (TM_EXPERT - 1)) // TM_EXPERT) * TM_EXPERT

    cos_p, sin_p = _rotary_tables(jnp.arange(seq))
    cos_s, sin_s = _rotary_tables(past_len + jnp.zeros((db,), I32))

    xp = x_prompt.reshape(n_p, d)
    xs = x_sample.reshape(db, d)
    kp_pages, vp_pages, ks_rows, vs_rows, conv_p, conv_s = [], [], [], [], [], []
    for i in range(depth):
        mp = [m[:, None, :] for m in jnp.split(mod[i, :batch], 6, axis=-1)]
        ms = [m[None] for m in jnp.split(mod[i, batch:batch + db], 6, axis=-1)]
        if i % 2 == 0:
            ia = i // 2
            q, kp, vp, kx, v16, km = _qkv_prompt(xp, mp[0], mp[1], w_qkv[ia], cos_p, sin_p, batch, seq)
            km = km.transpose(0, 2, 1, 3, 4).reshape(batch, N_KV_HEADS, seq // MOBA_BLOCK, HEAD_DIM)
            attn_p = _moba_prompt(q, kx, v16, km, batch, seq)
            kp_pages.append(kp)
            vp_pages.append(vp)

            qkv_s = _qkv_sample(xs, ms[0][0], ms[1][0], w_qkv[ia], cos_s, sin_s)
            nq = N_HEADS * HEAD_DIM
            nk = N_KV_HEADS * HEAD_DIM
            q_s = qkv_s[:, :nq].reshape(db, N_HEADS, HEAD_DIM)
            k_s = qkv_s[:, nq:nq + nk].reshape(db, N_KV_HEADS, HEAD_DIM)
            v_s = qkv_s[:, nq + nk:].reshape(db, N_KV_HEADS, HEAD_DIM)
            ks_rows.append(k_s[:, :, None, :])
            vs_rows.append(v_s[:, :, None, :])
            if topk_s > 0:
                kmean_s = _kmean_sample(page_table, cache_k, ia)
                kvh = jnp.arange(N_HEADS) // GQA_GROUP
                gate_s = jnp.einsum("bhd,bnhd->bhn", q_s, kmean_s[:, :, kvh])
                _, sel = lax.top_k(gate_s, topk_s)
                sel_pages = (sel[..., None] * PAGES_PER_BLOCK + jnp.arange(PAGES_PER_BLOCK)).reshape(db, N_HEADS, -1)
                phys = jnp.take_along_axis(page_table[:, None, :], sel_pages, axis=2).astype(I32)
                attn_s = _attn_sample(phys.reshape(-1), q_s.reshape(db * N_HEADS, 1, HEAD_DIM),
                                      k_s.reshape(db * N_KV_HEADS, 1, HEAD_DIM),
                                      v_s.reshape(db * N_KV_HEADS, 1, HEAD_DIM), cache_k, cache_v, ia)
                attn_s = attn_s.reshape(db, nq)
            else:
                attn_s = jnp.repeat(v_s, GQA_GROUP, axis=1).reshape(db, nq)
            mixer_p, mixer_s, w_mix = attn_p, attn_s, w_o[ia]
            mode_p, mode_s = "attn", "attn"
        else:
            ic = i // 2
            u_p = _glu(xp, mp[0], mp[1], conv_w_in[ic], TM, tps)
            u_s = _glu(xs, ms[0], ms[1], conv_w_in[ic], db, 1)
            cg, cb = conv_ln_g[ic][None], conv_ln_b[ic][None]
            mixer_p = (u_p, conv_w_dw[ic], cg, cb)
            mixer_s = (u_s, state_conv[ic].transpose(1, 0, 2), conv_w_dw[ic], cg, cb)
            w_mix = conv_w_out[ic]
            mode_p, mode_s = "conv", "conv_step"
            conv_p.append(u_p.reshape(batch, seq, d)[:, seq - (CONV_WIDTH - 1):])
            conv_s.append(jnp.concatenate([state_conv[ic][:, 1:], u_s[:, None, :]], axis=1))

        x1p, h2_all, lg_all = _post(mode_p, mixer_p, w_mix, xp, mp[2], ln_g[i, 0][None], ln_b[i, 0][None],
                                    mp[3], mp[4], wr_pad, tm=TM, rows_per_mod=tps, alpha=alpha,
                                    seq=seq, out_rows=n_buf)
        x1s, h2_all, lg_all = _post(mode_s, mixer_s, w_mix, xs, ms[2], ln_g[i, 0][None], ln_b[i, 0][None],
                                    ms[3], ms[4], wr_pad, tm=db, rows_per_mod=1, alpha=alpha,
                                    into=(h2_all, lg_all, n_p // db))
        row_tok, tile_e, tile_valid, n_used, dest, wts = _route(lg_all, b_router, n_all, n_rows)
        ys = _experts(row_tok, tile_e, tile_valid, n_used, h2_all, w_gate, w_up, w_down, i)
        xp = _combine(dest, ys, x1p, wts[:n_p], mp[5], ln_g[i, 1][None], ln_b[i, 1][None],
                      tm=TM_COMBINE, rows_per_mod=seq // TM_COMBINE, alpha=alpha, tok0=0)
        xs = _combine(dest, ys, x1s, wts[n_p:], ms[5], ln_g[i, 1][None], ln_b[i, 1][None],
                      tm=db, rows_per_mod=1, alpha=alpha, tok0=n_p)

    k_prompt = jnp.concatenate(kp_pages, axis=2)
    v_prompt = jnp.concatenate(vp_pages, axis=2)
    return (xp.reshape(batch, seq, d), xs.reshape(db, 1, d), k_prompt, v_prompt, jnp.stack(conv_p, axis=0),
            jnp.stack(ks_rows, axis=1), jnp.stack(vs_rows, axis=1), jnp.stack(conv_s, axis=0))
```

```python
import functools
import math

import jax
import jax.numpy as jnp
from jax import lax
from jax.experimental import pallas as pl
from jax.experimental.pallas import tpu as pltpu

F32 = jnp.float32
I32 = jnp.int32
MXU_DTYPE = jnp.bfloat16

N_HEADS = 8
N_KV_HEADS = 2
GQA_GROUP = N_HEADS // N_KV_HEADS
HEAD_DIM = 128
ROT_DIM = HEAD_DIM // 4
ROT_HALF = ROT_DIM // 2
ROPE_THETA = 500000.0
ATTN_SCALE = HEAD_DIM ** -0.5
MOBA_BLOCK = 256
MOBA_TOPK = 3
MOBA_CHUNK = 128
EXP2_SCALE = ATTN_SCALE * math.log2(math.e)
PAGE_SIZE = 128
PAGES_PER_BLOCK = MOBA_BLOCK // PAGE_SIZE
CONV_WIDTH = 31
CONV_HALO = 32
N_EXPERTS = 32
N_GROUPS = 4
EXPERTS_PER_GROUP = N_EXPERTS // N_GROUPS
TOP_K = 2
LN_EPS = 1e-5
MASK_VALUE = -1e30
LANES = 128
VMEM_LIMIT = 56 * 1024 * 1024

TM = 512
TM_EXPERT = 256
TM_COMBINE = 256
KMEAN_PAGES = 32
ROUTE_TILE = 512
GATHER_UNROLL = 8


def _params(*sem, row_gather=False):
    return pltpu.CompilerParams(dimension_semantics=sem, vmem_limit_bytes=VMEM_LIMIT,
                                disable_bounds_checks=row_gather)


def _dot(a, b):
    return jnp.dot(a.astype(MXU_DTYPE), b.astype(MXU_DTYPE), preferred_element_type=F32)


def _dot_nt(a, b):
    return lax.dot_general(a.astype(MXU_DTYPE), b.astype(MXU_DTYPE), (((1,), (1,)), ((), ())),
                           preferred_element_type=F32)


def _mxu_round(x):
    return x.astype(MXU_DTYPE).astype(F32)


def _store_row_tiles(ref, val):
    rows, d = val.shape
    per_row = d // LANES
    for s in range(per_row):
        ref[pl.ds(s, rows, stride=per_row), :] = val[:, s * LANES:(s + 1) * LANES]


def _load_row_tiles(ref, per_row):
    rows = ref.shape[0] // per_row
    return jnp.concatenate([ref[pl.ds(s, rows, stride=per_row), :] for s in range(per_row)], axis=1)


def _layer_norm(z, g, b):
    mu = jnp.mean(z, axis=-1, keepdims=True)
    zc = z - mu
    var = jnp.mean(zc * zc, axis=-1, keepdims=True)
    return zc * lax.rsqrt(var + LN_EPS) * g + b


def _silu(x):
    return x * jax.nn.sigmoid(x)


def _rotary(xc, cos, sin, lane):
    x_up = pltpu.roll(xc, LANES - ROT_HALF, axis=1)
    x_dn = pltpu.roll(xc, ROT_HALF, axis=1)
    first = xc * cos - x_up * sin
    second = xc * cos + x_dn * sin
    return jnp.where(lane < ROT_HALF, first, jnp.where(lane < ROT_DIM, second, xc))


def _ada_body(c_ref, w_ref, b_ref, o_ref):
    o_ref[0] = _dot(_silu(c_ref[...]), w_ref[0]) + b_ref[0]


def _ada(c_all, w_ada, b_ada):
    depth, d, n6 = w_ada.shape
    rows = c_all.shape[0]
    tn = 1536
    return pl.pallas_call(
        _ada_body,
        grid=(depth, n6 // tn),
        in_specs=[pl.BlockSpec((rows, d), lambda i, j: (0, 0)),
                  pl.BlockSpec((1, d, tn), lambda i, j: (i, 0, j)),
                  pl.BlockSpec((1, 1, tn), lambda i, j: (i, 0, j))],
        out_specs=pl.BlockSpec((1, rows, tn), lambda i, j: (i, 0, j)),
        out_shape=jax.ShapeDtypeStruct((depth, rows, n6), F32),
        compiler_params=_params("arbitrary", "arbitrary"),
        name="ada",
    )(c_all, w_ada, b_ada.reshape(depth, 1, n6))


def _qkv_prompt_body(x_ref, sh_ref, sc_ref, w_ref, cos_ref, sin_ref,
                     q_ref, kp_ref, vp_ref, kx_ref, v16_ref, km_ref, w16, *, tiles_per_seq):
    t = pl.program_id(0)
    tm = x_ref.shape[0]

    @pl.when(t == 0)
    def _():
        w16[...] = w_ref[...].astype(MXU_DTYPE)

    h = x_ref[...] * (1 + sc_ref[0]) + sh_ref[0]
    qkv = jnp.dot(h.astype(MXU_DTYPE), w16[...], preferred_element_type=F32)
    cos = cos_ref[...]
    sin = sin_ref[...]
    lane = lax.broadcasted_iota(I32, (tm, LANES), 1)
    nq = N_HEADS * HEAD_DIM
    nk = N_KV_HEADS * HEAD_DIM
    for hh in range(N_HEADS):
        sl = slice(hh * HEAD_DIM, (hh + 1) * HEAD_DIM)
        q_ref[:, sl] = _rotary(qkv[:, sl], cos, sin, lane).astype(q_ref.dtype)
    row = lax.broadcasted_iota(I32, (tm, LANES), 0)
    blk = ((t % tiles_per_seq) * tm + row) // MOBA_BLOCK
    onehot = jnp.where(lane == blk, 1.0, 0.0).astype(kx_ref.dtype)
    for n in range(N_KV_HEADS):
        kc = _rotary(qkv[:, nq + n * HEAD_DIM:nq + (n + 1) * HEAD_DIM], cos, sin, lane)
        vc = qkv[:, nq + nk + n * HEAD_DIM:nq + nk + (n + 1) * HEAD_DIM]
        kp_ref[0, :, 0, n] = kc.reshape(tm // PAGE_SIZE, PAGE_SIZE, HEAD_DIM)
        vp_ref[0, :, 0, n] = vc.reshape(tm // PAGE_SIZE, PAGE_SIZE, HEAD_DIM)
        kx_ref[0, n, :, 0:HEAD_DIM] = kc.astype(kx_ref.dtype)
        kx_ref[0, n, :, HEAD_DIM:2 * HEAD_DIM] = onehot
        v16_ref[0, n] = vc.astype(v16_ref.dtype)
        km_ref[0, 0, n] = jnp.sum(kc.reshape(tm // MOBA_BLOCK, MOBA_BLOCK, HEAD_DIM), axis=1) * (1.0 / MOBA_BLOCK)


def _qkv_prompt(x, shift, scale, w_qkv, cos_t, sin_t, batch, seq):
    n, d = x.shape
    tm = TM
    tps = seq // tm
    width = w_qkv.shape[1]
    npg = seq // PAGE_SIZE
    mod_spec = pl.BlockSpec((1, 1, d), lambda t: (t // tps, 0, 0))
    rot_spec = pl.BlockSpec((tm, LANES), lambda t: (t % tps, 0))
    page_spec = pl.BlockSpec((1, tm // PAGE_SIZE, 1, N_KV_HEADS, PAGE_SIZE, HEAD_DIM),
                             lambda t: (t // tps, t % tps, 0, 0, 0, 0))
    page_shape = jax.ShapeDtypeStruct((batch, npg, 1, N_KV_HEADS, PAGE_SIZE, HEAD_DIM), F32)
    return pl.pallas_call(
        functools.partial(_qkv_prompt_body, tiles_per_seq=tps),
        grid=(n // tm,),
        in_specs=[pl.BlockSpec((tm, d), lambda t: (t, 0)), mod_spec, mod_spec,
                  pl.BlockSpec((d, width), lambda t: (0, 0)), rot_spec, rot_spec],
        out_specs=[pl.BlockSpec((tm, N_HEADS * HEAD_DIM), lambda t: (t, 0)),
                   page_spec, page_spec,
                   pl.BlockSpec((1, N_KV_HEADS, tm, 2 * HEAD_DIM), lambda t: (t // tps, 0, t % tps, 0)),
                   pl.BlockSpec((1, N_KV_HEADS, tm, HEAD_DIM), lambda t: (t // tps, 0, t % tps, 0)),
                   pl.BlockSpec((1, 1, N_KV_HEADS, tm // MOBA_BLOCK, HEAD_DIM), lambda t: (t // tps, t % tps, 0, 0, 0))],
        out_shape=[jax.ShapeDtypeStruct((n, N_HEADS * HEAD_DIM), MXU_DTYPE),
                   page_shape, page_shape,
                   jax.ShapeDtypeStruct((batch, N_KV_HEADS, seq, 2 * HEAD_DIM), MXU_DTYPE),
                   jax.ShapeDtypeStruct((batch, N_KV_HEADS, seq, HEAD_DIM), MXU_DTYPE),
                   jax.ShapeDtypeStruct((batch, tps, N_KV_HEADS, tm // MOBA_BLOCK, HEAD_DIM), F32)],
        scratch_shapes=[pltpu.VMEM((d, width), MXU_DTYPE)],
        compiler_params=_params("arbitrary"),
        name="qkv_prompt",
    )(x, shift, scale, w_qkv, cos_t, sin_t)


def _qkv_sample_body(x_ref, sh_ref, sc_ref, w_ref, cos_ref, sin_ref, o_ref):
    rows = x_ref.shape[0]
    h = x_ref[...] * (1 + sc_ref[...]) + sh_ref[...]
    qkv = _dot(h, w_ref[...])
    lane = lax.broadcasted_iota(I32, (rows, LANES), 1)
    n_rot = N_HEADS + N_KV_HEADS
    for c in range(n_rot):
        sl = slice(c * HEAD_DIM, (c + 1) * HEAD_DIM)
        o_ref[:, sl] = _rotary(qkv[:, sl], cos_ref[...], sin_ref[...], lane)
    o_ref[:, n_rot * HEAD_DIM:] = qkv[:, n_rot * HEAD_DIM:]


def _qkv_sample(x, shift, scale, w_qkv, cos_t, sin_t):
    rows, d = x.shape
    width = w_qkv.shape[1]
    full = lambda shape: pl.BlockSpec(shape, lambda: tuple(0 for _ in shape))
    return pl.pallas_call(
        _qkv_sample_body,
        in_specs=[full((rows, d)), full((rows, d)), full((rows, d)), full((d, width)),
                  full((rows, LANES)), full((rows, LANES))],
        out_specs=full((rows, width)),
        out_shape=jax.ShapeDtypeStruct((rows, width), F32),
        compiler_params=pltpu.CompilerParams(vmem_limit_bytes=VMEM_LIMIT),
        name="qkv_sample",
    )(x, shift, scale, w_qkv, cos_t, sin_t)


def _moba_prompt_body(q_ref, kx_ref, v_ref, km_ref, o_ref, qx_ref, *state):
    i = pl.program_id(2)
    rows = GQA_GROUP * MOBA_BLOCK
    n_chunks = rows // MOBA_CHUNK
    s_refs, p_refs, a_refs, acc_ref = state[0:2], state[2:4], state[4:6], state[6]
    m_refs, l_refs = state[7:7 + n_chunks], state[7 + n_chunks:]
    n_blk = km_ref.shape[2]
    for h in range(GQA_GROUP):
        qx_ref[h * MOBA_BLOCK:(h + 1) * MOBA_BLOCK, 0:HEAD_DIM] = q_ref[:, h * HEAD_DIM:(h + 1) * HEAD_DIM]

    gate = _dot_nt(km_ref[0, 0], qx_ref[:, 0:HEAD_DIM])
    blk = lax.broadcasted_iota(I32, (n_blk, rows), 0)
    valid = blk < i
    cand = jnp.where(valid, gate, -jnp.inf)
    sel = blk == i
    for _ in range(MOBA_TOPK):
        top = jnp.max(cand, axis=0, keepdims=True)
        idx = jnp.min(jnp.where(cand == top, blk, n_blk), axis=0, keepdims=True)
        pick = blk == idx
        sel = sel | (pick & valid)
        cand = jnp.where(pick, -jnp.inf, cand)
    bias = jnp.where(sel, 0.0, MASK_VALUE)
    if n_blk < LANES:
        bias = jnp.concatenate([bias, jnp.zeros((LANES - n_blk, rows), F32)], axis=0)
    qx_ref[:, HEAD_DIM:2 * HEAD_DIM] = bias.T.astype(qx_ref.dtype)

    half = MOBA_BLOCK // 2

    def issue_scores(j, slot):
        start = pl.multiple_of(j * MOBA_BLOCK, MOBA_BLOCK)
        s_refs[slot][...] = _dot_nt(qx_ref[...], kx_ref[0, 0, pl.ds(start, MOBA_BLOCK), :])

    def softmax(slot, own):
        for c in range(n_chunks):
            rs = slice(c * MOBA_CHUNK, (c + 1) * MOBA_CHUNK)
            s = s_refs[slot][rs, :]
            if own:
                qpos = (c * MOBA_CHUNK) % MOBA_BLOCK + lax.broadcasted_iota(I32, (MOBA_CHUNK, MOBA_BLOCK), 0)
                kpos = lax.broadcasted_iota(I32, (MOBA_CHUNK, MOBA_BLOCK), 1)
                s = jnp.where(kpos <= qpos, s, MASK_VALUE)
            sa, sb = s[:, :half], s[:, half:]
            top = jnp.broadcast_to(jnp.max(jnp.maximum(sa, sb), axis=1, keepdims=True), (MOBA_CHUNK, half))
            if own:
                m_new = top
            else:
                m_old = m_refs[c][...]
                m_new = jnp.maximum(m_old, top)
                alpha = jnp.exp2((m_old - m_new) * EXP2_SCALE)
                a_refs[slot][rs, :] = alpha
            pa = jnp.exp2((sa - m_new) * EXP2_SCALE)
            pb = jnp.exp2((sb - m_new) * EXP2_SCALE)
            p_refs[slot][rs, :half] = pa.astype(MXU_DTYPE)
            p_refs[slot][rs, half:] = pb.astype(MXU_DTYPE)
            if own:
                l_refs[c][...] = pa + pb
            else:
                l_refs[c][...] = alpha * l_refs[c][...] + (pa + pb)
            m_refs[c][...] = m_new

    def accumulate(j, slot, own):
        start = pl.multiple_of(j * MOBA_BLOCK, MOBA_BLOCK)
        pv = jnp.dot(p_refs[slot][...], v_ref[0, 0, pl.ds(start, MOBA_BLOCK), :], preferred_element_type=F32)
        if own:
            acc_ref[...] = pv
        else:
            acc_ref[...] = a_refs[slot][...] * acc_ref[...] + pv

    issue_scores(i, 1)
    issue_scores(0, 0)
    softmax(1, True)
    accumulate(i, 1, True)

    def pair(t, carry):
        j = 2 * t
        issue_scores(j + 1, 1)
        softmax(0, False)
        accumulate(j, 0, False)
        issue_scores(jnp.minimum(j + 2, i - 1), 0)
        softmax(1, False)
        accumulate(j + 1, 1, False)
        return carry

    lax.fori_loop(0, i // 2, pair, 0)

    @pl.when(i % 2 == 1)
    def _():
        softmax(0, False)
        accumulate(i - 1, 0, False)

    per_head = MOBA_BLOCK // MOBA_CHUNK
    for c in range(n_chunks):
        rs = slice(c * MOBA_CHUNK, (c + 1) * MOBA_CHUNK)
        out = acc_ref[rs, :] / jnp.sum(l_refs[c][...], axis=1, keepdims=True)
        h, part = divmod(c, per_head)
        o_ref[part * MOBA_CHUNK:(part + 1) * MOBA_CHUNK, h * HEAD_DIM:(h + 1) * HEAD_DIM] = out.astype(o_ref.dtype)


def _moba_prompt(q, kx, v16, k_mean, batch, seq):
    n = q.shape[0]
    nq = seq // MOBA_BLOCK
    rows = GQA_GROUP * MOBA_BLOCK
    gw = GQA_GROUP * HEAD_DIM
    return pl.pallas_call(
        _moba_prompt_body,
        grid=(batch, N_KV_HEADS, nq),
        in_specs=[pl.BlockSpec((MOBA_BLOCK, gw), lambda b, g, i: (b * nq + i, g)),
                  pl.BlockSpec((1, 1, seq, 2 * HEAD_DIM), lambda b, g, i: (b, g, 0, 0)),
                  pl.BlockSpec((1, 1, seq, HEAD_DIM), lambda b, g, i: (b, g, 0, 0)),
                  pl.BlockSpec((1, 1, nq, HEAD_DIM), lambda b, g, i: (b, g, 0, 0))],
        out_specs=pl.BlockSpec((MOBA_BLOCK, gw), lambda b, g, i: (b * nq + i, g)),
        out_shape=jax.ShapeDtypeStruct((n, N_HEADS * HEAD_DIM), MXU_DTYPE),
        scratch_shapes=([pltpu.VMEM((rows, 2 * HEAD_DIM), MXU_DTYPE)]
                        + [pltpu.VMEM((rows, MOBA_BLOCK), F32)] * 2
                        + [pltpu.VMEM((rows, MOBA_BLOCK), MXU_DTYPE)] * 2
                        + [pltpu.VMEM((rows, MOBA_BLOCK // 2), F32)] * 2
                        + [pltpu.VMEM((rows, HEAD_DIM), F32)]
                        + [pltpu.VMEM((MOBA_CHUNK, MOBA_BLOCK // 2), F32)] * (2 * (rows // MOBA_CHUNK))),
        compiler_params=_params("arbitrary", "arbitrary", "arbitrary"),
        name="moba_prompt",
    )(q, kx, v16, k_mean)


def _kmean_copy(pt_ref, ck_hbm, buf, sem, layer, step, slot, p):
    phys = pt_ref[step * KMEAN_PAGES + p]
    return pltpu.make_async_copy(ck_hbm.at[phys, layer], buf.at[slot, p], sem.at[slot])


def _kmean_sample_body(pt_ref, ck_hbm, o_ref, buf, sem, *, layer):
    nc = pl.num_programs(1)
    step = pl.program_id(0) * nc + pl.program_id(1)
    total = pl.num_programs(0) * nc
    slot = step % 2

    def start(step_, slot_):
        for p in range(KMEAN_PAGES):
            _kmean_copy(pt_ref, ck_hbm, buf, sem, layer, step_, slot_, p).start()

    @pl.when(step == 0)
    def _():
        start(step, slot)

    @pl.when(step + 1 < total)
    def _():
        start(step + 1, 1 - slot)

    for p in range(KMEAN_PAGES):
        _kmean_copy(pt_ref, ck_hbm, buf, sem, layer, step, slot, p).wait()
    for jb in range(KMEAN_PAGES // PAGES_PER_BLOCK):
        for n in range(N_KV_HEADS):
            acc = jnp.zeros((1, HEAD_DIM), F32)
            for pp in range(PAGES_PER_BLOCK):
                acc = acc + jnp.sum(buf[slot, jb * PAGES_PER_BLOCK + pp, n], axis=0, keepdims=True)
            o_ref[0, jb, pl.ds(n, 1), :] = acc * (1.0 / MOBA_BLOCK)


def _kmean_sample(page_table, cache_k, layer):
    db, n_pages = page_table.shape
    n_full = n_pages // PAGES_PER_BLOCK
    page_shape = cache_k.shape[2:]
    bps = KMEAN_PAGES // PAGES_PER_BLOCK
    return pl.pallas_call(
        functools.partial(_kmean_sample_body, layer=layer),
        grid_spec=pltpu.PrefetchScalarGridSpec(
            num_scalar_prefetch=1,
            grid=(db, n_pages // KMEAN_PAGES),
            in_specs=[pl.BlockSpec(memory_space=pl.ANY)],
            out_specs=pl.BlockSpec((1, bps, N_KV_HEADS, HEAD_DIM), lambda b, c, pt: (b, c, 0, 0)),
            scratch_shapes=[pltpu.VMEM((2, KMEAN_PAGES) + page_shape, F32),
                            pltpu.SemaphoreType.DMA((2,))]),
        out_shape=jax.ShapeDtypeStruct((db, n_full, N_KV_HEADS, HEAD_DIM), F32),
        compiler_params=_params("arbitrary", "arbitrary"),
        name="kmean_sample",
    )(page_table.reshape(-1), cache_k)


def _attn_sample_body(pages_ref, q_ref, kn_ref, vn_ref, *refs):
    n_sel = (len(refs) - 1) // 2
    k_refs, v_refs, o_ref = refs[:n_sel], refs[n_sel:2 * n_sel], refs[-1]
    q = _mxu_round(q_ref[0])
    s_new = jnp.sum(q * _mxu_round(kn_ref[0]), axis=1, keepdims=True) * ATTN_SCALE
    s_sel = [jnp.sum(_mxu_round(k_ref[0, 0, 0]) * q, axis=1, keepdims=True) * ATTN_SCALE
             for k_ref in k_refs]
    m = s_new
    for s in s_sel:
        m = jnp.maximum(m, jnp.max(s, axis=0, keepdims=True))
    p_new = jnp.exp(s_new - m)
    denom = p_new
    out = _mxu_round(p_new) * _mxu_round(vn_ref[0])
    for s, v_ref in zip(s_sel, v_refs):
        p = jnp.exp(s - m)
        denom = denom + jnp.sum(p, axis=0, keepdims=True)
        out = out + jnp.sum(_mxu_round(p) * _mxu_round(v_ref[0, 0, 0]), axis=0, keepdims=True)
    o_ref[0] = out / denom


def _attn_sample(phys, q, k_new, v_new, cache_k, cache_v, layer):
    db = q.shape[0] // N_HEADS
    n_sel = phys.shape[0] // (db * N_HEADS)
    vec = lambda f: pl.BlockSpec((1, 1, HEAD_DIM), f)

    def page_spec(p):
        return pl.BlockSpec((1, 1, 1, PAGE_SIZE, HEAD_DIM),
                            lambda b, h, pg: (pg[(b * N_HEADS + h) * n_sel + p], layer, h // GQA_GROUP, 0, 0))

    page_specs = [page_spec(p) for p in range(n_sel)]
    return pl.pallas_call(
        _attn_sample_body,
        grid_spec=pltpu.PrefetchScalarGridSpec(
            num_scalar_prefetch=1,
            grid=(db, N_HEADS),
            in_specs=[vec(lambda b, h, pg: (b * N_HEADS + h, 0, 0)),
                      vec(lambda b, h, pg: (b * N_KV_HEADS + h // GQA_GROUP, 0, 0)),
                      vec(lambda b, h, pg: (b * N_KV_HEADS + h // GQA_GROUP, 0, 0))] + page_specs + page_specs,
            out_specs=vec(lambda b, h, pg: (b * N_HEADS + h, 0, 0))),
        out_shape=jax.ShapeDtypeStruct((db * N_HEADS, 1, HEAD_DIM), F32),
        compiler_params=_params("arbitrary", "arbitrary"),
        name="attn_sample",
    )(phys, q, k_new, v_new, *([cache_k] * n_sel), *([cache_v] * n_sel))


def _glu_body(x_ref, sh_ref, sc_ref, w_ref, o_ref, w16):
    c = o_ref.shape[1]
    h = x_ref[...] * (1 + sc_ref[0]) + sh_ref[0]

    @pl.when(pl.program_id(0) == 0)
    def _():
        w16[...] = w_ref[...].astype(MXU_DTYPE)

    ag = jnp.dot(h.astype(MXU_DTYPE), w16[...], preferred_element_type=F32)
    o_ref[...] = ag[:, :c] * jax.nn.sigmoid(ag[:, c:])


def _glu(x, shift, scale, w_in, tm, rows_per_mod):
    n, d = x.shape
    c2 = w_in.shape[1]
    r = shift.shape[1]
    mod_spec = pl.BlockSpec((1, r, d), lambda t: (t // rows_per_mod, 0, 0))
    return pl.pallas_call(
        _glu_body,
        grid=(n // tm,),
        in_specs=[pl.BlockSpec((tm, d), lambda t: (t, 0)), mod_spec, mod_spec,
                  pl.BlockSpec((d, c2), lambda t: (0, 0))],
        out_specs=pl.BlockSpec((tm, c2 // 2), lambda t: (t, 0)),
        out_shape=jax.ShapeDtypeStruct((n, c2 // 2), F32),
        scratch_shapes=[pltpu.VMEM((d, c2), MXU_DTYPE)],
        compiler_params=_params("arbitrary"),
        name="glu",
    )(x, shift, scale, w_in)


def _post_body(*refs, n_tiles, **kw):
    n_mixer = 1 if kw["mode"] == "attn" else 5
    h2_ref, lg_ref = refs[n_mixer + 9], refs[n_mixer + 10]
    t = pl.program_id(0)

    @pl.when(t < n_tiles)
    def _():
        _post_tile(*refs, **kw)

    @pl.when(t >= n_tiles)
    def _():
        h2_ref[...] = jnp.zeros(h2_ref.shape, F32)
        lg_ref[...] = jnp.zeros(lg_ref.shape, F32)


def _post_tile(*refs, mode, alpha, tiles_per_seq):
    if mode == "attn":
        a_ref, rest = refs[0], refs[1:]
    elif mode == "conv":
        u_ref, prev_ref, wdw_ref, cg_ref, cb_ref = refs[:5]
        rest = refs[5:]
    else:
        u_ref, st_ref, wdw_ref, cg_ref, cb_ref = refs[:5]
        rest = refs[5:]
    (w_ref, x_ref, g1_ref, lng_ref, lnb_ref, sh2_ref, sc2_ref, wr_ref,
     x1_ref, h2_ref, lg_ref, w16) = rest[:12]
    t = pl.program_id(0)
    tm = x_ref.shape[0]

    if mode == "attn":
        a = a_ref[...]
    else:
        wdw = _mxu_round(wdw_ref[...])
        if mode == "conv":
            ext_ref = rest[12]
            first = (t % tiles_per_seq) == 0
            ext_ref[0:CONV_HALO] = _mxu_round(jnp.where(first, 0.0, prev_ref[...]))
            ext_ref[CONV_HALO:] = _mxu_round(u_ref[...])
            off = CONV_HALO - (CONV_WIDTH - 1)
            y = ext_ref[pl.ds(off, tm), :] * wdw[0:1, :]
            for w in range(1, CONV_WIDTH):
                y = y + ext_ref[pl.ds(off + w, tm), :] * wdw[w:w + 1, :]
        else:
            y = _mxu_round(u_ref[...]) * wdw[CONV_WIDTH - 1:CONV_WIDTH, :]
            for w in range(CONV_WIDTH - 1):
                y = y + _mxu_round(st_ref[w]) * wdw[w:w + 1, :]
        a = _silu(_layer_norm(y, cg_ref[...], cb_ref[...]))

    @pl.when(t == 0)
    def _():
        w16[...] = w_ref[...].astype(MXU_DTYPE)

    f = jnp.dot(a.astype(MXU_DTYPE), w16[...], preferred_element_type=F32)
    x1 = _layer_norm(alpha * x_ref[...] + g1_ref[0] * f, lng_ref[...], lnb_ref[...])
    h2 = x1 * (1 + sc2_ref[0]) + sh2_ref[0]
    x1_ref[...] = x1
    _store_row_tiles(h2_ref, h2)
    lg_ref[...] = _dot(h2, wr_ref[...])


def _post(mode, mixer_in, w, x, gate1, ln_g, ln_b, shift2, scale2, w_router, *, tm, rows_per_mod,
          alpha, seq=None, out_rows=None, into=None):
    n, d = x.shape
    out_rows = n if out_rows is None else out_rows
    r = gate1.shape[1]
    tps = None if seq is None else seq // tm
    n_tiles = n // tm
    n_steps = -(-out_rows // tm)
    last = n_tiles - 1
    row = lambda t: (jnp.minimum(t, last), 0)
    const = lambda t: (0, 0)
    mod_spec = pl.BlockSpec((1, r, d), lambda t: (jnp.minimum(t, last) // rows_per_mod, 0, 0))
    vec_spec = pl.BlockSpec((1, d), const)
    if mode == "attn":
        mixer_specs = [pl.BlockSpec((tm, d), row)]
    elif mode == "conv":
        u, w_dw, cg, cb = mixer_in
        per = tm // CONV_HALO
        mixer_in = (u, u, w_dw, cg, cb)
        mixer_specs = [pl.BlockSpec((tm, d), row),
                       pl.BlockSpec((CONV_HALO, d), lambda t: (jnp.maximum(jnp.minimum(t, last) * per - 1, 0), 0)),
                       pl.BlockSpec(w_dw.shape, const), vec_spec, vec_spec]
    else:
        u, state, w_dw, cg, cb = mixer_in
        mixer_specs = [pl.BlockSpec((tm, d), row), pl.BlockSpec(state.shape, lambda t: (0, 0, 0)),
                       pl.BlockSpec(w_dw.shape, const), vec_spec, vec_spec]
    if mode == "attn":
        mixer_in = (mixer_in,)
    in_specs = mixer_specs + [pl.BlockSpec(w.shape, const), pl.BlockSpec((tm, d), row), mod_spec, vec_spec, vec_spec,
                              mod_spec, mod_spec, pl.BlockSpec(w_router.shape, const)]
    args = list(mixer_in) + [w, x, gate1, ln_g, ln_b, shift2, scale2, w_router]
    out_shape = [jax.ShapeDtypeStruct((n, d), F32)]
    out_specs = [pl.BlockSpec((tm, d), row)]
    aliases = {}
    if into is None:
        out_shape += [jax.ShapeDtypeStruct((out_rows * (d // LANES), LANES), F32),
                      jax.ShapeDtypeStruct((out_rows, LANES), F32)]
        out_specs += [pl.BlockSpec((tm * (d // LANES), LANES), lambda t: (t, 0)),
                      pl.BlockSpec((tm, LANES), lambda t: (t, 0))]
    else:
        h2_all, lg_all, row_block = into
        out_shape += [jax.ShapeDtypeStruct(h2_all.shape, F32), jax.ShapeDtypeStruct(lg_all.shape, F32)]
        out_specs += [pl.BlockSpec((tm * (d // LANES), LANES), lambda t: (row_block + t, 0)),
                      pl.BlockSpec((tm, LANES), lambda t: (row_block + t, 0))]
        aliases = {len(args): 1, len(args) + 1: 2}
        in_specs += [pl.BlockSpec(memory_space=pl.ANY), pl.BlockSpec(memory_space=pl.ANY)]
        args += [h2_all, lg_all]
    scratch = [pltpu.VMEM(w.shape, MXU_DTYPE)]
    if mode == "conv":
        scratch.append(pltpu.VMEM((CONV_HALO + tm, d), F32))

    def body(*refs):
        if into is not None:
            n_in = len(args)
            refs = refs[:n_in - 2] + refs[n_in:]
        _post_body(*refs, n_tiles=n_tiles, mode=mode, alpha=alpha, tiles_per_seq=tps)

    return pl.pallas_call(
        body,
        grid=(n_steps,),
        in_specs=in_specs,
        out_specs=out_specs,
        out_shape=out_shape,
        scratch_shapes=scratch,
        input_output_aliases=aliases,
        compiler_params=_params("arbitrary"),
        name="post_" + mode,
    )(*args)


def _row_tile(ref, r, per_row):
    return ref.at[pl.ds(pl.multiple_of(r * per_row, per_row), per_row)]


def _gather_rows(idx_ref, base, src_hbm, dst, sem, n_groups, per_row):
    def issue(g, c):
        for k in range(GATHER_UNROLL):
            r = g * GATHER_UNROLL + k
            pltpu.make_async_copy(_row_tile(src_hbm, idx_ref[base + r], per_row), _row_tile(dst, r, per_row), sem).start()
        return c

    lax.fori_loop(0, n_groups, issue, 0)


def _wait_rows(src_hbm, dst, sem, n_groups, per_row):
    def drain(g, c):
        for k in range(GATHER_UNROLL):
            r = g * GATHER_UNROLL + k
            pltpu.make_async_copy(_row_tile(src_hbm, 0, per_row), _row_tile(dst, r, per_row), sem).wait()
        return c

    lax.fori_loop(0, n_groups, drain, 0)


def _expert_body(dest_ref, gap_ref, te_ref, tv_ref, nu_ref, h_hbm, wg_ref, wu_ref, wd_ref, o_ref,
                 tok_ref, xbuf, wg16, wu16, wd16, sem, *, n_tok):
    i = pl.program_id(0)
    per_row = wg16.shape[0] // LANES
    tm = o_ref.shape[0] // per_row
    n_used = nu_ref[0]
    slot = i % 2

    def groups(tile):
        return (tv_ref[tile] + GATHER_UNROLL - 1) // GATHER_UNROLL

    @pl.when(i == 0)
    def _():
        def clear(r, c):
            tok_ref[r] = 0
            return c

        def put(tok, c):
            for k in range(TOP_K):
                tok_ref[dest_ref[k * (dest_ref.shape[0] // TOP_K) + tok]] = tok
            return c

        for g in range(gap_ref.shape[0] // 2):
            lax.fori_loop(gap_ref[2 * g], gap_ref[2 * g + 1], clear, 0)
        lax.fori_loop(0, n_tok, put, 0, unroll=GATHER_UNROLL)
        xbuf[...] = jnp.zeros(xbuf.shape, F32)
        _gather_rows(tok_ref, 0, h_hbm, xbuf.at[0], sem.at[0], groups(0), per_row)

    @pl.when(i + 1 < n_used)
    def _():
        _gather_rows(tok_ref, (i + 1) * tm, h_hbm, xbuf.at[1 - slot], sem.at[1 - slot], groups(i + 1), per_row)

    @pl.when(i < n_used)
    def _():
        changed = jnp.logical_or(i == 0, te_ref[i] != te_ref[jnp.maximum(i - 1, 0)])

        @pl.when(changed)
        def _():
            wg16[...] = wg_ref[0].astype(MXU_DTYPE)
            wu16[...] = wu_ref[0].astype(MXU_DTYPE)
            wd16[...] = wd_ref[0].astype(MXU_DTYPE)

        _wait_rows(h_hbm, xbuf.at[slot], sem.at[slot], groups(i), per_row)
        x = _load_row_tiles(xbuf.at[slot], per_row).astype(MXU_DTYPE)
        g = jnp.dot(x, wg16[...], preferred_element_type=F32)
        u = jnp.dot(x, wu16[...], preferred_element_type=F32)
        _store_row_tiles(o_ref, jnp.dot((_silu(g) * u).astype(MXU_DTYPE), wd16[...], preferred_element_type=F32))

    @pl.when(i >= n_used)
    def _():
        o_ref[...] = jnp.zeros(o_ref.shape, F32)


def _experts(dest, gaps, tile_e, tile_valid, n_used, h_all, w_gate, w_up, w_down, layer, n_tok, n_rows):
    tm = TM_EXPERT
    d, de = w_gate.shape[-2:]
    per_row = d // LANES
    wspec_in = pl.BlockSpec((None, 1, d, de), lambda i, ds_, gp, te, tv, nu: (layer, te[i], 0, 0))
    wspec_out = pl.BlockSpec((None, 1, de, d), lambda i, ds_, gp, te, tv, nu: (layer, te[i], 0, 0))
    return pl.pallas_call(
        functools.partial(_expert_body, n_tok=n_tok),
        grid_spec=pltpu.PrefetchScalarGridSpec(
            num_scalar_prefetch=5,
            grid=(n_rows // tm,),
            in_specs=[pl.BlockSpec(memory_space=pl.ANY), wspec_in, wspec_in, wspec_out],
            out_specs=pl.BlockSpec((tm * per_row, LANES), lambda i, ds_, gp, te, tv, nu: (i, 0)),
            scratch_shapes=[pltpu.SMEM((n_rows,), I32), pltpu.VMEM((2, tm * per_row, LANES), F32),
                            pltpu.VMEM((d, de), MXU_DTYPE), pltpu.VMEM((d, de), MXU_DTYPE),
                            pltpu.VMEM((de, d), MXU_DTYPE), pltpu.SemaphoreType.DMA((2,))]),
        out_shape=jax.ShapeDtypeStruct((n_rows * per_row, LANES), F32),
        compiler_params=_params("arbitrary", row_gather=True),
        name="experts",
    )(dest, gaps, tile_e, tile_valid, n_used, h_all, w_gate, w_up, w_down)


def _combine_body(dest_ref, ys_hbm, x_ref, w_ref, g_ref, lng_ref, lnb_ref, o_ref, buf, sem, *, alpha, tok0, k_stride):
    t = pl.program_id(0)
    tm, d = x_ref.shape
    per_row = d // LANES
    slot = t % 2

    def gather(tile, slot_):
        for k in range(TOP_K):
            _gather_rows(dest_ref, k * k_stride + tok0 + tile * tm, ys_hbm, buf.at[slot_, k], sem.at[slot_, k],
                         tm // GATHER_UNROLL, per_row)

    @pl.when(t == 0)
    def _():
        gather(0, 0)

    @pl.when(t + 1 < pl.num_programs(0))
    def _():
        gather(t + 1, 1 - slot)

    for k in range(TOP_K):
        _wait_rows(ys_hbm, buf.at[slot, k], sem.at[slot, k], tm // GATHER_UNROLL, per_row)
    wts = _mxu_round(w_ref[...])
    f = _mxu_round(_load_row_tiles(buf.at[slot, 0], per_row)) * wts[:, 0:1]
    for k in range(1, TOP_K):
        f = f + _mxu_round(_load_row_tiles(buf.at[slot, k], per_row)) * wts[:, k:k + 1]
    o_ref[...] = _layer_norm(alpha * x_ref[...] + g_ref[0] * f, lng_ref[...], lnb_ref[...])


def _combine(dest, ys, x, wts, gate2, ln_g, ln_b, *, tm, rows_per_mod, alpha, tok0):
    n, d = x.shape
    r = gate2.shape[1]
    vec_spec = pl.BlockSpec((1, d), lambda t, ds_: (0, 0))
    return pl.pallas_call(
        functools.partial(_combine_body, alpha=alpha, tok0=tok0, k_stride=dest.shape[0] // TOP_K),
        grid_spec=pltpu.PrefetchScalarGridSpec(
            num_scalar_prefetch=1,
            grid=(n // tm,),
            in_specs=[pl.BlockSpec(memory_space=pl.ANY),
                      pl.BlockSpec((tm, d), lambda t, ds_: (t, 0)),
                      pl.BlockSpec((tm, TOP_K), lambda t, ds_: (t, 0)),
                      pl.BlockSpec((1, r, d), lambda t, ds_: (t // rows_per_mod, 0, 0)),
                      vec_spec, vec_spec],
            out_specs=pl.BlockSpec((tm, d), lambda t, ds_: (t, 0)),
            scratch_shapes=[pltpu.VMEM((2, TOP_K, tm * (d // LANES), LANES), F32),
                            pltpu.SemaphoreType.DMA((2, TOP_K))]),
        out_shape=jax.ShapeDtypeStruct((n, d), F32),
        compiler_params=_params("arbitrary", row_gather=True),
        name="combine",
    )(dest, ys, x, wts, gate2, ln_g, ln_b)


def _top2_of_group(x, sub):
    m1 = jnp.max(x, axis=0, keepdims=True)
    i1 = jnp.min(jnp.where(x == m1, sub, EXPERTS_PER_GROUP), axis=0, keepdims=True)
    rest = jnp.where(sub == i1, -jnp.inf, x)
    m2 = jnp.max(rest, axis=0, keepdims=True)
    i2 = jnp.min(jnp.where(rest == m2, sub, EXPERTS_PER_GROUP), axis=0, keepdims=True)
    return m1 + m2, i1, i2


def _route_body(lg_ref, b_ref, e_ref, w_ref, r_ref, cnt_ref, tri_ref, base_ref, *, n_tok):
    t = pl.program_id(0)
    tt = lg_ref.shape[0]

    @pl.when(t == 0)
    def _():
        earlier = lax.broadcasted_iota(I32, (tt, tt), 0) < lax.broadcasted_iota(I32, (tt, tt), 1)
        tri_ref[...] = jnp.where(earlier, 1.0, 0.0).astype(MXU_DTYPE)
        base_ref[...] = jnp.zeros(base_ref.shape, F32)

    scores = jax.nn.sigmoid(lg_ref[...].T[:N_EXPERTS])
    biased = scores + b_ref[...]
    sub = lax.broadcasted_iota(I32, (EXPERTS_PER_GROUP, tt), 0)
    best, i1, i2 = _top2_of_group(biased[:EXPERTS_PER_GROUP], sub)
    g_sel = jnp.zeros((1, tt), I32)
    for g in range(1, N_GROUPS):
        score_g, i1_g, i2_g = _top2_of_group(biased[g * EXPERTS_PER_GROUP:(g + 1) * EXPERTS_PER_GROUP], sub)
        better = score_g > best
        best = jnp.where(better, score_g, best)
        g_sel = jnp.where(better, g, g_sel)
        i1 = jnp.where(better, i1_g, i1)
        i2 = jnp.where(better, i2_g, i2)
    eid = lax.broadcasted_iota(I32, (N_EXPERTS, tt), 0)
    valid = t * tt + lax.broadcasted_iota(I32, (1, tt), 1) < n_tok
    picks = [g_sel * EXPERTS_PER_GROUP + i1, g_sel * EXPERTS_PER_GROUP + i2]
    hit = [eid == e for e in picks]
    raw = [jnp.sum(jnp.where(h, scores, 0.0), axis=0, keepdims=True) for h in hit]
    denom = raw[0] + raw[1]
    base = base_ref[...]
    for k in range(TOP_K):
        onehot = jnp.where(hit[k] & valid, 1.0, 0.0)
        before = jnp.dot(onehot.astype(MXU_DTYPE), tri_ref[...], preferred_element_type=F32)
        rank = jnp.sum(onehot * (base + before), axis=0, keepdims=True)
        base = base + jnp.sum(onehot, axis=1, keepdims=True)
        e_ref[k:k + 1, :] = picks[k]
        w_ref[k:k + 1, :] = raw[k] / denom
        r_ref[k:k + 1, :] = rank.astype(I32)
    base_ref[...] = base
    cnt_ref[...] = jnp.broadcast_to(base, cnt_ref.shape)


def _route(logits, b_router, n_tok, n_rows):
    tm = TM_EXPERT
    tt = ROUTE_TILE
    n_steps = -(-n_tok // tt)
    n_pad = n_steps * tt
    pick_spec = pl.BlockSpec((TOP_K, tt), lambda t: (0, t))
    e_idx, wts, rank, cnt = pl.pallas_call(
        functools.partial(_route_body, n_tok=n_tok),
        grid=(n_steps,),
        in_specs=[pl.BlockSpec((tt, LANES), lambda t: (t, 0)), pl.BlockSpec((N_EXPERTS, 1), lambda t: (0, 0))],
        out_specs=[pick_spec, pick_spec, pick_spec, pl.BlockSpec((N_EXPERTS, LANES), lambda t: (0, 0))],
        out_shape=[jax.ShapeDtypeStruct((TOP_K, n_pad), I32), jax.ShapeDtypeStruct((TOP_K, n_pad), F32),
                   jax.ShapeDtypeStruct((TOP_K, n_pad), I32), jax.ShapeDtypeStruct((N_EXPERTS, LANES), F32)],
        scratch_shapes=[pltpu.VMEM((tt, tt), MXU_DTYPE), pltpu.VMEM((N_EXPERTS, 1), F32)],
        compiler_params=_params("arbitrary"),
        name="route",
    )(logits, b_router.astype(F32).reshape(N_EXPERTS, 1))
    counts = cnt[:, 0].astype(I32)
    padded = (counts + tm - 1) // tm * tm
    pend = jnp.cumsum(padded)
    pstart = pend - padded
    expert_ids = jnp.arange(N_EXPERTS, dtype=I32)[:, None, None]
    dest = rank + jnp.sum(jnp.where(e_idx[None] == expert_ids, pstart[:, None, None], 0), axis=0)
    gaps = jnp.stack([pstart + counts, pend], axis=1).reshape(-1).astype(I32)
    n_tiles = n_rows // tm
    n_used = (pend[-1] // tm).astype(I32)
    tile_ids = jnp.arange(n_tiles, dtype=I32)
    tile_start = jnp.minimum(tile_ids, n_used - 1) * tm
    tile_e = jnp.minimum(jnp.sum(pend[None, :] <= tile_start[:, None], axis=1), N_EXPERTS - 1).astype(I32)
    tile_valid = jnp.clip((pstart + counts)[tile_e] - tile_ids * tm, 0, tm).astype(I32)
    return gaps, tile_e, tile_valid, n_used.reshape(1), dest.reshape(-1), wts[:, :n_tok].T


def _top_k_indices(x, k):
    iota = lax.broadcasted_iota(I32, x.shape, x.ndim - 1)
    picks = []
    for _ in range(k):
        top = jnp.max(x, axis=-1, keepdims=True)
        idx = jnp.min(jnp.where(x == top, iota, x.shape[-1]), axis=-1, keepdims=True)
        picks.append(idx)
        x = jnp.where(iota == idx, -jnp.inf, x)
    return jnp.concatenate(picks, axis=-1)


def _rotary_tables(pos):
    inv_freq = ROPE_THETA ** (-jnp.arange(ROT_HALF, dtype=F32) / ROT_HALF)
    ang = pos.astype(F32)[:, None] * inv_freq[None, :]
    ones = jnp.ones((pos.shape[0], LANES - ROT_DIM), F32)
    cos_t = jnp.concatenate([jnp.cos(ang), jnp.cos(ang), ones], axis=1)
    sin_t = jnp.concatenate([jnp.sin(ang), jnp.sin(ang), 0.0 * ones], axis=1)
    return cos_t, sin_t


def kernel(x_prompt, x_sample, cache_k, cache_v, state_conv, page_table, c_prompt, c_sample,
           w_ada, b_ada, ln_g, ln_b, w_qkv, w_o, conv_w_in, conv_w_dw, conv_ln_g, conv_ln_b,
           conv_w_out, w_router, b_router, w_gate, w_up, w_down):
    batch, seq, d = x_prompt.shape
    db, dec_seq, _ = x_sample.shape
    assert dec_seq == 1 and seq % TM == 0 and TM % MOBA_BLOCK == 0 and d == N_HEADS * HEAD_DIM
    depth = w_ada.shape[0]
    n_pages = page_table.shape[1]
    past_len = n_pages * PAGE_SIZE
    assert n_pages % KMEAN_PAGES == 0 and n_pages % PAGES_PER_BLOCK == 0
    n_full = n_pages // PAGES_PER_BLOCK
    topk_s = min(MOBA_TOPK, n_full)
    alpha = (2 * depth) ** 0.25
    n_p = batch * seq
    n_all = n_p + db
    assert n_p % db == 0 and n_p % ROUTE_TILE == 0 and ROUTE_TILE % TM == 0
    n_buf = -(-n_all // ROUTE_TILE) * ROUTE_TILE
    tps = seq // TM

    c_rows = -(-(batch + db) // 8) * 8
    c_all = jnp.concatenate([c_prompt, c_sample, jnp.zeros((c_rows - batch - db, d), F32)], axis=0)
    mod = _ada(c_all, w_ada, b_ada)

    wr_pad = jnp.pad(w_router, ((0, 0), (0, LANES - N_EXPERTS))).astype(MXU_DTYPE)
    n_assign = n_all * TOP_K
    n_rows = -(-(n_assign + N_EXPERTS * (TM_EXPERT - 1)) // TM_EXPERT) * TM_EXPERT

    cos_p, sin_p = _rotary_tables(jnp.arange(seq))
    cos_s, sin_s = _rotary_tables(past_len + jnp.zeros((db,), I32))

    xp = x_prompt.reshape(n_p, d)
    xs = x_sample.reshape(db, d)
    kp_pages, vp_pages, ks_rows, vs_rows, conv_p, conv_s = [], [], [], [], [], []
    for i in range(depth):
        mp = [m[:, None, :] for m in jnp.split(mod[i, :batch], 6, axis=-1)]
        ms = [m[None] for m in jnp.split(mod[i, batch:batch + db], 6, axis=-1)]
        if i % 2 == 0:
            ia = i // 2
            q, kp, vp, kx, v16, km = _qkv_prompt(xp, mp[0], mp[1], w_qkv[ia], cos_p, sin_p, batch, seq)
            km = km.transpose(0, 2, 1, 3, 4).reshape(batch, N_KV_HEADS, seq // MOBA_BLOCK, HEAD_DIM)
            attn_p = _moba_prompt(q, kx, v16, km, batch, seq)
            kp_pages.append(kp)
            vp_pages.append(vp)

            qkv_s = _qkv_sample(xs, ms[0][0], ms[1][0], w_qkv[ia], cos_s, sin_s)
            nq = N_HEADS * HEAD_DIM
            nk = N_KV_HEADS * HEAD_DIM
            q_s = qkv_s[:, :nq].reshape(db, N_HEADS, HEAD_DIM)
            k_s = qkv_s[:, nq:nq + nk].reshape(db, N_KV_HEADS, HEAD_DIM)
            v_s = qkv_s[:, nq + nk:].reshape(db, N_KV_HEADS, HEAD_DIM)
            ks_rows.append(k_s[:, :, None, :])
            vs_rows.append(v_s[:, :, None, :])
            if topk_s > 0:
                kmean_s = _kmean_sample(page_table, cache_k, ia)
                kvh = jnp.arange(N_HEADS) // GQA_GROUP
                gate_s = jnp.einsum("bhd,bnhd->bhn", q_s, kmean_s[:, :, kvh])
                sel = _top_k_indices(gate_s, topk_s)
                sel_pages = (sel[..., None] * PAGES_PER_BLOCK + jnp.arange(PAGES_PER_BLOCK)).reshape(db, N_HEADS, -1)
                phys = jnp.take_along_axis(page_table[:, None, :], sel_pages, axis=2).astype(I32)
                attn_s = _attn_sample(phys.reshape(-1), q_s.reshape(db * N_HEADS, 1, HEAD_DIM),
                                      k_s.reshape(db * N_KV_HEADS, 1, HEAD_DIM),
                                      v_s.reshape(db * N_KV_HEADS, 1, HEAD_DIM), cache_k, cache_v, ia)
                attn_s = attn_s.reshape(db, nq)
            else:
                attn_s = jnp.repeat(v_s, GQA_GROUP, axis=1).reshape(db, nq)
            mixer_p, mixer_s, w_mix = attn_p, attn_s, w_o[ia]
            mode_p, mode_s = "attn", "attn"
        else:
            ic = i // 2
            u_p = _glu(xp, mp[0], mp[1], conv_w_in[ic], TM, tps)
            u_s = _glu(xs, ms[0], ms[1], conv_w_in[ic], db, 1)
            cg, cb = conv_ln_g[ic][None], conv_ln_b[ic][None]
            mixer_p = (u_p, conv_w_dw[ic], cg, cb)
            mixer_s = (u_s, state_conv[ic].transpose(1, 0, 2), conv_w_dw[ic], cg, cb)
            w_mix = conv_w_out[ic]
            mode_p, mode_s = "conv", "conv_step"
            conv_p.append(u_p.reshape(batch, seq, d)[:, seq - (CONV_WIDTH - 1):])
            conv_s.append(jnp.concatenate([state_conv[ic][:, 1:], u_s[:, None, :]], axis=1))

        x1p, h2_all, lg_all = _post(mode_p, mixer_p, w_mix, xp, mp[2], ln_g[i, 0][None], ln_b[i, 0][None],
                                    mp[3], mp[4], wr_pad, tm=TM, rows_per_mod=tps, alpha=alpha,
                                    seq=seq, out_rows=n_buf)
        x1s, h2_all, lg_all = _post(mode_s, mixer_s, w_mix, xs, ms[2], ln_g[i, 0][None], ln_b[i, 0][None],
                                    ms[3], ms[4], wr_pad, tm=db, rows_per_mod=1, alpha=alpha,
                                    into=(h2_all, lg_all, n_p // db))
        gaps, tile_e, tile_valid, n_used, dest, wts = _route(lg_all, b_router, n_all, n_rows)
        ys = _experts(dest, gaps, tile_e, tile_valid, n_used, h2_all, w_gate, w_up, w_down, i, n_all, n_rows)
        xp = _combine(dest, ys, x1p, wts[:n_p], mp[5], ln_g[i, 1][None], ln_b[i, 1][None],
                      tm=TM_COMBINE, rows_per_mod=seq // TM_COMBINE, alpha=alpha, tok0=0)
        xs = _combine(dest, ys, x1s, wts[n_p:], ms[5], ln_g[i, 1][None], ln_b[i, 1][None],
                      tm=db, rows_per_mod=1, alpha=alpha, tok0=n_p)

    k_prompt = jnp.concatenate(kp_pages, axis=2)
    v_prompt = jnp.concatenate(vp_pages, axis=2)
    return (xp.reshape(batch, seq, d), xs.reshape(db, 1, d), k_prompt, v_prompt, jnp.stack(conv_p, axis=0),
            jnp.stack(ks_rows, axis=1), jnp.stack(vs_rows, axis=1), jnp.stack(conv_s, axis=0))
```

```python
import functools
import math

import jax
import jax.numpy as jnp
from jax import lax
from jax.experimental import pallas as pl
from jax.experimental.pallas import tpu as pltpu

F32 = jnp.float32
I32 = jnp.int32
MXU_DTYPE = jnp.bfloat16

N_HEADS = 8
N_KV_HEADS = 2
GQA_GROUP = N_HEADS // N_KV_HEADS
HEAD_DIM = 128
ROT_DIM = HEAD_DIM // 4
ROT_HALF = ROT_DIM // 2
ROPE_THETA = 500000.0
ATTN_SCALE = HEAD_DIM ** -0.5
MOBA_BLOCK = 256
MOBA_TOPK = 3
MOBA_CHUNK = 128
EXP2_SCALE = ATTN_SCALE * math.log2(math.e)
PAGE_SIZE = 128
PAGES_PER_BLOCK = MOBA_BLOCK // PAGE_SIZE
CONV_WIDTH = 31
CONV_HALO = 32
N_EXPERTS = 32
N_GROUPS = 4
EXPERTS_PER_GROUP = N_EXPERTS // N_GROUPS
TOP_K = 2
LN_EPS = 1e-5
MASK_VALUE = -1e30
LANES = 128
SUBLANES = 8
VMEM_LIMIT = 56 * 1024 * 1024

TM = 512
TM_EXPERT = 256
TM_COMBINE = 256
KMEAN_PAGES = 32
ROUTE_TILE = 512
GATHER_UNROLL = 8


def _params(*sem, row_gather=False):
    return pltpu.CompilerParams(dimension_semantics=sem, vmem_limit_bytes=VMEM_LIMIT,
                                disable_bounds_checks=row_gather)


def _dot(a, b):
    return jnp.dot(a.astype(MXU_DTYPE), b.astype(MXU_DTYPE), preferred_element_type=F32)


def _dot_nt(a, b):
    return lax.dot_general(a.astype(MXU_DTYPE), b.astype(MXU_DTYPE), (((1,), (1,)), ((), ())),
                           preferred_element_type=F32)


def _mxu_round(x):
    return x.astype(MXU_DTYPE).astype(F32)


def _store_row_tiles(ref, val):
    rows, d = val.shape
    per_row = d // LANES
    for s in range(per_row):
        ref[pl.ds(s, rows, stride=per_row), :] = val[:, s * LANES:(s + 1) * LANES]


def _load_row_tiles(ref, per_row):
    rows = ref.shape[0] // per_row
    return jnp.concatenate([ref[pl.ds(s, rows, stride=per_row), :] for s in range(per_row)], axis=1)


def _layer_norm(z, g, b):
    mu = jnp.mean(z, axis=-1, keepdims=True)
    zc = z - mu
    var = jnp.mean(zc * zc, axis=-1, keepdims=True)
    return zc * lax.rsqrt(var + LN_EPS) * g + b


def _silu(x):
    return x * jax.nn.sigmoid(x)


def _rotary(xc, cos, sin, lane):
    x_up = pltpu.roll(xc, LANES - ROT_HALF, axis=1)
    x_dn = pltpu.roll(xc, ROT_HALF, axis=1)
    first = xc * cos - x_up * sin
    second = xc * cos + x_dn * sin
    return jnp.where(lane < ROT_HALF, first, jnp.where(lane < ROT_DIM, second, xc))


def _ada_body(c_ref, w_ref, b_ref, o_ref):
    o_ref[0] = _dot(_silu(c_ref[...]), w_ref[0]) + b_ref[0]


def _ada(c_all, w_ada, b_ada):
    depth, d, n6 = w_ada.shape
    rows = c_all.shape[0]
    tn = 1536
    return pl.pallas_call(
        _ada_body,
        grid=(depth, n6 // tn),
        in_specs=[pl.BlockSpec((rows, d), lambda i, j: (0, 0)),
                  pl.BlockSpec((1, d, tn), lambda i, j: (i, 0, j)),
                  pl.BlockSpec((1, 1, tn), lambda i, j: (i, 0, j))],
        out_specs=pl.BlockSpec((1, rows, tn), lambda i, j: (i, 0, j)),
        out_shape=jax.ShapeDtypeStruct((depth, rows, n6), F32),
        compiler_params=_params("arbitrary", "arbitrary"),
        name="ada",
    )(c_all, w_ada, b_ada.reshape(depth, 1, n6))


def _qkv_prompt_body(x_ref, sh_ref, sc_ref, w_ref, cos_ref, sin_ref,
                     q_ref, kp_ref, vp_ref, kx_ref, v16_ref, km_ref, w16, *, tiles_per_seq):
    t = pl.program_id(0)
    tm = x_ref.shape[0]

    @pl.when(t == 0)
    def _():
        w16[...] = w_ref[...].astype(MXU_DTYPE)

    h = x_ref[...] * (1 + sc_ref[0]) + sh_ref[0]
    qkv = jnp.dot(h.astype(MXU_DTYPE), w16[...], preferred_element_type=F32)
    cos = cos_ref[...]
    sin = sin_ref[...]
    lane = lax.broadcasted_iota(I32, (tm, LANES), 1)
    nq = N_HEADS * HEAD_DIM
    nk = N_KV_HEADS * HEAD_DIM
    for hh in range(N_HEADS):
        sl = slice(hh * HEAD_DIM, (hh + 1) * HEAD_DIM)
        q_ref[:, sl] = _rotary(qkv[:, sl], cos, sin, lane).astype(q_ref.dtype)
    row = lax.broadcasted_iota(I32, (tm, LANES), 0)
    blk = ((t % tiles_per_seq) * tm + row) // MOBA_BLOCK
    onehot = jnp.where(lane == blk, 1.0, 0.0).astype(kx_ref.dtype)
    for n in range(N_KV_HEADS):
        kc = _rotary(qkv[:, nq + n * HEAD_DIM:nq + (n + 1) * HEAD_DIM], cos, sin, lane)
        vc = qkv[:, nq + nk + n * HEAD_DIM:nq + nk + (n + 1) * HEAD_DIM]
        kp_ref[0, :, 0, n] = kc.reshape(tm // PAGE_SIZE, PAGE_SIZE, HEAD_DIM)
        vp_ref[0, :, 0, n] = vc.reshape(tm // PAGE_SIZE, PAGE_SIZE, HEAD_DIM)
        kx_ref[0, n, :, 0:HEAD_DIM] = kc.astype(kx_ref.dtype)
        kx_ref[0, n, :, HEAD_DIM:2 * HEAD_DIM] = onehot
        v16_ref[0, n] = vc.astype(v16_ref.dtype)
        km_ref[0, 0, n] = jnp.sum(kc.reshape(tm // MOBA_BLOCK, MOBA_BLOCK, HEAD_DIM), axis=1) * (1.0 / MOBA_BLOCK)


def _qkv_prompt(x, shift, scale, w_qkv, cos_t, sin_t, batch, seq):
    n, d = x.shape
    tm = TM
    tps = seq // tm
    width = w_qkv.shape[1]
    npg = seq // PAGE_SIZE
    mod_spec = pl.BlockSpec((1, 1, d), lambda t: (t // tps, 0, 0))
    rot_spec = pl.BlockSpec((tm, LANES), lambda t: (t % tps, 0))
    page_spec = pl.BlockSpec((1, tm // PAGE_SIZE, 1, N_KV_HEADS, PAGE_SIZE, HEAD_DIM),
                             lambda t: (t // tps, t % tps, 0, 0, 0, 0))
    page_shape = jax.ShapeDtypeStruct((batch, npg, 1, N_KV_HEADS, PAGE_SIZE, HEAD_DIM), F32)
    return pl.pallas_call(
        functools.partial(_qkv_prompt_body, tiles_per_seq=tps),
        grid=(n // tm,),
        in_specs=[pl.BlockSpec((tm, d), lambda t: (t, 0)), mod_spec, mod_spec,
                  pl.BlockSpec((d, width), lambda t: (0, 0)), rot_spec, rot_spec],
        out_specs=[pl.BlockSpec((tm, N_HEADS * HEAD_DIM), lambda t: (t, 0)),
                   page_spec, page_spec,
                   pl.BlockSpec((1, N_KV_HEADS, tm, 2 * HEAD_DIM), lambda t: (t // tps, 0, t % tps, 0)),
                   pl.BlockSpec((1, N_KV_HEADS, tm, HEAD_DIM), lambda t: (t // tps, 0, t % tps, 0)),
                   pl.BlockSpec((1, 1, N_KV_HEADS, tm // MOBA_BLOCK, HEAD_DIM), lambda t: (t // tps, t % tps, 0, 0, 0))],
        out_shape=[jax.ShapeDtypeStruct((n, N_HEADS * HEAD_DIM), MXU_DTYPE),
                   page_shape, page_shape,
                   jax.ShapeDtypeStruct((batch, N_KV_HEADS, seq, 2 * HEAD_DIM), MXU_DTYPE),
                   jax.ShapeDtypeStruct((batch, N_KV_HEADS, seq, HEAD_DIM), MXU_DTYPE),
                   jax.ShapeDtypeStruct((batch, tps, N_KV_HEADS, tm // MOBA_BLOCK, HEAD_DIM), F32)],
        scratch_shapes=[pltpu.VMEM((d, width), MXU_DTYPE)],
        compiler_params=_params("arbitrary"),
        name="qkv_prompt",
    )(x, shift, scale, w_qkv, cos_t, sin_t)


def _qkv_sample_body(x_ref, sh_ref, sc_ref, w_ref, cos_ref, sin_ref, o_ref):
    rows = x_ref.shape[0]
    h = x_ref[...] * (1 + sc_ref[...]) + sh_ref[...]
    qkv = _dot(h, w_ref[...])
    lane = lax.broadcasted_iota(I32, (rows, LANES), 1)
    n_rot = N_HEADS + N_KV_HEADS
    for c in range(n_rot):
        sl = slice(c * HEAD_DIM, (c + 1) * HEAD_DIM)
        o_ref[:, sl] = _rotary(qkv[:, sl], cos_ref[...], sin_ref[...], lane)
    o_ref[:, n_rot * HEAD_DIM:] = qkv[:, n_rot * HEAD_DIM:]


def _qkv_sample(x, shift, scale, w_qkv, cos_t, sin_t):
    rows, d = x.shape
    width = w_qkv.shape[1]
    full = lambda shape: pl.BlockSpec(shape, lambda: tuple(0 for _ in shape))
    return pl.pallas_call(
        _qkv_sample_body,
        in_specs=[full((rows, d)), full((rows, d)), full((rows, d)), full((d, width)),
                  full((rows, LANES)), full((rows, LANES))],
        out_specs=full((rows, width)),
        out_shape=jax.ShapeDtypeStruct((rows, width), F32),
        compiler_params=pltpu.CompilerParams(vmem_limit_bytes=VMEM_LIMIT),
        name="qkv_sample",
    )(x, shift, scale, w_qkv, cos_t, sin_t)


def _moba_prompt_body(q_ref, kx_ref, v_ref, km_ref, o_ref, qx_ref, *state):
    i = pl.program_id(2)
    rows = GQA_GROUP * MOBA_BLOCK
    n_chunks = rows // MOBA_CHUNK
    s_refs, p_refs, a_refs, acc_ref = state[0:2], state[2:4], state[4:6], state[6]
    m_refs, l_refs = state[7:7 + n_chunks], state[7 + n_chunks:]
    n_blk = km_ref.shape[2]
    for h in range(GQA_GROUP):
        qx_ref[h * MOBA_BLOCK:(h + 1) * MOBA_BLOCK, 0:HEAD_DIM] = q_ref[:, h * HEAD_DIM:(h + 1) * HEAD_DIM]

    gate = _dot_nt(km_ref[0, 0], qx_ref[:, 0:HEAD_DIM])
    blk = lax.broadcasted_iota(I32, (n_blk, rows), 0)
    valid = blk < i
    cand = jnp.where(valid, gate, -jnp.inf)
    sel = blk == i
    for _ in range(MOBA_TOPK):
        top = jnp.max(cand, axis=0, keepdims=True)
        idx = jnp.min(jnp.where(cand == top, blk, n_blk), axis=0, keepdims=True)
        pick = blk == idx
        sel = sel | (pick & valid)
        cand = jnp.where(pick, -jnp.inf, cand)
    bias = jnp.where(sel, 0.0, MASK_VALUE)
    if n_blk < LANES:
        bias = jnp.concatenate([bias, jnp.zeros((LANES - n_blk, rows), F32)], axis=0)
    qx_ref[:, HEAD_DIM:2 * HEAD_DIM] = bias.T.astype(qx_ref.dtype)

    half = MOBA_BLOCK // 2

    def issue_scores(j, slot):
        start = pl.multiple_of(j * MOBA_BLOCK, MOBA_BLOCK)
        s_refs[slot][...] = _dot_nt(qx_ref[...], kx_ref[0, 0, pl.ds(start, MOBA_BLOCK), :])

    def softmax(slot, own):
        for c in range(n_chunks):
            rs = slice(c * MOBA_CHUNK, (c + 1) * MOBA_CHUNK)
            s = s_refs[slot][rs, :]
            if own:
                qpos = (c * MOBA_CHUNK) % MOBA_BLOCK + lax.broadcasted_iota(I32, (MOBA_CHUNK, MOBA_BLOCK), 0)
                kpos = lax.broadcasted_iota(I32, (MOBA_CHUNK, MOBA_BLOCK), 1)
                s = jnp.where(kpos <= qpos, s, MASK_VALUE)
            sa, sb = s[:, :half], s[:, half:]
            top = jnp.broadcast_to(jnp.max(jnp.maximum(sa, sb), axis=1, keepdims=True), (MOBA_CHUNK, half))
            if own:
                m_new = top
            else:
                m_old = m_refs[c][...]
                m_new = jnp.maximum(m_old, top)
                alpha = jnp.exp2((m_old - m_new) * EXP2_SCALE)
                a_refs[slot][rs, :] = alpha
            pa = jnp.exp2((sa - m_new) * EXP2_SCALE)
            pb = jnp.exp2((sb - m_new) * EXP2_SCALE)
            p_refs[slot][rs, :half] = pa.astype(MXU_DTYPE)
            p_refs[slot][rs, half:] = pb.astype(MXU_DTYPE)
            if own:
                l_refs[c][...] = pa + pb
            else:
                l_refs[c][...] = alpha * l_refs[c][...] + (pa + pb)
            m_refs[c][...] = m_new

    def accumulate(j, slot, own):
        start = pl.multiple_of(j * MOBA_BLOCK, MOBA_BLOCK)
        pv = jnp.dot(p_refs[slot][...], v_ref[0, 0, pl.ds(start, MOBA_BLOCK), :], preferred_element_type=F32)
        if own:
            acc_ref[...] = pv
        else:
            acc_ref[...] = a_refs[slot][...] * acc_ref[...] + pv

    issue_scores(i, 1)
    issue_scores(0, 0)
    softmax(1, True)
    accumulate(i, 1, True)

    def pair(t, carry):
        j = 2 * t
        issue_scores(j + 1, 1)
        softmax(0, False)
        accumulate(j, 0, False)
        issue_scores(jnp.minimum(j + 2, i - 1), 0)
        softmax(1, False)
        accumulate(j + 1, 1, False)
        return carry

    lax.fori_loop(0, i // 2, pair, 0)

    @pl.when(i % 2 == 1)
    def _():
        softmax(0, False)
        accumulate(i - 1, 0, False)

    per_head = MOBA_BLOCK // MOBA_CHUNK
    for c in range(n_chunks):
        rs = slice(c * MOBA_CHUNK, (c + 1) * MOBA_CHUNK)
        out = acc_ref[rs, :] / jnp.sum(l_refs[c][...], axis=1, keepdims=True)
        h, part = divmod(c, per_head)
        o_ref[part * MOBA_CHUNK:(part + 1) * MOBA_CHUNK, h * HEAD_DIM:(h + 1) * HEAD_DIM] = out.astype(o_ref.dtype)


def _moba_prompt(q, kx, v16, k_mean, batch, seq):
    n = q.shape[0]
    nq = seq // MOBA_BLOCK
    rows = GQA_GROUP * MOBA_BLOCK
    gw = GQA_GROUP * HEAD_DIM
    return pl.pallas_call(
        _moba_prompt_body,
        grid=(batch, N_KV_HEADS, nq),
        in_specs=[pl.BlockSpec((MOBA_BLOCK, gw), lambda b, g, i: (b * nq + i, g)),
                  pl.BlockSpec((1, 1, seq, 2 * HEAD_DIM), lambda b, g, i: (b, g, 0, 0)),
                  pl.BlockSpec((1, 1, seq, HEAD_DIM), lambda b, g, i: (b, g, 0, 0)),
                  pl.BlockSpec((1, 1, nq, HEAD_DIM), lambda b, g, i: (b, g, 0, 0))],
        out_specs=pl.BlockSpec((MOBA_BLOCK, gw), lambda b, g, i: (b * nq + i, g)),
        out_shape=jax.ShapeDtypeStruct((n, N_HEADS * HEAD_DIM), MXU_DTYPE),
        scratch_shapes=([pltpu.VMEM((rows, 2 * HEAD_DIM), MXU_DTYPE)]
                        + [pltpu.VMEM((rows, MOBA_BLOCK), F32)] * 2
                        + [pltpu.VMEM((rows, MOBA_BLOCK), MXU_DTYPE)] * 2
                        + [pltpu.VMEM((rows, MOBA_BLOCK // 2), F32)] * 2
                        + [pltpu.VMEM((rows, HEAD_DIM), F32)]
                        + [pltpu.VMEM((MOBA_CHUNK, MOBA_BLOCK // 2), F32)] * (2 * (rows // MOBA_CHUNK))),
        compiler_params=_params("arbitrary", "arbitrary", "arbitrary"),
        name="moba_prompt",
    )(q, kx, v16, k_mean)


def _kmean_copy(pt_ref, ck_hbm, buf, sem, layer, step, slot, p):
    phys = pt_ref[step * KMEAN_PAGES + p]
    return pltpu.make_async_copy(ck_hbm.at[phys, layer], buf.at[slot, p], sem.at[slot])


def _kmean_sample_body(pt_ref, ck_hbm, o_ref, buf, sem, *, layer):
    nc = pl.num_programs(1)
    step = pl.program_id(0) * nc + pl.program_id(1)
    total = pl.num_programs(0) * nc
    slot = step % 2

    def start(step_, slot_):
        for p in range(KMEAN_PAGES):
            _kmean_copy(pt_ref, ck_hbm, buf, sem, layer, step_, slot_, p).start()

    @pl.when(step == 0)
    def _():
        start(step, slot)

    @pl.when(step + 1 < total)
    def _():
        start(step + 1, 1 - slot)

    for p in range(KMEAN_PAGES):
        _kmean_copy(pt_ref, ck_hbm, buf, sem, layer, step, slot, p).wait()
    for jb in range(KMEAN_PAGES // PAGES_PER_BLOCK):
        for n in range(N_KV_HEADS):
            acc = jnp.zeros((1, HEAD_DIM), F32)
            for pp in range(PAGES_PER_BLOCK):
                acc = acc + jnp.sum(buf[slot, jb * PAGES_PER_BLOCK + pp, n], axis=0, keepdims=True)
            o_ref[0, jb, pl.ds(n, 1), :] = acc * (1.0 / MOBA_BLOCK)


def _kmean_sample(page_table, cache_k, layer):
    db, n_pages = page_table.shape
    n_full = n_pages // PAGES_PER_BLOCK
    page_shape = cache_k.shape[2:]
    bps = KMEAN_PAGES // PAGES_PER_BLOCK
    return pl.pallas_call(
        functools.partial(_kmean_sample_body, layer=layer),
        grid_spec=pltpu.PrefetchScalarGridSpec(
            num_scalar_prefetch=1,
            grid=(db, n_pages // KMEAN_PAGES),
            in_specs=[pl.BlockSpec(memory_space=pl.ANY)],
            out_specs=pl.BlockSpec((1, bps, N_KV_HEADS, HEAD_DIM), lambda b, c, pt: (b, c, 0, 0)),
            scratch_shapes=[pltpu.VMEM((2, KMEAN_PAGES) + page_shape, F32),
                            pltpu.SemaphoreType.DMA((2,))]),
        out_shape=jax.ShapeDtypeStruct((db, n_full, N_KV_HEADS, HEAD_DIM), F32),
        compiler_params=_params("arbitrary", "arbitrary"),
        name="kmean_sample",
    )(page_table.reshape(-1), cache_k)


def _attn_sample_body(pages_ref, q_ref, kn_ref, vn_ref, *refs):
    n_sel = (len(refs) - 1) // 2
    k_refs, v_refs, o_ref = refs[:n_sel], refs[n_sel:2 * n_sel], refs[-1]
    q = _mxu_round(q_ref[0])
    s_new = jnp.sum(q * _mxu_round(kn_ref[0]), axis=1, keepdims=True) * ATTN_SCALE
    s_sel = [jnp.sum(_mxu_round(k_ref[0, 0, 0]) * q, axis=1, keepdims=True) * ATTN_SCALE
             for k_ref in k_refs]
    m = s_new
    for s in s_sel:
        m = jnp.maximum(m, jnp.max(s, axis=0, keepdims=True))
    p_new = jnp.exp(s_new - m)
    denom = p_new
    out = _mxu_round(p_new) * _mxu_round(vn_ref[0])
    for s, v_ref in zip(s_sel, v_refs):
        p = jnp.exp(s - m)
        denom = denom + jnp.sum(p, axis=0, keepdims=True)
        out = out + jnp.sum(_mxu_round(p) * _mxu_round(v_ref[0, 0, 0]), axis=0, keepdims=True)
    o_ref[0] = out / denom


def _attn_sample(phys, q, k_new, v_new, cache_k, cache_v, layer):
    db = q.shape[0] // N_HEADS
    n_sel = phys.shape[0] // (db * N_HEADS)
    vec = lambda f: pl.BlockSpec((1, 1, HEAD_DIM), f)

    def page_spec(p):
        return pl.BlockSpec((1, 1, 1, PAGE_SIZE, HEAD_DIM),
                            lambda b, h, pg: (pg[(b * N_HEADS + h) * n_sel + p], layer, h // GQA_GROUP, 0, 0))

    page_specs = [page_spec(p) for p in range(n_sel)]
    return pl.pallas_call(
        _attn_sample_body,
        grid_spec=pltpu.PrefetchScalarGridSpec(
            num_scalar_prefetch=1,
            grid=(db, N_HEADS),
            in_specs=[vec(lambda b, h, pg: (b * N_HEADS + h, 0, 0)),
                      vec(lambda b, h, pg: (b * N_KV_HEADS + h // GQA_GROUP, 0, 0)),
                      vec(lambda b, h, pg: (b * N_KV_HEADS + h // GQA_GROUP, 0, 0))] + page_specs + page_specs,
            out_specs=vec(lambda b, h, pg: (b * N_HEADS + h, 0, 0))),
        out_shape=jax.ShapeDtypeStruct((db * N_HEADS, 1, HEAD_DIM), F32),
        compiler_params=_params("arbitrary", "arbitrary"),
        name="attn_sample",
    )(phys, q, k_new, v_new, *([cache_k] * n_sel), *([cache_v] * n_sel))


def _glu_body(x_ref, sh_ref, sc_ref, w_ref, o_ref, w16):
    c = o_ref.shape[1]
    h = x_ref[...] * (1 + sc_ref[0]) + sh_ref[0]

    @pl.when(pl.program_id(0) == 0)
    def _():
        w16[...] = w_ref[...].astype(MXU_DTYPE)

    ag = jnp.dot(h.astype(MXU_DTYPE), w16[...], preferred_element_type=F32)
    o_ref[...] = ag[:, :c] * jax.nn.sigmoid(ag[:, c:])


def _glu(x, shift, scale, w_in, tm, rows_per_mod):
    n, d = x.shape
    c2 = w_in.shape[1]
    r = shift.shape[1]
    mod_spec = pl.BlockSpec((1, r, d), lambda t: (t // rows_per_mod, 0, 0))
    return pl.pallas_call(
        _glu_body,
        grid=(n // tm,),
        in_specs=[pl.BlockSpec((tm, d), lambda t: (t, 0)), mod_spec, mod_spec,
                  pl.BlockSpec((d, c2), lambda t: (0, 0))],
        out_specs=pl.BlockSpec((tm, c2 // 2), lambda t: (t, 0)),
        out_shape=jax.ShapeDtypeStruct((n, c2 // 2), F32),
        scratch_shapes=[pltpu.VMEM((d, c2), MXU_DTYPE)],
        compiler_params=_params("arbitrary"),
        name="glu",
    )(x, shift, scale, w_in)


def _post_body(*refs, n_tiles, **kw):
    n_mixer = 1 if kw["mode"] == "attn" else 5
    h2_ref, lg_ref = refs[n_mixer + 9], refs[n_mixer + 10]
    t = pl.program_id(0)

    @pl.when(t < n_tiles)
    def _():
        _post_tile(*refs, **kw)

    @pl.when(t >= n_tiles)
    def _():
        h2_ref[...] = jnp.zeros(h2_ref.shape, F32)
        lg_ref[...] = jnp.zeros(lg_ref.shape, F32)


def _post_tile(*refs, mode, alpha, tiles_per_seq):
    if mode == "attn":
        a_ref, rest = refs[0], refs[1:]
    elif mode == "conv":
        u_ref, prev_ref, wdw_ref, cg_ref, cb_ref = refs[:5]
        rest = refs[5:]
    else:
        u_ref, st_ref, wdw_ref, cg_ref, cb_ref = refs[:5]
        rest = refs[5:]
    (w_ref, x_ref, g1_ref, lng_ref, lnb_ref, sh2_ref, sc2_ref, wr_ref,
     x1_ref, h2_ref, lg_ref, w16) = rest[:12]
    t = pl.program_id(0)
    tm = x_ref.shape[0]

    if mode == "attn":
        a = a_ref[...]
    else:
        wdw = _mxu_round(wdw_ref[...])
        if mode == "conv":
            ext_ref, z_ref = rest[12:14]
            first = (t % tiles_per_seq) == 0
            ext_ref[0:CONV_HALO] = _mxu_round(jnp.where(first, 0.0, prev_ref[...]))
            ext_ref[CONV_HALO:CONV_HALO + tm] = _mxu_round(u_ref[...])
            ext_ref[CONV_HALO + tm:] = jnp.zeros((SUBLANES, x_ref.shape[1]), F32)
            off = CONV_HALO - (CONV_WIDTH - 1)
            y = None
            for b in range(SUBLANES):
                z = None
                for a in range(-(-(off + CONV_WIDTH) // SUBLANES)):
                    w = a * SUBLANES + b - off
                    if 0 <= w < CONV_WIDTH:
                        term = ext_ref[pl.ds(a * SUBLANES, tm + SUBLANES), :] * wdw[w:w + 1, :]
                        z = term if z is None else z + term
                if b == 0:
                    y = z[0:tm]
                else:
                    z_ref[...] = z
                    y = y + z_ref[pl.ds(b, tm), :]
        else:
            y = _mxu_round(u_ref[...]) * wdw[CONV_WIDTH - 1:CONV_WIDTH, :]
            for w in range(CONV_WIDTH - 1):
                y = y + _mxu_round(st_ref[w]) * wdw[w:w + 1, :]
        a = _silu(_layer_norm(y, cg_ref[...], cb_ref[...]))

    @pl.when(t == 0)
    def _():
        w16[...] = w_ref[...].astype(MXU_DTYPE)

    f = jnp.dot(a.astype(MXU_DTYPE), w16[...], preferred_element_type=F32)
    x1 = _layer_norm(alpha * x_ref[...] + g1_ref[0] * f, lng_ref[...], lnb_ref[...])
    h2 = x1 * (1 + sc2_ref[0]) + sh2_ref[0]
    x1_ref[...] = x1
    _store_row_tiles(h2_ref, h2)
    lg_ref[...] = _dot(h2, wr_ref[...])


def _post(mode, mixer_in, w, x, gate1, ln_g, ln_b, shift2, scale2, w_router, *, tm, rows_per_mod,
          alpha, seq=None, out_rows=None, into=None):
    n, d = x.shape
    out_rows = n if out_rows is None else out_rows
    r = gate1.shape[1]
    tps = None if seq is None else seq // tm
    n_tiles = n // tm
    n_steps = -(-out_rows // tm)
    last = n_tiles - 1
    row = lambda t: (jnp.minimum(t, last), 0)
    const = lambda t: (0, 0)
    mod_spec = pl.BlockSpec((1, r, d), lambda t: (jnp.minimum(t, last) // rows_per_mod, 0, 0))
    vec_spec = pl.BlockSpec((1, d), const)
    if mode == "attn":
        mixer_specs = [pl.BlockSpec((tm, d), row)]
    elif mode == "conv":
        u, w_dw, cg, cb = mixer_in
        per = tm // CONV_HALO
        mixer_in = (u, u, w_dw, cg, cb)
        mixer_specs = [pl.BlockSpec((tm, d), row),
                       pl.BlockSpec((CONV_HALO, d), lambda t: (jnp.maximum(jnp.minimum(t, last) * per - 1, 0), 0)),
                       pl.BlockSpec(w_dw.shape, const), vec_spec, vec_spec]
    else:
        u, state, w_dw, cg, cb = mixer_in
        mixer_specs = [pl.BlockSpec((tm, d), row), pl.BlockSpec(state.shape, lambda t: (0, 0, 0)),
                       pl.BlockSpec(w_dw.shape, const), vec_spec, vec_spec]
    if mode == "attn":
        mixer_in = (mixer_in,)
    in_specs = mixer_specs + [pl.BlockSpec(w.shape, const), pl.BlockSpec((tm, d), row), mod_spec, vec_spec, vec_spec,
                              mod_spec, mod_spec, pl.BlockSpec(w_router.shape, const)]
    args = list(mixer_in) + [w, x, gate1, ln_g, ln_b, shift2, scale2, w_router]
    out_shape = [jax.ShapeDtypeStruct((n, d), F32)]
    out_specs = [pl.BlockSpec((tm, d), row)]
    aliases = {}
    if into is None:
        out_shape += [jax.ShapeDtypeStruct((out_rows * (d // LANES), LANES), F32),
                      jax.ShapeDtypeStruct((out_rows, LANES), F32)]
        out_specs += [pl.BlockSpec((tm * (d // LANES), LANES), lambda t: (t, 0)),
                      pl.BlockSpec((tm, LANES), lambda t: (t, 0))]
    else:
        h2_all, lg_all, row_block = into
        out_shape += [jax.ShapeDtypeStruct(h2_all.shape, F32), jax.ShapeDtypeStruct(lg_all.shape, F32)]
        out_specs += [pl.BlockSpec((tm * (d // LANES), LANES), lambda t: (row_block + t, 0)),
                      pl.BlockSpec((tm, LANES), lambda t: (row_block + t, 0))]
        aliases = {len(args): 1, len(args) + 1: 2}
        in_specs += [pl.BlockSpec(memory_space=pl.ANY), pl.BlockSpec(memory_space=pl.ANY)]
        args += [h2_all, lg_all]
    scratch = [pltpu.VMEM(w.shape, MXU_DTYPE)]
    if mode == "conv":
        scratch += [pltpu.VMEM((CONV_HALO + tm + SUBLANES, d), F32), pltpu.VMEM((tm + SUBLANES, d), F32)]

    def body(*refs):
        if into is not None:
            n_in = len(args)
            refs = refs[:n_in - 2] + refs[n_in:]
        _post_body(*refs, n_tiles=n_tiles, mode=mode, alpha=alpha, tiles_per_seq=tps)

    return pl.pallas_call(
        body,
        grid=(n_steps,),
        in_specs=in_specs,
        out_specs=out_specs,
        out_shape=out_shape,
        scratch_shapes=scratch,
        input_output_aliases=aliases,
        compiler_params=_params("arbitrary"),
        name="post_" + mode,
    )(*args)


def _row_tile(ref, r, per_row):
    return ref.at[pl.ds(pl.multiple_of(r * per_row, per_row), per_row)]


def _gather_rows(idx_ref, base, src_hbm, dst, sem, n_groups, per_row):
    def issue(g, c):
        for k in range(GATHER_UNROLL):
            r = g * GATHER_UNROLL + k
            pltpu.make_async_copy(_row_tile(src_hbm, idx_ref[base + r], per_row), _row_tile(dst, r, per_row), sem).start()
        return c

    lax.fori_loop(0, n_groups, issue, 0)


def _wait_rows(src_hbm, dst, sem, n_groups, per_row):
    def drain(g, c):
        for k in range(GATHER_UNROLL):
            r = g * GATHER_UNROLL + k
            pltpu.make_async_copy(_row_tile(src_hbm, 0, per_row), _row_tile(dst, r, per_row), sem).wait()
        return c

    lax.fori_loop(0, n_groups, drain, 0)


def _expert_body(dest_ref, gap_ref, te_ref, tv_ref, nu_ref, h_hbm, wg_ref, wu_ref, wd_ref, o_ref,
                 tok_ref, xbuf, wg16, wu16, wd16, sem, *, n_tok):
    i = pl.program_id(0)
    per_row = wg16.shape[0] // LANES
    tm = o_ref.shape[0] // per_row
    n_used = nu_ref[0]
    slot = i % 2

    def groups(tile):
        return (tv_ref[tile] + GATHER_UNROLL - 1) // GATHER_UNROLL

    @pl.when(i == 0)
    def _():
        def clear(r, c):
            tok_ref[r] = 0
            return c

        def put(tok, c):
            for k in range(TOP_K):
                tok_ref[dest_ref[k * (dest_ref.shape[0] // TOP_K) + tok]] = tok
            return c

        for g in range(gap_ref.shape[0] // 2):
            lax.fori_loop(gap_ref[2 * g], gap_ref[2 * g + 1], clear, 0)
        lax.fori_loop(0, n_tok, put, 0, unroll=GATHER_UNROLL)
        xbuf[...] = jnp.zeros(xbuf.shape, F32)
        _gather_rows(tok_ref, 0, h_hbm, xbuf.at[0], sem.at[0], groups(0), per_row)

    @pl.when(i + 1 < n_used)
    def _():
        _gather_rows(tok_ref, (i + 1) * tm, h_hbm, xbuf.at[1 - slot], sem.at[1 - slot], groups(i + 1), per_row)

    @pl.when(i < n_used)
    def _():
        changed = jnp.logical_or(i == 0, te_ref[i] != te_ref[jnp.maximum(i - 1, 0)])

        @pl.when(changed)
        def _():
            wg16[...] = wg_ref[0].astype(MXU_DTYPE)
            wu16[...] = wu_ref[0].astype(MXU_DTYPE)
            wd16[...] = wd_ref[0].astype(MXU_DTYPE)

        _wait_rows(h_hbm, xbuf.at[slot], sem.at[slot], groups(i), per_row)
        x = _load_row_tiles(xbuf.at[slot], per_row).astype(MXU_DTYPE)
        g = jnp.dot(x, wg16[...], preferred_element_type=F32)
        u = jnp.dot(x, wu16[...], preferred_element_type=F32)
        _store_row_tiles(o_ref, jnp.dot((_silu(g) * u).astype(MXU_DTYPE), wd16[...], preferred_element_type=F32))

    @pl.when(i >= n_used)
    def _():
        o_ref[...] = jnp.zeros(o_ref.shape, F32)


def _experts(dest, gaps, tile_e, tile_valid, n_used, h_all, w_gate, w_up, w_down, layer, n_tok, n_rows):
    tm = TM_EXPERT
    d, de = w_gate.shape[-2:]
    per_row = d // LANES
    wspec_in = pl.BlockSpec((None, 1, d, de), lambda i, ds_, gp, te, tv, nu: (layer, te[i], 0, 0))
    wspec_out = pl.BlockSpec((None, 1, de, d), lambda i, ds_, gp, te, tv, nu: (layer, te[i], 0, 0))
    return pl.pallas_call(
        functools.partial(_expert_body, n_tok=n_tok),
        grid_spec=pltpu.PrefetchScalarGridSpec(
            num_scalar_prefetch=5,
            grid=(n_rows // tm,),
            in_specs=[pl.BlockSpec(memory_space=pl.ANY), wspec_in, wspec_in, wspec_out],
            out_specs=pl.BlockSpec((tm * per_row, LANES), lambda i, ds_, gp, te, tv, nu: (i, 0)),
            scratch_shapes=[pltpu.SMEM((n_rows,), I32), pltpu.VMEM((2, tm * per_row, LANES), F32),
                            pltpu.VMEM((d, de), MXU_DTYPE), pltpu.VMEM((d, de), MXU_DTYPE),
                            pltpu.VMEM((de, d), MXU_DTYPE), pltpu.SemaphoreType.DMA((2,))]),
        out_shape=jax.ShapeDtypeStruct((n_rows * per_row, LANES), F32),
        compiler_params=_params("arbitrary", row_gather=True),
        name="experts",
    )(dest, gaps, tile_e, tile_valid, n_used, h_all, w_gate, w_up, w_down)


def _combine_body(dest_ref, ys_hbm, x_ref, w_ref, g_ref, lng_ref, lnb_ref, o_ref, buf, sem, *, alpha, tok0, k_stride):
    t = pl.program_id(0)
    tm, d = x_ref.shape
    per_row = d // LANES
    slot = t % 2

    def gather(tile, slot_):
        for k in range(TOP_K):
            _gather_rows(dest_ref, k * k_stride + tok0 + tile * tm, ys_hbm, buf.at[slot_, k], sem.at[slot_, k],
                         tm // GATHER_UNROLL, per_row)

    @pl.when(t == 0)
    def _():
        gather(0, 0)

    @pl.when(t + 1 < pl.num_programs(0))
    def _():
        gather(t + 1, 1 - slot)

    for k in range(TOP_K):
        _wait_rows(ys_hbm, buf.at[slot, k], sem.at[slot, k], tm // GATHER_UNROLL, per_row)
    wts = _mxu_round(w_ref[...])
    f = _mxu_round(_load_row_tiles(buf.at[slot, 0], per_row)) * wts[:, 0:1]
    for k in range(1, TOP_K):
        f = f + _mxu_round(_load_row_tiles(buf.at[slot, k], per_row)) * wts[:, k:k + 1]
    o_ref[...] = _layer_norm(alpha * x_ref[...] + g_ref[0] * f, lng_ref[...], lnb_ref[...])


def _combine(dest, ys, x, wts, gate2, ln_g, ln_b, *, tm, rows_per_mod, alpha, tok0):
    n, d = x.shape
    r = gate2.shape[1]
    vec_spec = pl.BlockSpec((1, d), lambda t, ds_: (0, 0))
    return pl.pallas_call(
        functools.partial(_combine_body, alpha=alpha, tok0=tok0, k_stride=dest.shape[0] // TOP_K),
        grid_spec=pltpu.PrefetchScalarGridSpec(
            num_scalar_prefetch=1,
            grid=(n // tm,),
            in_specs=[pl.BlockSpec(memory_space=pl.ANY),
                      pl.BlockSpec((tm, d), lambda t, ds_: (t, 0)),
                      pl.BlockSpec((tm, TOP_K), lambda t, ds_: (t, 0)),
                      pl.BlockSpec((1, r, d), lambda t, ds_: (t // rows_per_mod, 0, 0)),
                      vec_spec, vec_spec],
            out_specs=pl.BlockSpec((tm, d), lambda t, ds_: (t, 0)),
            scratch_shapes=[pltpu.VMEM((2, TOP_K, tm * (d // LANES), LANES), F32),
                            pltpu.SemaphoreType.DMA((2, TOP_K))]),
        out_shape=jax.ShapeDtypeStruct((n, d), F32),
        compiler_params=_params("arbitrary", row_gather=True),
        name="combine",
    )(dest, ys, x, wts, gate2, ln_g, ln_b)


def _top2_of_group(x, sub):
    m1 = jnp.max(x, axis=0, keepdims=True)
    i1 = jnp.min(jnp.where(x == m1, sub, EXPERTS_PER_GROUP), axis=0, keepdims=True)
    rest = jnp.where(sub == i1, -jnp.inf, x)
    m2 = jnp.max(rest, axis=0, keepdims=True)
    i2 = jnp.min(jnp.where(rest == m2, sub, EXPERTS_PER_GROUP), axis=0, keepdims=True)
    return m1 + m2, i1, i2


def _route_body(lg_ref, b_ref, e_ref, w_ref, r_ref, cnt_ref, tri_ref, base_ref, *, n_tok):
    t = pl.program_id(0)
    tt = lg_ref.shape[0]

    @pl.when(t == 0)
    def _():
        earlier = lax.broadcasted_iota(I32, (tt, tt), 0) < lax.broadcasted_iota(I32, (tt, tt), 1)
        tri_ref[...] = jnp.where(earlier, 1.0, 0.0).astype(MXU_DTYPE)
        base_ref[...] = jnp.zeros(base_ref.shape, F32)

    scores = jax.nn.sigmoid(lg_ref[...].T[:N_EXPERTS])
    biased = scores + b_ref[...]
    sub = lax.broadcasted_iota(I32, (EXPERTS_PER_GROUP, tt), 0)
    best, i1, i2 = _top2_of_group(biased[:EXPERTS_PER_GROUP], sub)
    g_sel = jnp.zeros((1, tt), I32)
    for g in range(1, N_GROUPS):
        score_g, i1_g, i2_g = _top2_of_group(biased[g * EXPERTS_PER_GROUP:(g + 1) * EXPERTS_PER_GROUP], sub)
        better = score_g > best
        best = jnp.where(better, score_g, best)
        g_sel = jnp.where(better, g, g_sel)
        i1 = jnp.where(better, i1_g, i1)
        i2 = jnp.where(better, i2_g, i2)
    eid = lax.broadcasted_iota(I32, (N_EXPERTS, tt), 0)
    valid = t * tt + lax.broadcasted_iota(I32, (1, tt), 1) < n_tok
    picks = [g_sel * EXPERTS_PER_GROUP + i1, g_sel * EXPERTS_PER_GROUP + i2]
    hit = [eid == e for e in picks]
    raw = [jnp.sum(jnp.where(h, scores, 0.0), axis=0, keepdims=True) for h in hit]
    denom = raw[0] + raw[1]
    base = base_ref[...]
    for k in range(TOP_K):
        onehot = jnp.where(hit[k] & valid, 1.0, 0.0)
        before = jnp.dot(onehot.astype(MXU_DTYPE), tri_ref[...], preferred_element_type=F32)
        rank = jnp.sum(onehot * (base + before), axis=0, keepdims=True)
        base = base + jnp.sum(onehot, axis=1, keepdims=True)
        e_ref[k:k + 1, :] = picks[k]
        w_ref[k:k + 1, :] = raw[k] / denom
        r_ref[k:k + 1, :] = rank.astype(I32)
    base_ref[...] = base
    cnt_ref[...] = jnp.broadcast_to(base, cnt_ref.shape)


def _route(logits, b_router, n_tok, n_rows):
    tm = TM_EXPERT
    tt = ROUTE_TILE
    n_steps = -(-n_tok // tt)
    n_pad = n_steps * tt
    pick_spec = pl.BlockSpec((TOP_K, tt), lambda t: (0, t))
    e_idx, wts, rank, cnt = pl.pallas_call(
        functools.partial(_route_body, n_tok=n_tok),
        grid=(n_steps,),
        in_specs=[pl.BlockSpec((tt, LANES), lambda t: (t, 0)), pl.BlockSpec((N_EXPERTS, 1), lambda t: (0, 0))],
        out_specs=[pick_spec, pick_spec, pick_spec, pl.BlockSpec((N_EXPERTS, LANES), lambda t: (0, 0))],
        out_shape=[jax.ShapeDtypeStruct((TOP_K, n_pad), I32), jax.ShapeDtypeStruct((TOP_K, n_pad), F32),
                   jax.ShapeDtypeStruct((TOP_K, n_pad), I32), jax.ShapeDtypeStruct((N_EXPERTS, LANES), F32)],
        scratch_shapes=[pltpu.VMEM((tt, tt), MXU_DTYPE), pltpu.VMEM((N_EXPERTS, 1), F32)],
        compiler_params=_params("arbitrary"),
        name="route",
    )(logits, b_router.astype(F32).reshape(N_EXPERTS, 1))
    counts = cnt[:, 0].astype(I32)
    padded = (counts + tm - 1) // tm * tm
    pend = jnp.cumsum(padded)
    pstart = pend - padded
    expert_ids = jnp.arange(N_EXPERTS, dtype=I32)[:, None, None]
    dest = rank + jnp.sum(jnp.where(e_idx[None] == expert_ids, pstart[:, None, None], 0), axis=0)
    gaps = jnp.stack([pstart + counts, pend], axis=1).reshape(-1).astype(I32)
    n_tiles = n_rows // tm
    n_used = (pend[-1] // tm).astype(I32)
    tile_ids = jnp.arange(n_tiles, dtype=I32)
    tile_start = jnp.minimum(tile_ids, n_used - 1) * tm
    tile_e = jnp.minimum(jnp.sum(pend[None, :] <= tile_start[:, None], axis=1), N_EXPERTS - 1).astype(I32)
    tile_valid = jnp.clip((pstart + counts)[tile_e] - tile_ids * tm, 0, tm).astype(I32)
    return gaps, tile_e, tile_valid, n_used.reshape(1), dest.reshape(-1), wts[:, :n_tok].T


def _top_k_indices(x, k):
    iota = lax.broadcasted_iota(I32, x.shape, x.ndim - 1)
    picks = []
    for _ in range(k):
        top = jnp.max(x, axis=-1, keepdims=True)
        idx = jnp.min(jnp.where(x == top, iota, x.shape[-1]), axis=-1, keepdims=True)
        picks.append(idx)
        x = jnp.where(iota == idx, -jnp.inf, x)
    return jnp.concatenate(picks, axis=-1)


def _rotary_tables(pos):
    inv_freq = ROPE_THETA ** (-jnp.arange(ROT_HALF, dtype=F32) / ROT_HALF)
    ang = pos.astype(F32)[:, None] * inv_freq[None, :]
    ones = jnp.ones((pos.shape[0], LANES - ROT_DIM), F32)
    cos_t = jnp.concatenate([jnp.cos(ang), jnp.cos(ang), ones], axis=1)
    sin_t = jnp.concatenate([jnp.sin(ang), jnp.sin(ang), 0.0 * ones], axis=1)
    return cos_t, sin_t


def kernel(x_prompt, x_sample, cache_k, cache_v, state_conv, page_table, c_prompt, c_sample,
           w_ada, b_ada, ln_g, ln_b, w_qkv, w_o, conv_w_in, conv_w_dw, conv_ln_g, conv_ln_b,
           conv_w_out, w_router, b_router, w_gate, w_up, w_down):
    batch, seq, d = x_prompt.shape
    db, dec_seq, _ = x_sample.shape
    assert dec_seq == 1 and seq % TM == 0 and TM % MOBA_BLOCK == 0 and d == N_HEADS * HEAD_DIM
    depth = w_ada.shape[0]
    n_pages = page_table.shape[1]
    past_len = n_pages * PAGE_SIZE
    assert n_pages % KMEAN_PAGES == 0 and n_pages % PAGES_PER_BLOCK == 0
    n_full = n_pages // PAGES_PER_BLOCK
    topk_s = min(MOBA_TOPK, n_full)
    alpha = (2 * depth) ** 0.25
    n_p = batch * seq
    n_all = n_p + db
    assert n_p % db == 0 and n_p % ROUTE_TILE == 0 and ROUTE_TILE % TM == 0
    n_buf = -(-n_all // ROUTE_TILE) * ROUTE_TILE
    tps = seq // TM

    c_rows = -(-(batch + db) // 8) * 8
    c_all = jnp.concatenate([c_prompt, c_sample, jnp.zeros((c_rows - batch - db, d), F32)], axis=0)
    mod = _ada(c_all, w_ada, b_ada)

    wr_pad = jnp.pad(w_router, ((0, 0), (0, LANES - N_EXPERTS))).astype(MXU_DTYPE)
    n_assign = n_all * TOP_K
    n_rows = -(-(n_assign + N_EXPERTS * (TM_EXPERT - 1)) // TM_EXPERT) * TM_EXPERT

    cos_p, sin_p = _rotary_tables(jnp.arange(seq))
    cos_s, sin_s = _rotary_tables(past_len + jnp.zeros((db,), I32))

    xp = x_prompt.reshape(n_p, d)
    xs = x_sample.reshape(db, d)
    kp_pages, vp_pages, ks_rows, vs_rows, conv_p, conv_s = [], [], [], [], [], []
    for i in range(depth):
        mp = [m[:, None, :] for m in jnp.split(mod[i, :batch], 6, axis=-1)]
        ms = [m[None] for m in jnp.split(mod[i, batch:batch + db], 6, axis=-1)]
        if i % 2 == 0:
            ia = i // 2
            q, kp, vp, kx, v16, km = _qkv_prompt(xp, mp[0], mp[1], w_qkv[ia], cos_p, sin_p, batch, seq)
            km = km.transpose(0, 2, 1, 3, 4).reshape(batch, N_KV_HEADS, seq // MOBA_BLOCK, HEAD_DIM)
            attn_p = _moba_prompt(q, kx, v16, km, batch, seq)
            kp_pages.append(kp)
            vp_pages.append(vp)

            qkv_s = _qkv_sample(xs, ms[0][0], ms[1][0], w_qkv[ia], cos_s, sin_s)
            nq = N_HEADS * HEAD_DIM
            nk = N_KV_HEADS * HEAD_DIM
            q_s = qkv_s[:, :nq].reshape(db, N_HEADS, HEAD_DIM)
            k_s = qkv_s[:, nq:nq + nk].reshape(db, N_KV_HEADS, HEAD_DIM)
            v_s = qkv_s[:, nq + nk:].reshape(db, N_KV_HEADS, HEAD_DIM)
            ks_rows.append(k_s[:, :, None, :])
            vs_rows.append(v_s[:, :, None, :])
            if topk_s > 0:
                kmean_s = _kmean_sample(page_table, cache_k, ia)
                kvh = jnp.arange(N_HEADS) // GQA_GROUP
                gate_s = jnp.einsum("bhd,bnhd->bhn", q_s, kmean_s[:, :, kvh])
                sel = _top_k_indices(gate_s, topk_s)
                sel_pages = (sel[..., None] * PAGES_PER_BLOCK + jnp.arange(PAGES_PER_BLOCK)).reshape(db, N_HEADS, -1)
                phys = jnp.take_along_axis(page_table[:, None, :], sel_pages, axis=2).astype(I32)
                attn_s = _attn_sample(phys.reshape(-1), q_s.reshape(db * N_HEADS, 1, HEAD_DIM),
                                      k_s.reshape(db * N_KV_HEADS, 1, HEAD_DIM),
                                      v_s.reshape(db * N_KV_HEADS, 1, HEAD_DIM), cache_k, cache_v, ia)
                attn_s = attn_s.reshape(db, nq)
            else:
                attn_s = jnp.repeat(v_s, GQA_GROUP, axis=1).reshape(db, nq)
            mixer_p, mixer_s, w_mix = attn_p, attn_s, w_o[ia]
            mode_p, mode_s = "attn", "attn"
        else:
            ic = i // 2
            u_p = _glu(xp, mp[0], mp[1], conv_w_in[ic], TM, tps)
            u_s = _glu(xs, ms[0], ms[1], conv_w_in[ic], db, 1)
            cg, cb = conv_ln_g[ic][None], conv_ln_b[ic][None]
            mixer_p = (u_p, conv_w_dw[ic], cg, cb)
            mixer_s = (u_s, state_conv[ic].transpose(1, 0, 2), conv_w_dw[ic], cg, cb)
            w_mix = conv_w_out[ic]
            mode_p, mode_s = "conv", "conv_step"
            conv_p.append(u_p.reshape(batch, seq, d)[:, seq - (CONV_WIDTH - 1):])
            conv_s.append(jnp.concatenate([state_conv[ic][:, 1:], u_s[:, None, :]], axis=1))

        x1p, h2_all, lg_all = _post(mode_p, mixer_p, w_mix, xp, mp[2], ln_g[i, 0][None], ln_b[i, 0][None],
                                    mp[3], mp[4], wr_pad, tm=TM, rows_per_mod=tps, alpha=alpha,
                                    seq=seq, out_rows=n_buf)
        x1s, h2_all, lg_all = _post(mode_s, mixer_s, w_mix, xs, ms[2], ln_g[i, 0][None], ln_b[i, 0][None],
                                    ms[3], ms[4], wr_pad, tm=db, rows_per_mod=1, alpha=alpha,
                                    into=(h2_all, lg_all, n_p // db))
        gaps, tile_e, tile_valid, n_used, dest, wts = _route(lg_all, b_router, n_all, n_rows)
        ys = _experts(dest, gaps, tile_e, tile_valid, n_used, h2_all, w_gate, w_up, w_down, i, n_all, n_rows)
        xp = _combine(dest, ys, x1p, wts[:n_p], mp[5], ln_g[i, 1][None], ln_b[i, 1][None],
                      tm=TM_COMBINE, rows_per_mod=seq // TM_COMBINE, alpha=alpha, tok0=0)
        xs = _combine(dest, ys, x1s, wts[n_p:], ms[5], ln_g[i, 1][None], ln_b[i, 1][None],
                      tm=db, rows_per_mod=1, alpha=alpha, tok0=n_p)

    k_prompt = jnp.concatenate(kp_pages, axis=2)
    v_prompt = jnp.concatenate(vp_pages, axis=2)
    return (xp.reshape(batch, seq, d), xs.reshape(db, 1, d), k_prompt, v_prompt, jnp.stack(conv_p, axis=0),
            jnp.stack(ks_rows, axis=1), jnp.stack(vs_rows, axis=1), jnp.stack(conv_s, axis=0))
```

```python
import functools
import math

import jax
import jax.numpy as jnp
from jax import lax
from jax.experimental import pallas as pl
from jax.experimental.pallas import tpu as pltpu

F32 = jnp.float32
I32 = jnp.int32
MXU_DTYPE = jnp.bfloat16

N_HEADS = 8
N_KV_HEADS = 2
GQA_GROUP = N_HEADS // N_KV_HEADS
HEAD_DIM = 128
ROT_DIM = HEAD_DIM // 4
ROT_HALF = ROT_DIM // 2
ROPE_THETA = 500000.0
ATTN_SCALE = HEAD_DIM ** -0.5
MOBA_BLOCK = 256
MOBA_TOPK = 3
MOBA_CHUNK = 128
EXP2_SCALE = ATTN_SCALE * math.log2(math.e)
PAGE_SIZE = 128
PAGES_PER_BLOCK = MOBA_BLOCK // PAGE_SIZE
CONV_WIDTH = 31
CONV_HALO = 32
N_EXPERTS = 32
N_GROUPS = 4
EXPERTS_PER_GROUP = N_EXPERTS // N_GROUPS
TOP_K = 2
LN_EPS = 1e-5
MASK_VALUE = -1e30
LANES = 128
SUBLANES = 8
VMEM_LIMIT = 56 * 1024 * 1024

TM = 512
TM_EXPERT = 256
TM_COMBINE = 256
KMEAN_PAGES = 32
ROUTE_TILE = 512
GATHER_UNROLL = 8


def _params(*sem, row_gather=False):
    return pltpu.CompilerParams(dimension_semantics=sem, vmem_limit_bytes=VMEM_LIMIT,
                                disable_bounds_checks=row_gather)


def _dot(a, b):
    return jnp.dot(a.astype(MXU_DTYPE), b.astype(MXU_DTYPE), preferred_element_type=F32)


def _dot_nt(a, b):
    return lax.dot_general(a.astype(MXU_DTYPE), b.astype(MXU_DTYPE), (((1,), (1,)), ((), ())),
                           preferred_element_type=F32)


def _mxu_round(x):
    return x.astype(MXU_DTYPE).astype(F32)


def _store_row_tiles(ref, val):
    rows, d = val.shape
    per_row = d // LANES
    for s in range(per_row):
        ref[pl.ds(s, rows, stride=per_row), :] = val[:, s * LANES:(s + 1) * LANES]


def _load_row_tiles(ref, per_row):
    rows = ref.shape[0] // per_row
    return jnp.concatenate([ref[pl.ds(s, rows, stride=per_row), :] for s in range(per_row)], axis=1)


def _layer_norm(z, g, b):
    mu = jnp.mean(z, axis=-1, keepdims=True)
    zc = z - mu
    var = jnp.mean(zc * zc, axis=-1, keepdims=True)
    return zc * lax.rsqrt(var + LN_EPS) * g + b


def _silu(x):
    return x * jax.nn.sigmoid(x)


def _rotary(xc, cos, sin, lane):
    x_up = pltpu.roll(xc, LANES - ROT_HALF, axis=1)
    x_dn = pltpu.roll(xc, ROT_HALF, axis=1)
    first = xc * cos - x_up * sin
    second = xc * cos + x_dn * sin
    return jnp.where(lane < ROT_HALF, first, jnp.where(lane < ROT_DIM, second, xc))


def _ada_body(c_ref, w_ref, b_ref, o_ref):
    o_ref[0] = _dot(_silu(c_ref[...]), w_ref[0]) + b_ref[0]


def _ada(c_all, w_ada, b_ada):
    depth, d, n6 = w_ada.shape
    rows = c_all.shape[0]
    tn = 1536
    return pl.pallas_call(
        _ada_body,
        grid=(depth, n6 // tn),
        in_specs=[pl.BlockSpec((rows, d), lambda i, j: (0, 0)),
                  pl.BlockSpec((1, d, tn), lambda i, j: (i, 0, j)),
                  pl.BlockSpec((1, 1, tn), lambda i, j: (i, 0, j))],
        out_specs=pl.BlockSpec((1, rows, tn), lambda i, j: (i, 0, j)),
        out_shape=jax.ShapeDtypeStruct((depth, rows, n6), F32),
        compiler_params=_params("arbitrary", "arbitrary"),
        name="ada",
    )(c_all, w_ada, b_ada.reshape(depth, 1, n6))


def _qkv_prompt_body(x_ref, sh_ref, sc_ref, w_ref, cos_ref, sin_ref,
                     q_ref, kp_ref, vp_ref, kx_ref, v16_ref, km_ref, w16, *, tiles_per_seq):
    t = pl.program_id(0)
    tm = x_ref.shape[0]

    @pl.when(t == 0)
    def _():
        w16[...] = w_ref[...].astype(MXU_DTYPE)

    h = x_ref[...] * (1 + sc_ref[0]) + sh_ref[0]
    qkv = jnp.dot(h.astype(MXU_DTYPE), w16[...], preferred_element_type=F32)
    cos = cos_ref[...]
    sin = sin_ref[...]
    lane = lax.broadcasted_iota(I32, (tm, LANES), 1)
    nq = N_HEADS * HEAD_DIM
    nk = N_KV_HEADS * HEAD_DIM
    for hh in range(N_HEADS):
        sl = slice(hh * HEAD_DIM, (hh + 1) * HEAD_DIM)
        q_ref[:, sl] = _rotary(qkv[:, sl], cos, sin, lane).astype(q_ref.dtype)
    row = lax.broadcasted_iota(I32, (tm, LANES), 0)
    blk = ((t % tiles_per_seq) * tm + row) // MOBA_BLOCK
    onehot = jnp.where(lane == blk, 1.0, 0.0).astype(kx_ref.dtype)
    for n in range(N_KV_HEADS):
        kc = _rotary(qkv[:, nq + n * HEAD_DIM:nq + (n + 1) * HEAD_DIM], cos, sin, lane)
        vc = qkv[:, nq + nk + n * HEAD_DIM:nq + nk + (n + 1) * HEAD_DIM]
        kp_ref[0, :, 0, n] = kc.reshape(tm // PAGE_SIZE, PAGE_SIZE, HEAD_DIM)
        vp_ref[0, :, 0, n] = vc.reshape(tm // PAGE_SIZE, PAGE_SIZE, HEAD_DIM)
        kx_ref[0, n, :, 0:HEAD_DIM] = kc.astype(kx_ref.dtype)
        kx_ref[0, n, :, HEAD_DIM:2 * HEAD_DIM] = onehot
        v16_ref[0, n] = vc.astype(v16_ref.dtype)
        km_ref[0, 0, n] = jnp.sum(kc.reshape(tm // MOBA_BLOCK, MOBA_BLOCK, HEAD_DIM), axis=1) * (1.0 / MOBA_BLOCK)


def _qkv_prompt(x, shift, scale, w_qkv, cos_t, sin_t, batch, seq):
    n, d = x.shape
    tm = TM
    tps = seq // tm
    width = w_qkv.shape[1]
    npg = seq // PAGE_SIZE
    mod_spec = pl.BlockSpec((1, 1, d), lambda t: (t // tps, 0, 0))
    rot_spec = pl.BlockSpec((tm, LANES), lambda t: (t % tps, 0))
    page_spec = pl.BlockSpec((1, tm // PAGE_SIZE, 1, N_KV_HEADS, PAGE_SIZE, HEAD_DIM),
                             lambda t: (t // tps, t % tps, 0, 0, 0, 0))
    page_shape = jax.ShapeDtypeStruct((batch, npg, 1, N_KV_HEADS, PAGE_SIZE, HEAD_DIM), F32)
    return pl.pallas_call(
        functools.partial(_qkv_prompt_body, tiles_per_seq=tps),
        grid=(n // tm,),
        in_specs=[pl.BlockSpec((tm, d), lambda t: (t, 0)), mod_spec, mod_spec,
                  pl.BlockSpec((d, width), lambda t: (0, 0)), rot_spec, rot_spec],
        out_specs=[pl.BlockSpec((tm, N_HEADS * HEAD_DIM), lambda t: (t, 0)),
                   page_spec, page_spec,
                   pl.BlockSpec((1, N_KV_HEADS, tm, 2 * HEAD_DIM), lambda t: (t // tps, 0, t % tps, 0)),
                   pl.BlockSpec((1, N_KV_HEADS, tm, HEAD_DIM), lambda t: (t // tps, 0, t % tps, 0)),
                   pl.BlockSpec((1, 1, N_KV_HEADS, tm // MOBA_BLOCK, HEAD_DIM), lambda t: (t // tps, t % tps, 0, 0, 0))],
        out_shape=[jax.ShapeDtypeStruct((n, N_HEADS * HEAD_DIM), MXU_DTYPE),
                   page_shape, page_shape,
                   jax.ShapeDtypeStruct((batch, N_KV_HEADS, seq, 2 * HEAD_DIM), MXU_DTYPE),
                   jax.ShapeDtypeStruct((batch, N_KV_HEADS, seq, HEAD_DIM), MXU_DTYPE),
                   jax.ShapeDtypeStruct((batch, tps, N_KV_HEADS, tm // MOBA_BLOCK, HEAD_DIM), F32)],
        scratch_shapes=[pltpu.VMEM((d, width), MXU_DTYPE)],
        compiler_params=_params("arbitrary"),
        name="qkv_prompt",
    )(x, shift, scale, w_qkv, cos_t, sin_t)


def _qkv_sample_body(x_ref, sh_ref, sc_ref, w_ref, cos_ref, sin_ref, o_ref):
    rows = x_ref.shape[0]
    h = x_ref[...] * (1 + sc_ref[...]) + sh_ref[...]
    qkv = _dot(h, w_ref[...])
    lane = lax.broadcasted_iota(I32, (rows, LANES), 1)
    n_rot = N_HEADS + N_KV_HEADS
    for c in range(n_rot):
        sl = slice(c * HEAD_DIM, (c + 1) * HEAD_DIM)
        o_ref[:, sl] = _rotary(qkv[:, sl], cos_ref[...], sin_ref[...], lane)
    o_ref[:, n_rot * HEAD_DIM:] = qkv[:, n_rot * HEAD_DIM:]


def _qkv_sample(x, shift, scale, w_qkv, cos_t, sin_t):
    rows, d = x.shape
    width = w_qkv.shape[1]
    full = lambda shape: pl.BlockSpec(shape, lambda: tuple(0 for _ in shape))
    return pl.pallas_call(
        _qkv_sample_body,
        in_specs=[full((rows, d)), full((rows, d)), full((rows, d)), full((d, width)),
                  full((rows, LANES)), full((rows, LANES))],
        out_specs=full((rows, width)),
        out_shape=jax.ShapeDtypeStruct((rows, width), F32),
        compiler_params=pltpu.CompilerParams(vmem_limit_bytes=VMEM_LIMIT),
        name="qkv_sample",
    )(x, shift, scale, w_qkv, cos_t, sin_t)


def _moba_prompt_body(q_ref, kx_ref, v_ref, km_ref, o_ref, qx_ref, *state):
    i = pl.program_id(2)
    rows = GQA_GROUP * MOBA_BLOCK
    n_chunks = rows // MOBA_CHUNK
    s_refs, p_refs, a_refs, acc_ref = state[0:2], state[2:4], state[4:6], state[6]
    m_refs, l_refs = state[7:7 + n_chunks], state[7 + n_chunks:]
    n_blk = km_ref.shape[2]
    for h in range(GQA_GROUP):
        qx_ref[h * MOBA_BLOCK:(h + 1) * MOBA_BLOCK, 0:HEAD_DIM] = q_ref[:, h * HEAD_DIM:(h + 1) * HEAD_DIM]

    gate = _dot_nt(km_ref[0, 0], qx_ref[:, 0:HEAD_DIM])
    blk = lax.broadcasted_iota(I32, (n_blk, rows), 0)
    valid = blk < i
    cand = jnp.where(valid, gate, -jnp.inf)
    sel = blk == i
    for _ in range(MOBA_TOPK):
        top = jnp.max(cand, axis=0, keepdims=True)
        idx = jnp.min(jnp.where(cand == top, blk, n_blk), axis=0, keepdims=True)
        pick = blk == idx
        sel = sel | (pick & valid)
        cand = jnp.where(pick, -jnp.inf, cand)
    bias = jnp.where(sel, 0.0, MASK_VALUE)
    if n_blk < LANES:
        bias = jnp.concatenate([bias, jnp.zeros((LANES - n_blk, rows), F32)], axis=0)
    qx_ref[:, HEAD_DIM:2 * HEAD_DIM] = bias.T.astype(qx_ref.dtype)

    half = MOBA_BLOCK // 2

    def issue_scores(j, slot):
        start = pl.multiple_of(j * MOBA_BLOCK, MOBA_BLOCK)
        s_refs[slot][...] = _dot_nt(qx_ref[...], kx_ref[0, 0, pl.ds(start, MOBA_BLOCK), :])

    def softmax(slot, own):
        for c in range(n_chunks):
            rs = slice(c * MOBA_CHUNK, (c + 1) * MOBA_CHUNK)
            s = s_refs[slot][rs, :]
            if own:
                qpos = (c * MOBA_CHUNK) % MOBA_BLOCK + lax.broadcasted_iota(I32, (MOBA_CHUNK, MOBA_BLOCK), 0)
                kpos = lax.broadcasted_iota(I32, (MOBA_CHUNK, MOBA_BLOCK), 1)
                s = jnp.where(kpos <= qpos, s, MASK_VALUE)
            sa, sb = s[:, :half], s[:, half:]
            top = jnp.broadcast_to(jnp.max(jnp.maximum(sa, sb), axis=1, keepdims=True), (MOBA_CHUNK, half))
            if own:
                m_new = top
            else:
                m_old = m_refs[c][...]
                m_new = jnp.maximum(m_old, top)
                alpha = jnp.exp2((m_old - m_new) * EXP2_SCALE)
                a_refs[slot][rs, :] = alpha
            pa = jnp.exp2((sa - m_new) * EXP2_SCALE)
            pb = jnp.exp2((sb - m_new) * EXP2_SCALE)
            p_refs[slot][rs, :half] = pa.astype(MXU_DTYPE)
            p_refs[slot][rs, half:] = pb.astype(MXU_DTYPE)
            if own:
                l_refs[c][...] = pa + pb
            else:
                l_refs[c][...] = alpha * l_refs[c][...] + (pa + pb)
            m_refs[c][...] = m_new

    def accumulate(j, slot, own):
        start = pl.multiple_of(j * MOBA_BLOCK, MOBA_BLOCK)
        pv = jnp.dot(p_refs[slot][...], v_ref[0, 0, pl.ds(start, MOBA_BLOCK), :], preferred_element_type=F32)
        if own:
            acc_ref[...] = pv
        else:
            acc_ref[...] = a_refs[slot][...] * acc_ref[...] + pv

    issue_scores(i, 1)
    issue_scores(0, 0)
    softmax(1, True)
    accumulate(i, 1, True)

    def pair(t, carry):
        j = 2 * t
        issue_scores(j + 1, 1)
        softmax(0, False)
        accumulate(j, 0, False)
        issue_scores(jnp.minimum(j + 2, i - 1), 0)
        softmax(1, False)
        accumulate(j + 1, 1, False)
        return carry

    lax.fori_loop(0, i // 2, pair, 0)

    @pl.when(i % 2 == 1)
    def _():
        softmax(0, False)
        accumulate(i - 1, 0, False)

    per_head = MOBA_BLOCK // MOBA_CHUNK
    for c in range(n_chunks):
        rs = slice(c * MOBA_CHUNK, (c + 1) * MOBA_CHUNK)
        out = acc_ref[rs, :] / jnp.sum(l_refs[c][...], axis=1, keepdims=True)
        h, part = divmod(c, per_head)
        o_ref[part * MOBA_CHUNK:(part + 1) * MOBA_CHUNK, h * HEAD_DIM:(h + 1) * HEAD_DIM] = out.astype(o_ref.dtype)


def _moba_prompt(q, kx, v16, k_mean, batch, seq):
    n = q.shape[0]
    nq = seq // MOBA_BLOCK
    rows = GQA_GROUP * MOBA_BLOCK
    gw = GQA_GROUP * HEAD_DIM
    return pl.pallas_call(
        _moba_prompt_body,
        grid=(batch, N_KV_HEADS, nq),
        in_specs=[pl.BlockSpec((MOBA_BLOCK, gw), lambda b, g, i: (b * nq + i, g)),
                  pl.BlockSpec((1, 1, seq, 2 * HEAD_DIM), lambda b, g, i: (b, g, 0, 0)),
                  pl.BlockSpec((1, 1, seq, HEAD_DIM), lambda b, g, i: (b, g, 0, 0)),
                  pl.BlockSpec((1, 1, nq, HEAD_DIM), lambda b, g, i: (b, g, 0, 0))],
        out_specs=pl.BlockSpec((MOBA_BLOCK, gw), lambda b, g, i: (b * nq + i, g)),
        out_shape=jax.ShapeDtypeStruct((n, N_HEADS * HEAD_DIM), MXU_DTYPE),
        scratch_shapes=([pltpu.VMEM((rows, 2 * HEAD_DIM), MXU_DTYPE)]
                        + [pltpu.VMEM((rows, MOBA_BLOCK), F32)] * 2
                        + [pltpu.VMEM((rows, MOBA_BLOCK), MXU_DTYPE)] * 2
                        + [pltpu.VMEM((rows, MOBA_BLOCK // 2), F32)] * 2
                        + [pltpu.VMEM((rows, HEAD_DIM), F32)]
                        + [pltpu.VMEM((MOBA_CHUNK, MOBA_BLOCK // 2), F32)] * (2 * (rows // MOBA_CHUNK))),
        compiler_params=_params("arbitrary", "arbitrary", "arbitrary"),
        name="moba_prompt",
    )(q, kx, v16, k_mean)


def _kmean_copy(pt_ref, ck_hbm, buf, sem, layer, step, slot, p):
    phys = pt_ref[step * KMEAN_PAGES + p]
    return pltpu.make_async_copy(ck_hbm.at[phys, layer], buf.at[slot, p], sem.at[slot])


def _kmean_sample_body(pt_ref, ck_hbm, o_ref, buf, sem, *, layer):
    nc = pl.num_programs(1)
    step = pl.program_id(0) * nc + pl.program_id(1)
    total = pl.num_programs(0) * nc
    slot = step % 2

    def start(step_, slot_):
        for p in range(KMEAN_PAGES):
            _kmean_copy(pt_ref, ck_hbm, buf, sem, layer, step_, slot_, p).start()

    @pl.when(step == 0)
    def _():
        start(step, slot)

    @pl.when(step + 1 < total)
    def _():
        start(step + 1, 1 - slot)

    for p in range(KMEAN_PAGES):
        _kmean_copy(pt_ref, ck_hbm, buf, sem, layer, step, slot, p).wait()
    for jb in range(KMEAN_PAGES // PAGES_PER_BLOCK):
        for n in range(N_KV_HEADS):
            acc = jnp.zeros((1, HEAD_DIM), F32)
            for pp in range(PAGES_PER_BLOCK):
                acc = acc + jnp.sum(buf[slot, jb * PAGES_PER_BLOCK + pp, n], axis=0, keepdims=True)
            o_ref[0, jb, pl.ds(n, 1), :] = acc * (1.0 / MOBA_BLOCK)


def _kmean_sample(page_table, cache_k, layer):
    db, n_pages = page_table.shape
    n_full = n_pages // PAGES_PER_BLOCK
    page_shape = cache_k.shape[2:]
    bps = KMEAN_PAGES // PAGES_PER_BLOCK
    return pl.pallas_call(
        functools.partial(_kmean_sample_body, layer=layer),
        grid_spec=pltpu.PrefetchScalarGridSpec(
            num_scalar_prefetch=1,
            grid=(db, n_pages // KMEAN_PAGES),
            in_specs=[pl.BlockSpec(memory_space=pl.ANY)],
            out_specs=pl.BlockSpec((1, bps, N_KV_HEADS, HEAD_DIM), lambda b, c, pt: (b, c, 0, 0)),
            scratch_shapes=[pltpu.VMEM((2, KMEAN_PAGES) + page_shape, F32),
                            pltpu.SemaphoreType.DMA((2,))]),
        out_shape=jax.ShapeDtypeStruct((db, n_full, N_KV_HEADS, HEAD_DIM), F32),
        compiler_params=_params("arbitrary", "arbitrary"),
        name="kmean_sample",
    )(page_table.reshape(-1), cache_k)


def _attn_sample_body(pages_ref, q_ref, kn_ref, vn_ref, *refs):
    n_sel = (len(refs) - 1) // 2
    k_refs, v_refs, o_ref = refs[:n_sel], refs[n_sel:2 * n_sel], refs[-1]
    q = _mxu_round(q_ref[0])
    s_new = jnp.sum(q * _mxu_round(kn_ref[0]), axis=1, keepdims=True) * ATTN_SCALE
    s_sel = [jnp.sum(_mxu_round(k_ref[0, 0, 0]) * q, axis=1, keepdims=True) * ATTN_SCALE
             for k_ref in k_refs]
    m = s_new
    for s in s_sel:
        m = jnp.maximum(m, jnp.max(s, axis=0, keepdims=True))
    p_new = jnp.exp(s_new - m)
    denom = p_new
    out = _mxu_round(p_new) * _mxu_round(vn_ref[0])
    for s, v_ref in zip(s_sel, v_refs):
        p = jnp.exp(s - m)
        denom = denom + jnp.sum(p, axis=0, keepdims=True)
        out = out + jnp.sum(_mxu_round(p) * _mxu_round(v_ref[0, 0, 0]), axis=0, keepdims=True)
    o_ref[0] = out / denom


def _attn_sample(phys, q, k_new, v_new, cache_k, cache_v, layer):
    db = q.shape[0] // N_HEADS
    n_sel = phys.shape[0] // (db * N_HEADS)
    vec = lambda f: pl.BlockSpec((1, 1, HEAD_DIM), f)

    def page_spec(p):
        return pl.BlockSpec((1, 1, 1, PAGE_SIZE, HEAD_DIM),
                            lambda b, h, pg: (pg[(b * N_HEADS + h) * n_sel + p], layer, h // GQA_GROUP, 0, 0))

    page_specs = [page_spec(p) for p in range(n_sel)]
    return pl.pallas_call(
        _attn_sample_body,
        grid_spec=pltpu.PrefetchScalarGridSpec(
            num_scalar_prefetch=1,
            grid=(db, N_HEADS),
            in_specs=[vec(lambda b, h, pg: (b * N_HEADS + h, 0, 0)),
                      vec(lambda b, h, pg: (b * N_KV_HEADS + h // GQA_GROUP, 0, 0)),
                      vec(lambda b, h, pg: (b * N_KV_HEADS + h // GQA_GROUP, 0, 0))] + page_specs + page_specs,
            out_specs=vec(lambda b, h, pg: (b * N_HEADS + h, 0, 0))),
        out_shape=jax.ShapeDtypeStruct((db * N_HEADS, 1, HEAD_DIM), F32),
        compiler_params=_params("arbitrary", "arbitrary"),
        name="attn_sample",
    )(phys, q, k_new, v_new, *([cache_k] * n_sel), *([cache_v] * n_sel))


def _glu_body(x_ref, sh_ref, sc_ref, w_ref, o_ref, w16):
    c = o_ref.shape[1]
    h = x_ref[...] * (1 + sc_ref[0]) + sh_ref[0]

    @pl.when(pl.program_id(0) == 0)
    def _():
        w16[...] = w_ref[...].astype(MXU_DTYPE)

    ag = jnp.dot(h.astype(MXU_DTYPE), w16[...], preferred_element_type=F32)
    o_ref[...] = ag[:, :c] * jax.nn.sigmoid(ag[:, c:])


def _glu(x, shift, scale, w_in, tm, rows_per_mod):
    n, d = x.shape
    c2 = w_in.shape[1]
    r = shift.shape[1]
    mod_spec = pl.BlockSpec((1, r, d), lambda t: (t // rows_per_mod, 0, 0))
    return pl.pallas_call(
        _glu_body,
        grid=(n // tm,),
        in_specs=[pl.BlockSpec((tm, d), lambda t: (t, 0)), mod_spec, mod_spec,
                  pl.BlockSpec((d, c2), lambda t: (0, 0))],
        out_specs=pl.BlockSpec((tm, c2 // 2), lambda t: (t, 0)),
        out_shape=jax.ShapeDtypeStruct((n, c2 // 2), F32),
        scratch_shapes=[pltpu.VMEM((d, c2), MXU_DTYPE)],
        compiler_params=_params("arbitrary"),
        name="glu",
    )(x, shift, scale, w_in)


def _post_body(*refs, n_tiles, **kw):
    n_mixer = 1 if kw["mode"] == "attn" else 5
    h2_ref, lg_ref = refs[n_mixer + 9], refs[n_mixer + 10]
    t = pl.program_id(0)

    @pl.when(t < n_tiles)
    def _():
        _post_tile(*refs, **kw)

    @pl.when(t >= n_tiles)
    def _():
        h2_ref[...] = jnp.zeros(h2_ref.shape, F32)
        lg_ref[...] = jnp.zeros(lg_ref.shape, F32)


def _post_tile(*refs, mode, alpha, tiles_per_seq):
    if mode == "attn":
        a_ref, rest = refs[0], refs[1:]
    elif mode == "conv":
        u_ref, prev_ref, wdw_ref, cg_ref, cb_ref = refs[:5]
        rest = refs[5:]
    else:
        u_ref, st_ref, wdw_ref, cg_ref, cb_ref = refs[:5]
        rest = refs[5:]
    (w_ref, x_ref, g1_ref, lng_ref, lnb_ref, sh2_ref, sc2_ref, wr_ref,
     x1_ref, h2_ref, lg_ref, w16) = rest[:12]
    t = pl.program_id(0)
    tm = x_ref.shape[0]

    if mode == "attn":
        a = a_ref[...]
    else:
        wdw = _mxu_round(wdw_ref[...])
        if mode == "conv":
            ext_ref, z_ref = rest[12:14]
            first = (t % tiles_per_seq) == 0
            ext_ref[0:CONV_HALO] = _mxu_round(jnp.where(first, 0.0, prev_ref[...]))
            ext_ref[CONV_HALO:CONV_HALO + tm] = _mxu_round(u_ref[...])
            ext_ref[CONV_HALO + tm:] = jnp.zeros((SUBLANES, x_ref.shape[1]), F32)
            off = CONV_HALO - (CONV_WIDTH - 1)
            y = None
            for b in range(SUBLANES):
                z = None
                for a in range(-(-(off + CONV_WIDTH) // SUBLANES)):
                    w = a * SUBLANES + b - off
                    if 0 <= w < CONV_WIDTH:
                        term = ext_ref[pl.ds(a * SUBLANES, tm + SUBLANES), :] * wdw[w:w + 1, :]
                        z = term if z is None else z + term
                if b == 0:
                    y = z[0:tm]
                else:
                    z_ref[...] = z
                    y = y + z_ref[pl.ds(b, tm), :]
        else:
            y = _mxu_round(u_ref[...]) * wdw[CONV_WIDTH - 1:CONV_WIDTH, :]
            for w in range(CONV_WIDTH - 1):
                y = y + _mxu_round(st_ref[w]) * wdw[w:w + 1, :]
        a = _silu(_layer_norm(y, cg_ref[...], cb_ref[...]))

    @pl.when(t == 0)
    def _():
        w16[...] = w_ref[...].astype(MXU_DTYPE)

    f = jnp.dot(a.astype(MXU_DTYPE), w16[...], preferred_element_type=F32)
    x1 = _layer_norm(alpha * x_ref[...] + g1_ref[0] * f, lng_ref[...], lnb_ref[...])
    h2 = x1 * (1 + sc2_ref[0]) + sh2_ref[0]
    x1_ref[...] = x1
    _store_row_tiles(h2_ref, h2)
    lg_ref[...] = _dot(h2, wr_ref[...])


def _post(mode, mixer_in, w, x, gate1, ln_g, ln_b, shift2, scale2, w_router, *, tm, rows_per_mod,
          alpha, seq=None, out_rows=None, into=None):
    n, d = x.shape
    out_rows = n if out_rows is None else out_rows
    r = gate1.shape[1]
    tps = None if seq is None else seq // tm
    n_tiles = n // tm
    n_steps = -(-out_rows // tm)
    last = n_tiles - 1
    row = lambda t: (jnp.minimum(t, last), 0)
    const = lambda t: (0, 0)
    mod_spec = pl.BlockSpec((1, r, d), lambda t: (jnp.minimum(t, last) // rows_per_mod, 0, 0))
    vec_spec = pl.BlockSpec((1, d), const)
    if mode == "attn":
        mixer_specs = [pl.BlockSpec((tm, d), row)]
    elif mode == "conv":
        u, w_dw, cg, cb = mixer_in
        per = tm // CONV_HALO
        mixer_in = (u, u, w_dw, cg, cb)
        mixer_specs = [pl.BlockSpec((tm, d), row),
                       pl.BlockSpec((CONV_HALO, d), lambda t: (jnp.maximum(jnp.minimum(t, last) * per - 1, 0), 0)),
                       pl.BlockSpec(w_dw.shape, const), vec_spec, vec_spec]
    else:
        u, state, w_dw, cg, cb = mixer_in
        mixer_specs = [pl.BlockSpec((tm, d), row), pl.BlockSpec(state.shape, lambda t: (0, 0, 0)),
                       pl.BlockSpec(w_dw.shape, const), vec_spec, vec_spec]
    if mode == "attn":
        mixer_in = (mixer_in,)
    in_specs = mixer_specs + [pl.BlockSpec(w.shape, const), pl.BlockSpec((tm, d), row), mod_spec, vec_spec, vec_spec,
                              mod_spec, mod_spec, pl.BlockSpec(w_router.shape, const)]
    args = list(mixer_in) + [w, x, gate1, ln_g, ln_b, shift2, scale2, w_router]
    out_shape = [jax.ShapeDtypeStruct((n, d), F32)]
    out_specs = [pl.BlockSpec((tm, d), row)]
    aliases = {}
    if into is None:
        out_shape += [jax.ShapeDtypeStruct((out_rows * (d // LANES), LANES), F32),
                      jax.ShapeDtypeStruct((out_rows, LANES), F32)]
        out_specs += [pl.BlockSpec((tm * (d // LANES), LANES), lambda t: (t, 0)),
                      pl.BlockSpec((tm, LANES), lambda t: (t, 0))]
    else:
        h2_all, lg_all, row_block = into
        out_shape += [jax.ShapeDtypeStruct(h2_all.shape, F32), jax.ShapeDtypeStruct(lg_all.shape, F32)]
        out_specs += [pl.BlockSpec((tm * (d // LANES), LANES), lambda t: (row_block + t, 0)),
                      pl.BlockSpec((tm, LANES), lambda t: (row_block + t, 0))]
        aliases = {len(args): 1, len(args) + 1: 2}
        in_specs += [pl.BlockSpec(memory_space=pl.ANY), pl.BlockSpec(memory_space=pl.ANY)]
        args += [h2_all, lg_all]
    scratch = [pltpu.VMEM(w.shape, MXU_DTYPE)]
    if mode == "conv":
        scratch += [pltpu.VMEM((CONV_HALO + tm + SUBLANES, d), F32), pltpu.VMEM((tm + SUBLANES, d), F32)]

    def body(*refs):
        if into is not None:
            n_in = len(args)
            refs = refs[:n_in - 2] + refs[n_in:]
        _post_body(*refs, n_tiles=n_tiles, mode=mode, alpha=alpha, tiles_per_seq=tps)

    return pl.pallas_call(
        body,
        grid=(n_steps,),
        in_specs=in_specs,
        out_specs=out_specs,
        out_shape=out_shape,
        scratch_shapes=scratch,
        input_output_aliases=aliases,
        compiler_params=_params("arbitrary"),
        name="post_" + mode,
    )(*args)


def _row_tile(ref, r, per_row):
    return ref.at[pl.ds(pl.multiple_of(r * per_row, per_row), per_row)]


def _gather_rows(idx_ref, base, src_hbm, dst, sem, n_groups, per_row):
    def issue(g, c):
        for k in range(GATHER_UNROLL):
            r = g * GATHER_UNROLL + k
            pltpu.make_async_copy(_row_tile(src_hbm, idx_ref[base + r], per_row), _row_tile(dst, r, per_row), sem).start()
        return c

    lax.fori_loop(0, n_groups, issue, 0)


def _wait_rows(src_hbm, dst, sem, n_groups, per_row):
    def drain(g, c):
        for k in range(GATHER_UNROLL):
            r = g * GATHER_UNROLL + k
            pltpu.make_async_copy(_row_tile(src_hbm, 0, per_row), _row_tile(dst, r, per_row), sem).wait()
        return c

    lax.fori_loop(0, n_groups, drain, 0)


def _expert_body(dest_ref, gap_ref, te_ref, tv_ref, nu_ref, h_hbm, wg_ref, wu_ref, wd_ref, o_ref,
                 tok_ref, xbuf, wg16, wu16, wd16, sem, *, n_tok):
    i = pl.program_id(0)
    per_row = wg16.shape[0] // LANES
    tm = o_ref.shape[0] // per_row
    n_used = nu_ref[0]
    slot = i % 2

    def groups(tile):
        return (tv_ref[tile] + GATHER_UNROLL - 1) // GATHER_UNROLL

    @pl.when(i == 0)
    def _():
        def clear(r, c):
            tok_ref[r] = 0
            return c

        def put(tok, c):
            for k in range(TOP_K):
                tok_ref[dest_ref[k * (dest_ref.shape[0] // TOP_K) + tok]] = tok
            return c

        for g in range(gap_ref.shape[0] // 2):
            lax.fori_loop(gap_ref[2 * g], gap_ref[2 * g + 1], clear, 0)
        lax.fori_loop(0, n_tok, put, 0, unroll=GATHER_UNROLL)
        xbuf[...] = jnp.zeros(xbuf.shape, F32)
        _gather_rows(tok_ref, 0, h_hbm, xbuf.at[0], sem.at[0], groups(0), per_row)

    @pl.when(i + 1 < n_used)
    def _():
        _gather_rows(tok_ref, (i + 1) * tm, h_hbm, xbuf.at[1 - slot], sem.at[1 - slot], groups(i + 1), per_row)

    @pl.when(i < n_used)
    def _():
        changed = jnp.logical_or(i == 0, te_ref[i] != te_ref[jnp.maximum(i - 1, 0)])

        @pl.when(changed)
        def _():
            wg16[...] = wg_ref[0].astype(MXU_DTYPE)
            wu16[...] = wu_ref[0].astype(MXU_DTYPE)
            wd16[...] = wd_ref[0].astype(MXU_DTYPE)

        _wait_rows(h_hbm, xbuf.at[slot], sem.at[slot], groups(i), per_row)
        x = _load_row_tiles(xbuf.at[slot], per_row).astype(MXU_DTYPE)
        g = jnp.dot(x, wg16[...], preferred_element_type=F32)
        u = jnp.dot(x, wu16[...], preferred_element_type=F32)
        _store_row_tiles(o_ref, jnp.dot((_silu(g) * u).astype(MXU_DTYPE), wd16[...], preferred_element_type=F32))

    @pl.when(i >= n_used)
    def _():
        o_ref[...] = jnp.zeros(o_ref.shape, F32)


def _experts(dest, gaps, tile_e, tile_valid, n_used, h_all, w_gate, w_up, w_down, layer, n_tok, n_rows):
    tm = TM_EXPERT
    d, de = w_gate.shape[-2:]
    per_row = d // LANES
    wspec_in = pl.BlockSpec((None, 1, d, de), lambda i, ds_, gp, te, tv, nu: (layer, te[i], 0, 0))
    wspec_out = pl.BlockSpec((None, 1, de, d), lambda i, ds_, gp, te, tv, nu: (layer, te[i], 0, 0))
    return pl.pallas_call(
        functools.partial(_expert_body, n_tok=n_tok),
        grid_spec=pltpu.PrefetchScalarGridSpec(
            num_scalar_prefetch=5,
            grid=(n_rows // tm,),
            in_specs=[pl.BlockSpec(memory_space=pl.ANY), wspec_in, wspec_in, wspec_out],
            out_specs=pl.BlockSpec((tm * per_row, LANES), lambda i, ds_, gp, te, tv, nu: (i, 0)),
            scratch_shapes=[pltpu.SMEM((n_rows,), I32), pltpu.VMEM((2, tm * per_row, LANES), F32),
                            pltpu.VMEM((d, de), MXU_DTYPE), pltpu.VMEM((d, de), MXU_DTYPE),
                            pltpu.VMEM((de, d), MXU_DTYPE), pltpu.SemaphoreType.DMA((2,))]),
        out_shape=jax.ShapeDtypeStruct((n_rows * per_row, LANES), F32),
        compiler_params=_params("arbitrary", row_gather=True),
        name="experts",
    )(dest, gaps, tile_e, tile_valid, n_used, h_all, w_gate, w_up, w_down)


def _combine_body(dest_ref, ys_hbm, x_ref, w_ref, g_ref, lng_ref, lnb_ref, o_ref, buf0, buf1, sem, *,
                  alpha, tok0, k_stride):
    t = pl.program_id(0)
    last = pl.num_programs(0) - 1
    tm, d = x_ref.shape
    per_row = d // LANES
    bufs = (buf0, buf1)
    groups = tm // GATHER_UNROLL

    @pl.when(t == 0)
    def _():
        for k in range(TOP_K):
            _gather_rows(dest_ref, k * k_stride + tok0, ys_hbm, buf0.at[k], sem.at[0, k], groups, per_row)

    def tile(cur, oth):
        for k in range(TOP_K):
            _wait_rows(ys_hbm, bufs[cur].at[k], sem.at[cur, k], groups, per_row)
        nxt = jnp.minimum(t + 1, last)
        for k in range(TOP_K):
            base = k * k_stride + tok0 + nxt * tm
            for r in range(tm):
                pltpu.make_async_copy(_row_tile(ys_hbm, dest_ref[base + r], per_row),
                                      bufs[oth].at[k, pl.ds(r * per_row, per_row)], sem.at[oth, k]).start()
        wts = _mxu_round(w_ref[...])
        f = _mxu_round(_load_row_tiles(bufs[cur].at[0], per_row)) * wts[:, 0:1]
        for k in range(1, TOP_K):
            f = f + _mxu_round(_load_row_tiles(bufs[cur].at[k], per_row)) * wts[:, k:k + 1]
        o_ref[...] = _layer_norm(alpha * x_ref[...] + g_ref[0] * f, lng_ref[...], lnb_ref[...])

        @pl.when(t == last)
        def _():
            for k in range(TOP_K):
                _wait_rows(ys_hbm, bufs[oth].at[k], sem.at[oth, k], groups, per_row)

    @pl.when(t % 2 == 0)
    def _():
        tile(0, 1)

    @pl.when(t % 2 == 1)
    def _():
        tile(1, 0)


def _combine(dest, ys, x, wts, gate2, ln_g, ln_b, *, tm, rows_per_mod, alpha, tok0):
    n, d = x.shape
    r = gate2.shape[1]
    vec_spec = pl.BlockSpec((1, d), lambda t, ds_: (0, 0))
    return pl.pallas_call(
        functools.partial(_combine_body, alpha=alpha, tok0=tok0, k_stride=dest.shape[0] // TOP_K),
        grid_spec=pltpu.PrefetchScalarGridSpec(
            num_scalar_prefetch=1,
            grid=(n // tm,),
            in_specs=[pl.BlockSpec(memory_space=pl.ANY),
                      pl.BlockSpec((tm, d), lambda t, ds_: (t, 0)),
                      pl.BlockSpec((tm, TOP_K), lambda t, ds_: (t, 0)),
                      pl.BlockSpec((1, r, d), lambda t, ds_: (t // rows_per_mod, 0, 0)),
                      vec_spec, vec_spec],
            out_specs=pl.BlockSpec((tm, d), lambda t, ds_: (t, 0)),
            scratch_shapes=[pltpu.VMEM((TOP_K, tm * (d // LANES), LANES), F32),
                            pltpu.VMEM((TOP_K, tm * (d // LANES), LANES), F32),
                            pltpu.SemaphoreType.DMA((2, TOP_K))]),
        out_shape=jax.ShapeDtypeStruct((n, d), F32),
        compiler_params=_params("arbitrary", row_gather=True),
        name="combine",
    )(dest, ys, x, wts, gate2, ln_g, ln_b)


def _top2_of_group(x, sub):
    m1 = jnp.max(x, axis=0, keepdims=True)
    i1 = jnp.min(jnp.where(x == m1, sub, EXPERTS_PER_GROUP), axis=0, keepdims=True)
    rest = jnp.where(sub == i1, -jnp.inf, x)
    m2 = jnp.max(rest, axis=0, keepdims=True)
    i2 = jnp.min(jnp.where(rest == m2, sub, EXPERTS_PER_GROUP), axis=0, keepdims=True)
    return m1 + m2, i1, i2


def _route_body(lg_ref, b_ref, e_ref, w_ref, r_ref, cnt_ref, tri_ref, base_ref, *, n_tok):
    t = pl.program_id(0)
    tt = lg_ref.shape[0]

    @pl.when(t == 0)
    def _():
        earlier = lax.broadcasted_iota(I32, (tt, tt), 0) < lax.broadcasted_iota(I32, (tt, tt), 1)
        tri_ref[...] = jnp.where(earlier, 1.0, 0.0).astype(MXU_DTYPE)
        base_ref[...] = jnp.zeros(base_ref.shape, F32)

    scores = jax.nn.sigmoid(lg_ref[...].T[:N_EXPERTS])
    biased = scores + b_ref[...]
    sub = lax.broadcasted_iota(I32, (EXPERTS_PER_GROUP, tt), 0)
    best, i1, i2 = _top2_of_group(biased[:EXPERTS_PER_GROUP], sub)
    g_sel = jnp.zeros((1, tt), I32)
    for g in range(1, N_GROUPS):
        score_g, i1_g, i2_g = _top2_of_group(biased[g * EXPERTS_PER_GROUP:(g + 1) * EXPERTS_PER_GROUP], sub)
        better = score_g > best
        best = jnp.where(better, score_g, best)
        g_sel = jnp.where(better, g, g_sel)
        i1 = jnp.where(better, i1_g, i1)
        i2 = jnp.where(better, i2_g, i2)
    eid = lax.broadcasted_iota(I32, (N_EXPERTS, tt), 0)
    valid = t * tt + lax.broadcasted_iota(I32, (1, tt), 1) < n_tok
    picks = [g_sel * EXPERTS_PER_GROUP + i1, g_sel * EXPERTS_PER_GROUP + i2]
    hit = [eid == e for e in picks]
    raw = [jnp.sum(jnp.where(h, scores, 0.0), axis=0, keepdims=True) for h in hit]
    denom = raw[0] + raw[1]
    base = base_ref[...]
    for k in range(TOP_K):
        onehot = jnp.where(hit[k] & valid, 1.0, 0.0)
        before = jnp.dot(onehot.astype(MXU_DTYPE), tri_ref[...], preferred_element_type=F32)
        rank = jnp.sum(onehot * (base + before), axis=0, keepdims=True)
        base = base + jnp.sum(onehot, axis=1, keepdims=True)
        e_ref[k:k + 1, :] = picks[k]
        w_ref[k:k + 1, :] = raw[k] / denom
        r_ref[k:k + 1, :] = rank.astype(I32)
    base_ref[...] = base
    cnt_ref[...] = jnp.broadcast_to(base, cnt_ref.shape)


def _route(logits, b_router, n_tok, n_rows):
    tm = TM_EXPERT
    tt = ROUTE_TILE
    n_steps = -(-n_tok // tt)
    n_pad = n_steps * tt
    pick_spec = pl.BlockSpec((TOP_K, tt), lambda t: (0, t))
    e_idx, wts, rank, cnt = pl.pallas_call(
        functools.partial(_route_body, n_tok=n_tok),
        grid=(n_steps,),
        in_specs=[pl.BlockSpec((tt, LANES), lambda t: (t, 0)), pl.BlockSpec((N_EXPERTS, 1), lambda t: (0, 0))],
        out_specs=[pick_spec, pick_spec, pick_spec, pl.BlockSpec((N_EXPERTS, LANES), lambda t: (0, 0))],
        out_shape=[jax.ShapeDtypeStruct((TOP_K, n_pad), I32), jax.ShapeDtypeStruct((TOP_K, n_pad), F32),
                   jax.ShapeDtypeStruct((TOP_K, n_pad), I32), jax.ShapeDtypeStruct((N_EXPERTS, LANES), F32)],
        scratch_shapes=[pltpu.VMEM((tt, tt), MXU_DTYPE), pltpu.VMEM((N_EXPERTS, 1), F32)],
        compiler_params=_params("arbitrary"),
        name="route",
    )(logits, b_router.astype(F32).reshape(N_EXPERTS, 1))
    counts = cnt[:, 0].astype(I32)
    padded = (counts + tm - 1) // tm * tm
    pend = jnp.cumsum(padded)
    pstart = pend - padded
    expert_ids = jnp.arange(N_EXPERTS, dtype=I32)[:, None, None]
    dest = rank + jnp.sum(jnp.where(e_idx[None] == expert_ids, pstart[:, None, None], 0), axis=0)
    gaps = jnp.stack([pstart + counts, pend], axis=1).reshape(-1).astype(I32)
    n_tiles = n_rows // tm
    n_used = (pend[-1] // tm).astype(I32)
    tile_ids = jnp.arange(n_tiles, dtype=I32)
    tile_start = jnp.minimum(tile_ids, n_used - 1) * tm
    tile_e = jnp.minimum(jnp.sum(pend[None, :] <= tile_start[:, None], axis=1), N_EXPERTS - 1).astype(I32)
    tile_valid = jnp.clip((pstart + counts)[tile_e] - tile_ids * tm, 0, tm).astype(I32)
    return gaps, tile_e, tile_valid, n_used.reshape(1), dest.reshape(-1), wts[:, :n_tok].T


def _top_k_indices(x, k):
    iota = lax.broadcasted_iota(I32, x.shape, x.ndim - 1)
    picks = []
    for _ in range(k):
        top = jnp.max(x, axis=-1, keepdims=True)
        idx = jnp.min(jnp.where(x == top, iota, x.shape[-1]), axis=-1, keepdims=True)
        picks.append(idx)
        x = jnp.where(iota == idx, -jnp.inf, x)
    return jnp.concatenate(picks, axis=-1)


def _rotary_tables(pos):
    inv_freq = ROPE_THETA ** (-jnp.arange(ROT_HALF, dtype=F32) / ROT_HALF)
    ang = pos.astype(F32)[:, None] * inv_freq[None, :]
    ones = jnp.ones((pos.shape[0], LANES - ROT_DIM), F32)
    cos_t = jnp.concatenate([jnp.cos(ang), jnp.cos(ang), ones], axis=1)
    sin_t = jnp.concatenate([jnp.sin(ang), jnp.sin(ang), 0.0 * ones], axis=1)
    return cos_t, sin_t


def kernel(x_prompt, x_sample, cache_k, cache_v, state_conv, page_table, c_prompt, c_sample,
           w_ada, b_ada, ln_g, ln_b, w_qkv, w_o, conv_w_in, conv_w_dw, conv_ln_g, conv_ln_b,
           conv_w_out, w_router, b_router, w_gate, w_up, w_down):
    batch, seq, d = x_prompt.shape
    db, dec_seq, _ = x_sample.shape
    assert dec_seq == 1 and seq % TM == 0 and TM % MOBA_BLOCK == 0 and d == N_HEADS * HEAD_DIM
    depth = w_ada.shape[0]
    n_pages = page_table.shape[1]
    past_len = n_pages * PAGE_SIZE
    assert n_pages % KMEAN_PAGES == 0 and n_pages % PAGES_PER_BLOCK == 0
    n_full = n_pages // PAGES_PER_BLOCK
    topk_s = min(MOBA_TOPK, n_full)
    alpha = (2 * depth) ** 0.25
    n_p = batch * seq
    n_all = n_p + db
    assert n_p % db == 0 and n_p % ROUTE_TILE == 0 and ROUTE_TILE % TM == 0
    n_buf = -(-n_all // ROUTE_TILE) * ROUTE_TILE
    tps = seq // TM

    c_rows = -(-(batch + db) // 8) * 8
    c_all = jnp.concatenate([c_prompt, c_sample, jnp.zeros((c_rows - batch - db, d), F32)], axis=0)
    mod = _ada(c_all, w_ada, b_ada)

    wr_pad = jnp.pad(w_router, ((0, 0), (0, LANES - N_EXPERTS))).astype(MXU_DTYPE)
    n_assign = n_all * TOP_K
    n_rows = -(-(n_assign + N_EXPERTS * (TM_EXPERT - 1)) // TM_EXPERT) * TM_EXPERT

    cos_p, sin_p = _rotary_tables(jnp.arange(seq))
    cos_s, sin_s = _rotary_tables(past_len + jnp.zeros((db,), I32))

    xp = x_prompt.reshape(n_p, d)
    xs = x_sample.reshape(db, d)
    kp_pages, vp_pages, ks_rows, vs_rows, conv_p, conv_s = [], [], [], [], [], []
    for i in range(depth):
        mp = [m[:, None, :] for m in jnp.split(mod[i, :batch], 6, axis=-1)]
        ms = [m[None] for m in jnp.split(mod[i, batch:batch + db], 6, axis=-1)]
        if i % 2 == 0:
            ia = i // 2
            q, kp, vp, kx, v16, km = _qkv_prompt(xp, mp[0], mp[1], w_qkv[ia], cos_p, sin_p, batch, seq)
            km = km.transpose(0, 2, 1, 3, 4).reshape(batch, N_KV_HEADS, seq // MOBA_BLOCK, HEAD_DIM)
            attn_p = _moba_prompt(q, kx, v16, km, batch, seq)
            kp_pages.append(kp)
            vp_pages.append(vp)

            qkv_s = _qkv_sample(xs, ms[0][0], ms[1][0], w_qkv[ia], cos_s, sin_s)
            nq = N_HEADS * HEAD_DIM
            nk = N_KV_HEADS * HEAD_DIM
            q_s = qkv_s[:, :nq].reshape(db, N_HEADS, HEAD_DIM)
            k_s = qkv_s[:, nq:nq + nk].reshape(db, N_KV_HEADS, HEAD_DIM)
            v_s = qkv_s[:, nq + nk:].reshape(db, N_KV_HEADS, HEAD_DIM)
            ks_rows.append(k_s[:, :, None, :])
            vs_rows.append(v_s[:, :, None, :])
            if topk_s > 0:
                kmean_s = _kmean_sample(page_table, cache_k, ia)
                kvh = jnp.arange(N_HEADS) // GQA_GROUP
                gate_s = jnp.einsum("bhd,bnhd->bhn", q_s, kmean_s[:, :, kvh])
                sel = _top_k_indices(gate_s, topk_s)
                sel_pages = (sel[..., None] * PAGES_PER_BLOCK + jnp.arange(PAGES_PER_BLOCK)).reshape(db, N_HEADS, -1)
                phys = jnp.take_along_axis(page_table[:, None, :], sel_pages, axis=2).astype(I32)
                attn_s = _attn_sample(phys.reshape(-1), q_s.reshape(db * N_HEADS, 1, HEAD_DIM),
                                      k_s.reshape(db * N_KV_HEADS, 1, HEAD_DIM),
                                      v_s.reshape(db * N_KV_HEADS, 1, HEAD_DIM), cache_k, cache_v, ia)
                attn_s = attn_s.reshape(db, nq)
            else:
                attn_s = jnp.repeat(v_s, GQA_GROUP, axis=1).reshape(db, nq)
            mixer_p, mixer_s, w_mix = attn_p, attn_s, w_o[ia]
            mode_p, mode_s = "attn", "attn"
        else:
            ic = i // 2
            u_p = _glu(xp, mp[0], mp[1], conv_w_in[ic], TM, tps)
            u_s = _glu(xs, ms[0], ms[1], conv_w_in[ic], db, 1)
            cg, cb = conv_ln_g[ic][None], conv_ln_b[ic][None]
            mixer_p = (u_p, conv_w_dw[ic], cg, cb)
            mixer_s = (u_s, state_conv[ic].transpose(1, 0, 2), conv_w_dw[ic], cg, cb)
            w_mix = conv_w_out[ic]
            mode_p, mode_s = "conv", "conv_step"
            conv_p.append(u_p.reshape(batch, seq, d)[:, seq - (CONV_WIDTH - 1):])
            conv_s.append(jnp.concatenate([state_conv[ic][:, 1:], u_s[:, None, :]], axis=1))

        x1p, h2_all, lg_all = _post(mode_p, mixer_p, w_mix, xp, mp[2], ln_g[i, 0][None], ln_b[i, 0][None],
                                    mp[3], mp[4], wr_pad, tm=TM, rows_per_mod=tps, alpha=alpha,
                                    seq=seq, out_rows=n_buf)
        x1s, h2_all, lg_all = _post(mode_s, mixer_s, w_mix, xs, ms[2], ln_g[i, 0][None], ln_b[i, 0][None],
                                    ms[3], ms[4], wr_pad, tm=db, rows_per_mod=1, alpha=alpha,
                                    into=(h2_all, lg_all, n_p // db))
        gaps, tile_e, tile_valid, n_used, dest, wts = _route(lg_all, b_router, n_all, n_rows)
        ys = _experts(dest, gaps, tile_e, tile_valid, n_used, h2_all, w_gate, w_up, w_down, i, n_all, n_rows)
        xp = _combine(dest, ys, x1p, wts[:n_p], mp[5], ln_g[i, 1][None], ln_b[i, 1][None],
                      tm=TM_COMBINE, rows_per_mod=seq // TM_COMBINE, alpha=alpha, tok0=0)
        xs = _combine(dest, ys, x1s, wts[n_p:], ms[5], ln_g[i, 1][None], ln_b[i, 1][None],
                      tm=db, rows_per_mod=1, alpha=alpha, tok0=n_p)

    k_prompt = jnp.concatenate(kp_pages, axis=2)
    v_prompt = jnp.concatenate(vp_pages, axis=2)
    return (xp.reshape(batch, seq, d), xs.reshape(db, 1, d), k_prompt, v_prompt, jnp.stack(conv_p, axis=0),
            jnp.stack(ks_rows, axis=1), jnp.stack(vs_rows, axis=1), jnp.stack(conv_s, axis=0))
```

```python
import functools
import math

import jax
import jax.numpy as jnp
from jax import lax
from jax.experimental import pallas as pl
from jax.experimental.pallas import tpu as pltpu

F32 = jnp.float32
I32 = jnp.int32
MXU_DTYPE = jnp.bfloat16

N_HEADS = 8
N_KV_HEADS = 2
GQA_GROUP = N_HEADS // N_KV_HEADS
HEAD_DIM = 128
ROT_DIM = HEAD_DIM // 4
ROT_HALF = ROT_DIM // 2
ROPE_THETA = 500000.0
ATTN_SCALE = HEAD_DIM ** -0.5
MOBA_BLOCK = 256
MOBA_TOPK = 3
MOBA_CHUNK = 128
EXP2_SCALE = ATTN_SCALE * math.log2(math.e)
PAGE_SIZE = 128
PAGES_PER_BLOCK = MOBA_BLOCK // PAGE_SIZE
CONV_WIDTH = 31
CONV_HALO = 32
N_EXPERTS = 32
N_GROUPS = 4
EXPERTS_PER_GROUP = N_EXPERTS // N_GROUPS
TOP_K = 2
LN_EPS = 1e-5
MASK_VALUE = -1e30
LANES = 128
SUBLANES = 8
VMEM_LIMIT = 56 * 1024 * 1024

TM = 512
TM_EXPERT = 256
TM_COMBINE = 512
KMEAN_PAGES = 64
ROUTE_TILE = 512
GATHER_UNROLL = 8


def _params(*sem, row_gather=False):
    return pltpu.CompilerParams(dimension_semantics=sem, vmem_limit_bytes=VMEM_LIMIT,
                                disable_bounds_checks=row_gather)


def _dot(a, b):
    return jnp.dot(a.astype(MXU_DTYPE), b.astype(MXU_DTYPE), preferred_element_type=F32)


def _dot_nt(a, b):
    return lax.dot_general(a.astype(MXU_DTYPE), b.astype(MXU_DTYPE), (((1,), (1,)), ((), ())),
                           preferred_element_type=F32)


def _mxu_round(x):
    return x.astype(MXU_DTYPE).astype(F32)


def _store_row_tiles(ref, val):
    rows, d = val.shape
    per_row = d // LANES
    for s in range(per_row):
        ref[pl.ds(s, rows, stride=per_row), :] = val[:, s * LANES:(s + 1) * LANES]


def _load_row_tiles(ref, per_row):
    rows = ref.shape[0] // per_row
    return jnp.concatenate([ref[pl.ds(s, rows, stride=per_row), :] for s in range(per_row)], axis=1)


def _layer_norm(z, g, b):
    mu = jnp.mean(z, axis=-1, keepdims=True)
    zc = z - mu
    var = jnp.mean(zc * zc, axis=-1, keepdims=True)
    return zc * lax.rsqrt(var + LN_EPS) * g + b


def _silu(x):
    return x * jax.nn.sigmoid(x)


def _rotary(xc, cos, sin, lane):
    x_up = pltpu.roll(xc, LANES - ROT_HALF, axis=1)
    x_dn = pltpu.roll(xc, ROT_HALF, axis=1)
    first = xc * cos - x_up * sin
    second = xc * cos + x_dn * sin
    return jnp.where(lane < ROT_HALF, first, jnp.where(lane < ROT_DIM, second, xc))


def _ada_body(c_ref, w_ref, b_ref, o_ref):
    o_ref[0] = _dot(_silu(c_ref[...]), w_ref[0]) + b_ref[0]


def _ada(c_all, w_ada, b_ada):
    depth, d, n6 = w_ada.shape
    rows = c_all.shape[0]
    tn = 1536
    return pl.pallas_call(
        _ada_body,
        grid=(depth, n6 // tn),
        in_specs=[pl.BlockSpec((rows, d), lambda i, j: (0, 0)),
                  pl.BlockSpec((1, d, tn), lambda i, j: (i, 0, j)),
                  pl.BlockSpec((1, 1, tn), lambda i, j: (i, 0, j))],
        out_specs=pl.BlockSpec((1, rows, tn), lambda i, j: (i, 0, j)),
        out_shape=jax.ShapeDtypeStruct((depth, rows, n6), F32),
        compiler_params=_params("arbitrary", "arbitrary"),
        name="ada",
    )(c_all, w_ada, b_ada.reshape(depth, 1, n6))


def _qkv_prompt_body(x_ref, sh_ref, sc_ref, w_ref, cos_ref, sin_ref,
                     q_ref, kp_ref, vp_ref, kx_ref, v16_ref, km_ref, w16, *, tiles_per_seq):
    t = pl.program_id(0)
    tm = x_ref.shape[0]

    @pl.when(t == 0)
    def _():
        w16[...] = w_ref[...].astype(MXU_DTYPE)

    h = x_ref[...] * (1 + sc_ref[0]) + sh_ref[0]
    qkv = jnp.dot(h.astype(MXU_DTYPE), w16[...], preferred_element_type=F32)
    cos = cos_ref[...]
    sin = sin_ref[...]
    lane = lax.broadcasted_iota(I32, (tm, LANES), 1)
    nq = N_HEADS * HEAD_DIM
    nk = N_KV_HEADS * HEAD_DIM
    for hh in range(N_HEADS):
        sl = slice(hh * HEAD_DIM, (hh + 1) * HEAD_DIM)
        q_ref[:, sl] = _rotary(qkv[:, sl], cos, sin, lane).astype(q_ref.dtype)
    row = lax.broadcasted_iota(I32, (tm, LANES), 0)
    blk = ((t % tiles_per_seq) * tm + row) // MOBA_BLOCK
    onehot = jnp.where(lane == blk, 1.0, 0.0).astype(kx_ref.dtype)
    for n in range(N_KV_HEADS):
        kc = _rotary(qkv[:, nq + n * HEAD_DIM:nq + (n + 1) * HEAD_DIM], cos, sin, lane)
        vc = qkv[:, nq + nk + n * HEAD_DIM:nq + nk + (n + 1) * HEAD_DIM]
        kp_ref[0, :, 0, n] = kc.reshape(tm // PAGE_SIZE, PAGE_SIZE, HEAD_DIM)
        vp_ref[0, :, 0, n] = vc.reshape(tm // PAGE_SIZE, PAGE_SIZE, HEAD_DIM)
        kx_ref[0, n, :, 0:HEAD_DIM] = kc.astype(kx_ref.dtype)
        kx_ref[0, n, :, HEAD_DIM:2 * HEAD_DIM] = onehot
        v16_ref[0, n] = vc.astype(v16_ref.dtype)
        km_ref[0, 0, n] = jnp.sum(kc.reshape(tm // MOBA_BLOCK, MOBA_BLOCK, HEAD_DIM), axis=1) * (1.0 / MOBA_BLOCK)


def _qkv_prompt(x, shift, scale, w_qkv, cos_t, sin_t, batch, seq):
    n, d = x.shape
    tm = TM
    tps = seq // tm
    width = w_qkv.shape[1]
    npg = seq // PAGE_SIZE
    mod_spec = pl.BlockSpec((1, 1, d), lambda t: (t // tps, 0, 0))
    rot_spec = pl.BlockSpec((tm, LANES), lambda t: (t % tps, 0))
    page_spec = pl.BlockSpec((1, tm // PAGE_SIZE, 1, N_KV_HEADS, PAGE_SIZE, HEAD_DIM),
                             lambda t: (t // tps, t % tps, 0, 0, 0, 0))
    page_shape = jax.ShapeDtypeStruct((batch, npg, 1, N_KV_HEADS, PAGE_SIZE, HEAD_DIM), F32)
    return pl.pallas_call(
        functools.partial(_qkv_prompt_body, tiles_per_seq=tps),
        grid=(n // tm,),
        in_specs=[pl.BlockSpec((tm, d), lambda t: (t, 0)), mod_spec, mod_spec,
                  pl.BlockSpec((d, width), lambda t: (0, 0)), rot_spec, rot_spec],
        out_specs=[pl.BlockSpec((tm, N_HEADS * HEAD_DIM), lambda t: (t, 0)),
                   page_spec, page_spec,
                   pl.BlockSpec((1, N_KV_HEADS, tm, 2 * HEAD_DIM), lambda t: (t // tps, 0, t % tps, 0)),
                   pl.BlockSpec((1, N_KV_HEADS, tm, HEAD_DIM), lambda t: (t // tps, 0, t % tps, 0)),
                   pl.BlockSpec((1, 1, N_KV_HEADS, tm // MOBA_BLOCK, HEAD_DIM), lambda t: (t // tps, t % tps, 0, 0, 0))],
        out_shape=[jax.ShapeDtypeStruct((n, N_HEADS * HEAD_DIM), MXU_DTYPE),
                   page_shape, page_shape,
                   jax.ShapeDtypeStruct((batch, N_KV_HEADS, seq, 2 * HEAD_DIM), MXU_DTYPE),
                   jax.ShapeDtypeStruct((batch, N_KV_HEADS, seq, HEAD_DIM), MXU_DTYPE),
                   jax.ShapeDtypeStruct((batch, tps, N_KV_HEADS, tm // MOBA_BLOCK, HEAD_DIM), F32)],
        scratch_shapes=[pltpu.VMEM((d, width), MXU_DTYPE)],
        compiler_params=_params("arbitrary"),
        name="qkv_prompt",
    )(x, shift, scale, w_qkv, cos_t, sin_t)


def _qkv_sample_body(x_ref, sh_ref, sc_ref, w_ref, cos_ref, sin_ref, o_ref):
    rows = x_ref.shape[0]
    h = x_ref[...] * (1 + sc_ref[...]) + sh_ref[...]
    qkv = _dot(h, w_ref[...])
    lane = lax.broadcasted_iota(I32, (rows, LANES), 1)
    n_rot = N_HEADS + N_KV_HEADS
    for c in range(n_rot):
        sl = slice(c * HEAD_DIM, (c + 1) * HEAD_DIM)
        o_ref[:, sl] = _rotary(qkv[:, sl], cos_ref[...], sin_ref[...], lane)
    o_ref[:, n_rot * HEAD_DIM:] = qkv[:, n_rot * HEAD_DIM:]


def _qkv_sample(x, shift, scale, w_qkv, cos_t, sin_t):
    rows, d = x.shape
    width = w_qkv.shape[1]
    full = lambda shape: pl.BlockSpec(shape, lambda: tuple(0 for _ in shape))
    return pl.pallas_call(
        _qkv_sample_body,
        in_specs=[full((rows, d)), full((rows, d)), full((rows, d)), full((d, width)),
                  full((rows, LANES)), full((rows, LANES))],
        out_specs=full((rows, width)),
        out_shape=jax.ShapeDtypeStruct((rows, width), F32),
        compiler_params=pltpu.CompilerParams(vmem_limit_bytes=VMEM_LIMIT),
        name="qkv_sample",
    )(x, shift, scale, w_qkv, cos_t, sin_t)


def _moba_prompt_body(q_ref, kx_ref, v_ref, km_ref, o_ref, qx_ref, *state):
    i = pl.program_id(2)
    rows = GQA_GROUP * MOBA_BLOCK
    n_chunks = rows // MOBA_CHUNK
    s_refs, p_refs, a_refs, acc_ref = state[0:2], state[2:4], state[4:6], state[6]
    m_refs, l_refs = state[7:7 + n_chunks], state[7 + n_chunks:]
    n_blk = km_ref.shape[2]
    for h in range(GQA_GROUP):
        qx_ref[h * MOBA_BLOCK:(h + 1) * MOBA_BLOCK, 0:HEAD_DIM] = q_ref[:, h * HEAD_DIM:(h + 1) * HEAD_DIM]

    gate = _dot_nt(km_ref[0, 0], qx_ref[:, 0:HEAD_DIM])
    blk = lax.broadcasted_iota(I32, (n_blk, rows), 0)
    valid = blk < i
    cand = jnp.where(valid, gate, -jnp.inf)
    sel = blk == i
    for _ in range(MOBA_TOPK):
        top = jnp.max(cand, axis=0, keepdims=True)
        idx = jnp.min(jnp.where(cand == top, blk, n_blk), axis=0, keepdims=True)
        pick = blk == idx
        sel = sel | (pick & valid)
        cand = jnp.where(pick, -jnp.inf, cand)
    bias = jnp.where(sel, 0.0, MASK_VALUE)
    if n_blk < LANES:
        bias = jnp.concatenate([bias, jnp.zeros((LANES - n_blk, rows), F32)], axis=0)
    qx_ref[:, HEAD_DIM:2 * HEAD_DIM] = bias.T.astype(qx_ref.dtype)

    half = MOBA_BLOCK // 2

    def issue_scores(j, slot):
        start = pl.multiple_of(j * MOBA_BLOCK, MOBA_BLOCK)
        s_refs[slot][...] = _dot_nt(qx_ref[...], kx_ref[0, 0, pl.ds(start, MOBA_BLOCK), :])

    def softmax(slot, own):
        for c in range(n_chunks):
            rs = slice(c * MOBA_CHUNK, (c + 1) * MOBA_CHUNK)
            s = s_refs[slot][rs, :]
            if own:
                qpos = (c * MOBA_CHUNK) % MOBA_BLOCK + lax.broadcasted_iota(I32, (MOBA_CHUNK, MOBA_BLOCK), 0)
                kpos = lax.broadcasted_iota(I32, (MOBA_CHUNK, MOBA_BLOCK), 1)
                s = jnp.where(kpos <= qpos, s, MASK_VALUE)
            sa, sb = s[:, :half], s[:, half:]
            top = jnp.broadcast_to(jnp.max(jnp.maximum(sa, sb), axis=1, keepdims=True), (MOBA_CHUNK, half))
            if own:
                m_new = top
            else:
                m_old = m_refs[c][...]
                m_new = jnp.maximum(m_old, top)
                alpha = jnp.exp2((m_old - m_new) * EXP2_SCALE)
                a_refs[slot][rs, :] = alpha
            pa = jnp.exp2((sa - m_new) * EXP2_SCALE)
            pb = jnp.exp2((sb - m_new) * EXP2_SCALE)
            p_refs[slot][rs, :half] = pa.astype(MXU_DTYPE)
            p_refs[slot][rs, half:] = pb.astype(MXU_DTYPE)
            if own:
                l_refs[c][...] = pa + pb
            else:
                l_refs[c][...] = alpha * l_refs[c][...] + (pa + pb)
            m_refs[c][...] = m_new

    def accumulate(j, slot, own):
        start = pl.multiple_of(j * MOBA_BLOCK, MOBA_BLOCK)
        pv = jnp.dot(p_refs[slot][...], v_ref[0, 0, pl.ds(start, MOBA_BLOCK), :], preferred_element_type=F32)
        if own:
            acc_ref[...] = pv
        else:
            acc_ref[...] = a_refs[slot][...] * acc_ref[...] + pv

    issue_scores(i, 1)
    issue_scores(0, 0)
    softmax(1, True)
    accumulate(i, 1, True)

    def pair(t, carry):
        j = 2 * t
        issue_scores(j + 1, 1)
        softmax(0, False)
        accumulate(j, 0, False)
        issue_scores(jnp.minimum(j + 2, i - 1), 0)
        softmax(1, False)
        accumulate(j + 1, 1, False)
        return carry

    lax.fori_loop(0, i // 2, pair, 0)

    @pl.when(i % 2 == 1)
    def _():
        softmax(0, False)
        accumulate(i - 1, 0, False)

    per_head = MOBA_BLOCK // MOBA_CHUNK
    for c in range(n_chunks):
        rs = slice(c * MOBA_CHUNK, (c + 1) * MOBA_CHUNK)
        out = acc_ref[rs, :] / jnp.sum(l_refs[c][...], axis=1, keepdims=True)
        h, part = divmod(c, per_head)
        o_ref[part * MOBA_CHUNK:(part + 1) * MOBA_CHUNK, h * HEAD_DIM:(h + 1) * HEAD_DIM] = out.astype(o_ref.dtype)


def _moba_prompt(q, kx, v16, k_mean, batch, seq):
    n = q.shape[0]
    nq = seq // MOBA_BLOCK
    rows = GQA_GROUP * MOBA_BLOCK
    gw = GQA_GROUP * HEAD_DIM
    return pl.pallas_call(
        _moba_prompt_body,
        grid=(batch, N_KV_HEADS, nq),
        in_specs=[pl.BlockSpec((MOBA_BLOCK, gw), lambda b, g, i: (b * nq + i, g)),
                  pl.BlockSpec((1, 1, seq, 2 * HEAD_DIM), lambda b, g, i: (b, g, 0, 0)),
                  pl.BlockSpec((1, 1, seq, HEAD_DIM), lambda b, g, i: (b, g, 0, 0)),
                  pl.BlockSpec((1, 1, nq, HEAD_DIM), lambda b, g, i: (b, g, 0, 0))],
        out_specs=pl.BlockSpec((MOBA_BLOCK, gw), lambda b, g, i: (b * nq + i, g)),
        out_shape=jax.ShapeDtypeStruct((n, N_HEADS * HEAD_DIM), MXU_DTYPE),
        scratch_shapes=([pltpu.VMEM((rows, 2 * HEAD_DIM), MXU_DTYPE)]
                        + [pltpu.VMEM((rows, MOBA_BLOCK), F32)] * 2
                        + [pltpu.VMEM((rows, MOBA_BLOCK), MXU_DTYPE)] * 2
                        + [pltpu.VMEM((rows, MOBA_BLOCK // 2), F32)] * 2
                        + [pltpu.VMEM((rows, HEAD_DIM), F32)]
                        + [pltpu.VMEM((MOBA_CHUNK, MOBA_BLOCK // 2), F32)] * (2 * (rows // MOBA_CHUNK))),
        compiler_params=_params("arbitrary", "arbitrary", "arbitrary"),
        name="moba_prompt",
    )(q, kx, v16, k_mean)


def _kmean_copy(pt_ref, ck_hbm, buf, sem, layer, step, slot, p):
    phys = pt_ref[step * KMEAN_PAGES + p]
    return pltpu.make_async_copy(ck_hbm.at[phys, layer], buf.at[slot, p], sem.at[slot])


def _kmean_sample_body(pt_ref, ck_hbm, o_ref, buf, sem, *, layer):
    nc = pl.num_programs(1)
    step = pl.program_id(0) * nc + pl.program_id(1)
    total = pl.num_programs(0) * nc
    slot = step % 2

    def start(step_, slot_):
        for p in range(KMEAN_PAGES):
            _kmean_copy(pt_ref, ck_hbm, buf, sem, layer, step_, slot_, p).start()

    @pl.when(step == 0)
    def _():
        start(step, slot)

    @pl.when(step + 1 < total)
    def _():
        start(step + 1, 1 - slot)

    for p in range(KMEAN_PAGES):
        _kmean_copy(pt_ref, ck_hbm, buf, sem, layer, step, slot, p).wait()
    for jb in range(KMEAN_PAGES // PAGES_PER_BLOCK):
        for n in range(N_KV_HEADS):
            acc = jnp.zeros((1, HEAD_DIM), F32)
            for pp in range(PAGES_PER_BLOCK):
                acc = acc + jnp.sum(buf[slot, jb * PAGES_PER_BLOCK + pp, n], axis=0, keepdims=True)
            o_ref[0, jb, pl.ds(n, 1), :] = acc * (1.0 / MOBA_BLOCK)


def _kmean_sample(page_table, cache_k, layer):
    db, n_pages = page_table.shape
    n_full = n_pages // PAGES_PER_BLOCK
    page_shape = cache_k.shape[2:]
    bps = KMEAN_PAGES // PAGES_PER_BLOCK
    return pl.pallas_call(
        functools.partial(_kmean_sample_body, layer=layer),
        grid_spec=pltpu.PrefetchScalarGridSpec(
            num_scalar_prefetch=1,
            grid=(db, n_pages // KMEAN_PAGES),
            in_specs=[pl.BlockSpec(memory_space=pl.ANY)],
            out_specs=pl.BlockSpec((1, bps, N_KV_HEADS, HEAD_DIM), lambda b, c, pt: (b, c, 0, 0)),
            scratch_shapes=[pltpu.VMEM((2, KMEAN_PAGES) + page_shape, F32),
                            pltpu.SemaphoreType.DMA((2,))]),
        out_shape=jax.ShapeDtypeStruct((db, n_full, N_KV_HEADS, HEAD_DIM), F32),
        compiler_params=_params("arbitrary", "arbitrary"),
        name="kmean_sample",
    )(page_table.reshape(-1), cache_k)


def _attn_sample_body(pages_ref, q_ref, kn_ref, vn_ref, *refs):
    n_sel = (len(refs) - 1) // 2
    k_refs, v_refs, o_ref = refs[:n_sel], refs[n_sel:2 * n_sel], refs[-1]
    q = _mxu_round(q_ref[0])
    s_new = jnp.sum(q * _mxu_round(kn_ref[0]), axis=1, keepdims=True) * ATTN_SCALE
    s_sel = [jnp.sum(_mxu_round(k_ref[0, 0, 0]) * q, axis=1, keepdims=True) * ATTN_SCALE
             for k_ref in k_refs]
    m = s_new
    for s in s_sel:
        m = jnp.maximum(m, jnp.max(s, axis=0, keepdims=True))
    p_new = jnp.exp(s_new - m)
    denom = p_new
    out = _mxu_round(p_new) * _mxu_round(vn_ref[0])
    for s, v_ref in zip(s_sel, v_refs):
        p = jnp.exp(s - m)
        denom = denom + jnp.sum(p, axis=0, keepdims=True)
        out = out + jnp.sum(_mxu_round(p) * _mxu_round(v_ref[0, 0, 0]), axis=0, keepdims=True)
    o_ref[0] = out / denom


def _attn_sample(phys, q, k_new, v_new, cache_k, cache_v, layer):
    db = q.shape[0] // N_HEADS
    n_sel = phys.shape[0] // (db * N_HEADS)
    vec = lambda f: pl.BlockSpec((1, 1, HEAD_DIM), f)

    def page_spec(p):
        return pl.BlockSpec((1, 1, 1, PAGE_SIZE, HEAD_DIM),
                            lambda b, h, pg: (pg[(b * N_HEADS + h) * n_sel + p], layer, h // GQA_GROUP, 0, 0))

    page_specs = [page_spec(p) for p in range(n_sel)]
    return pl.pallas_call(
        _attn_sample_body,
        grid_spec=pltpu.PrefetchScalarGridSpec(
            num_scalar_prefetch=1,
            grid=(db, N_HEADS),
            in_specs=[vec(lambda b, h, pg: (b * N_HEADS + h, 0, 0)),
                      vec(lambda b, h, pg: (b * N_KV_HEADS + h // GQA_GROUP, 0, 0)),
                      vec(lambda b, h, pg: (b * N_KV_HEADS + h // GQA_GROUP, 0, 0))] + page_specs + page_specs,
            out_specs=vec(lambda b, h, pg: (b * N_HEADS + h, 0, 0))),
        out_shape=jax.ShapeDtypeStruct((db * N_HEADS, 1, HEAD_DIM), F32),
        compiler_params=_params("arbitrary", "arbitrary"),
        name="attn_sample",
    )(phys, q, k_new, v_new, *([cache_k] * n_sel), *([cache_v] * n_sel))


def _glu_body(x_ref, sh_ref, sc_ref, w_ref, o_ref, w16):
    c = o_ref.shape[1]
    h = x_ref[...] * (1 + sc_ref[0]) + sh_ref[0]

    @pl.when(pl.program_id(0) == 0)
    def _():
        w16[...] = w_ref[...].astype(MXU_DTYPE)

    ag = jnp.dot(h.astype(MXU_DTYPE), w16[...], preferred_element_type=F32)
    o_ref[...] = ag[:, :c] * jax.nn.sigmoid(ag[:, c:])


def _glu(x, shift, scale, w_in, tm, rows_per_mod):
    n, d = x.shape
    c2 = w_in.shape[1]
    r = shift.shape[1]
    mod_spec = pl.BlockSpec((1, r, d), lambda t: (t // rows_per_mod, 0, 0))
    return pl.pallas_call(
        _glu_body,
        grid=(n // tm,),
        in_specs=[pl.BlockSpec((tm, d), lambda t: (t, 0)), mod_spec, mod_spec,
                  pl.BlockSpec((d, c2), lambda t: (0, 0))],
        out_specs=pl.BlockSpec((tm, c2 // 2), lambda t: (t, 0)),
        out_shape=jax.ShapeDtypeStruct((n, c2 // 2), F32),
        scratch_shapes=[pltpu.VMEM((d, c2), MXU_DTYPE)],
        compiler_params=_params("arbitrary"),
        name="glu",
    )(x, shift, scale, w_in)


def _post_body(*refs, n_tiles, **kw):
    n_mixer = 1 if kw["mode"] == "attn" else 5
    h2_ref, lg_ref = refs[n_mixer + 9], refs[n_mixer + 10]
    t = pl.program_id(0)

    @pl.when(t < n_tiles)
    def _():
        _post_tile(*refs, **kw)

    @pl.when(t >= n_tiles)
    def _():
        h2_ref[...] = jnp.zeros(h2_ref.shape, F32)
        lg_ref[...] = jnp.zeros(lg_ref.shape, F32)


def _post_tile(*refs, mode, alpha, tiles_per_seq):
    if mode == "attn":
        a_ref, rest = refs[0], refs[1:]
    elif mode == "conv":
        u_ref, prev_ref, wdw_ref, cg_ref, cb_ref = refs[:5]
        rest = refs[5:]
    else:
        u_ref, st_ref, wdw_ref, cg_ref, cb_ref = refs[:5]
        rest = refs[5:]
    (w_ref, x_ref, g1_ref, lng_ref, lnb_ref, sh2_ref, sc2_ref, wr_ref,
     x1_ref, h2_ref, lg_ref, w16) = rest[:12]
    t = pl.program_id(0)
    tm = x_ref.shape[0]

    if mode == "attn":
        a = a_ref[...]
    else:
        wdw = _mxu_round(wdw_ref[...])
        if mode == "conv":
            ext_ref, z_ref = rest[12:14]
            first = (t % tiles_per_seq) == 0
            ext_ref[0:CONV_HALO] = _mxu_round(jnp.where(first, 0.0, prev_ref[...]))
            ext_ref[CONV_HALO:CONV_HALO + tm] = _mxu_round(u_ref[...])
            ext_ref[CONV_HALO + tm:] = jnp.zeros((SUBLANES, x_ref.shape[1]), F32)
            off = CONV_HALO - (CONV_WIDTH - 1)
            y = None
            for b in range(SUBLANES):
                z = None
                for a in range(-(-(off + CONV_WIDTH) // SUBLANES)):
                    w = a * SUBLANES + b - off
                    if 0 <= w < CONV_WIDTH:
                        term = ext_ref[pl.ds(a * SUBLANES, tm + SUBLANES), :] * wdw[w:w + 1, :]
                        z = term if z is None else z + term
                if b == 0:
                    y = z[0:tm]
                else:
                    z_ref[...] = z
                    y = y + z_ref[pl.ds(b, tm), :]
        else:
            y = _mxu_round(u_ref[...]) * wdw[CONV_WIDTH - 1:CONV_WIDTH, :]
            for w in range(CONV_WIDTH - 1):
                y = y + _mxu_round(st_ref[w]) * wdw[w:w + 1, :]
        a = _silu(_layer_norm(y, cg_ref[...], cb_ref[...]))

    @pl.when(t == 0)
    def _():
        w16[...] = w_ref[...].astype(MXU_DTYPE)

    f = jnp.dot(a.astype(MXU_DTYPE), w16[...], preferred_element_type=F32)
    x1 = _layer_norm(alpha * x_ref[...] + g1_ref[0] * f, lng_ref[...], lnb_ref[...])
    h2 = x1 * (1 + sc2_ref[0]) + sh2_ref[0]
    x1_ref[...] = x1
    _store_row_tiles(h2_ref, h2)
    lg_ref[...] = _dot(h2, wr_ref[...])


def _post(mode, mixer_in, w, x, gate1, ln_g, ln_b, shift2, scale2, w_router, *, tm, rows_per_mod,
          alpha, seq=None, out_rows=None, into=None):
    n, d = x.shape
    out_rows = n if out_rows is None else out_rows
    r = gate1.shape[1]
    tps = None if seq is None else seq // tm
    n_tiles = n // tm
    n_steps = -(-out_rows // tm)
    last = n_tiles - 1
    row = lambda t: (jnp.minimum(t, last), 0)
    const = lambda t: (0, 0)
    mod_spec = pl.BlockSpec((1, r, d), lambda t: (jnp.minimum(t, last) // rows_per_mod, 0, 0))
    vec_spec = pl.BlockSpec((1, d), const)
    if mode == "attn":
        mixer_specs = [pl.BlockSpec((tm, d), row)]
    elif mode == "conv":
        u, w_dw, cg, cb = mixer_in
        per = tm // CONV_HALO
        mixer_in = (u, u, w_dw, cg, cb)
        mixer_specs = [pl.BlockSpec((tm, d), row),
                       pl.BlockSpec((CONV_HALO, d), lambda t: (jnp.maximum(jnp.minimum(t, last) * per - 1, 0), 0)),
                       pl.BlockSpec(w_dw.shape, const), vec_spec, vec_spec]
    else:
        u, state, w_dw, cg, cb = mixer_in
        mixer_specs = [pl.BlockSpec((tm, d), row), pl.BlockSpec(state.shape, lambda t: (0, 0, 0)),
                       pl.BlockSpec(w_dw.shape, const), vec_spec, vec_spec]
    if mode == "attn":
        mixer_in = (mixer_in,)
    in_specs = mixer_specs + [pl.BlockSpec(w.shape, const), pl.BlockSpec((tm, d), row), mod_spec, vec_spec, vec_spec,
                              mod_spec, mod_spec, pl.BlockSpec(w_router.shape, const)]
    args = list(mixer_in) + [w, x, gate1, ln_g, ln_b, shift2, scale2, w_router]
    out_shape = [jax.ShapeDtypeStruct((n, d), F32)]
    out_specs = [pl.BlockSpec((tm, d), row)]
    aliases = {}
    if into is None:
        out_shape += [jax.ShapeDtypeStruct((out_rows * (d // LANES), LANES), F32),
                      jax.ShapeDtypeStruct((out_rows, LANES), F32)]
        out_specs += [pl.BlockSpec((tm * (d // LANES), LANES), lambda t: (t, 0)),
                      pl.BlockSpec((tm, LANES), lambda t: (t, 0))]
    else:
        h2_all, lg_all, row_block = into
        out_shape += [jax.ShapeDtypeStruct(h2_all.shape, F32), jax.ShapeDtypeStruct(lg_all.shape, F32)]
        out_specs += [pl.BlockSpec((tm * (d // LANES), LANES), lambda t: (row_block + t, 0)),
                      pl.BlockSpec((tm, LANES), lambda t: (row_block + t, 0))]
        aliases = {len(args): 1, len(args) + 1: 2}
        in_specs += [pl.BlockSpec(memory_space=pl.ANY), pl.BlockSpec(memory_space=pl.ANY)]
        args += [h2_all, lg_all]
    scratch = [pltpu.VMEM(w.shape, MXU_DTYPE)]
    if mode == "conv":
        scratch += [pltpu.VMEM((CONV_HALO + tm + SUBLANES, d), F32), pltpu.VMEM((tm + SUBLANES, d), F32)]

    def body(*refs):
        if into is not None:
            n_in = len(args)
            refs = refs[:n_in - 2] + refs[n_in:]
        _post_body(*refs, n_tiles=n_tiles, mode=mode, alpha=alpha, tiles_per_seq=tps)

    return pl.pallas_call(
        body,
        grid=(n_steps,),
        in_specs=in_specs,
        out_specs=out_specs,
        out_shape=out_shape,
        scratch_shapes=scratch,
        input_output_aliases=aliases,
        compiler_params=_params("arbitrary"),
        name="post_" + mode,
    )(*args)


def _row_tile(ref, r, per_row):
    return ref.at[pl.ds(pl.multiple_of(r * per_row, per_row), per_row)]


def _gather_rows(idx_ref, base, src_hbm, dst, sem, n_groups, per_row):
    def issue(g, c):
        for k in range(GATHER_UNROLL):
            r = g * GATHER_UNROLL + k
            pltpu.make_async_copy(_row_tile(src_hbm, idx_ref[base + r], per_row), _row_tile(dst, r, per_row), sem).start()
        return c

    lax.fori_loop(0, n_groups, issue, 0)


def _wait_rows(src_hbm, dst, sem, n_groups, per_row):
    def drain(g, c):
        for k in range(GATHER_UNROLL):
            r = g * GATHER_UNROLL + k
            pltpu.make_async_copy(_row_tile(src_hbm, 0, per_row), _row_tile(dst, r, per_row), sem).wait()
        return c

    lax.fori_loop(0, n_groups, drain, 0)


def _expert_body(dest_ref, gap_ref, te_ref, tv_ref, nu_ref, h_hbm, wg_ref, wu_ref, wd_ref, o_ref,
                 tok_ref, xbuf, wg16, wu16, wd16, sem, *, n_tok):
    i = pl.program_id(0)
    per_row = wg16.shape[0] // LANES
    tm = o_ref.shape[0] // per_row
    n_used = nu_ref[0]
    slot = i % 2

    def groups(tile):
        return (tv_ref[tile] + GATHER_UNROLL - 1) // GATHER_UNROLL

    @pl.when(i == 0)
    def _():
        def clear(r, c):
            tok_ref[r] = 0
            return c

        def put(tok, c):
            for k in range(TOP_K):
                tok_ref[dest_ref[k * (dest_ref.shape[0] // TOP_K) + tok]] = tok
            return c

        for g in range(gap_ref.shape[0] // 2):
            lax.fori_loop(gap_ref[2 * g], gap_ref[2 * g + 1], clear, 0)
        lax.fori_loop(0, n_tok, put, 0, unroll=GATHER_UNROLL)
        xbuf[...] = jnp.zeros(xbuf.shape, F32)
        _gather_rows(tok_ref, 0, h_hbm, xbuf.at[0], sem.at[0], groups(0), per_row)

    @pl.when(i + 1 < n_used)
    def _():
        _gather_rows(tok_ref, (i + 1) * tm, h_hbm, xbuf.at[1 - slot], sem.at[1 - slot], groups(i + 1), per_row)

    @pl.when(i < n_used)
    def _():
        changed = jnp.logical_or(i == 0, te_ref[i] != te_ref[jnp.maximum(i - 1, 0)])

        @pl.when(changed)
        def _():
            wg16[...] = wg_ref[0].astype(MXU_DTYPE)
            wu16[...] = wu_ref[0].astype(MXU_DTYPE)
            wd16[...] = wd_ref[0].astype(MXU_DTYPE)

        _wait_rows(h_hbm, xbuf.at[slot], sem.at[slot], groups(i), per_row)
        x = _load_row_tiles(xbuf.at[slot], per_row).astype(MXU_DTYPE)
        g = jnp.dot(x, wg16[...], preferred_element_type=F32)
        u = jnp.dot(x, wu16[...], preferred_element_type=F32)
        _store_row_tiles(o_ref, jnp.dot((_silu(g) * u).astype(MXU_DTYPE), wd16[...], preferred_element_type=F32))

    @pl.when(i >= n_used)
    def _():
        o_ref[...] = jnp.zeros(o_ref.shape, F32)


def _experts(dest, gaps, tile_e, tile_valid, n_used, h_all, w_gate, w_up, w_down, layer, n_tok, n_rows):
    tm = TM_EXPERT
    d, de = w_gate.shape[-2:]
    per_row = d // LANES
    wspec_in = pl.BlockSpec((None, 1, d, de), lambda i, ds_, gp, te, tv, nu: (layer, te[i], 0, 0))
    wspec_out = pl.BlockSpec((None, 1, de, d), lambda i, ds_, gp, te, tv, nu: (layer, te[i], 0, 0))
    return pl.pallas_call(
        functools.partial(_expert_body, n_tok=n_tok),
        grid_spec=pltpu.PrefetchScalarGridSpec(
            num_scalar_prefetch=5,
            grid=(n_rows // tm,),
            in_specs=[pl.BlockSpec(memory_space=pl.ANY), wspec_in, wspec_in, wspec_out],
            out_specs=pl.BlockSpec((tm * per_row, LANES), lambda i, ds_, gp, te, tv, nu: (i, 0)),
            scratch_shapes=[pltpu.SMEM((n_rows,), I32), pltpu.VMEM((2, tm * per_row, LANES), F32),
                            pltpu.VMEM((d, de), MXU_DTYPE), pltpu.VMEM((d, de), MXU_DTYPE),
                            pltpu.VMEM((de, d), MXU_DTYPE), pltpu.SemaphoreType.DMA((2,))]),
        out_shape=jax.ShapeDtypeStruct((n_rows * per_row, LANES), F32),
        compiler_params=_params("arbitrary", row_gather=True),
        name="experts",
    )(dest, gaps, tile_e, tile_valid, n_used, h_all, w_gate, w_up, w_down)


def _combine_body(dest_ref, ys_hbm, x_ref, w_ref, g_ref, lng_ref, lnb_ref, o_ref, buf, sem, *, alpha, tok0, k_stride):
    t = pl.program_id(0)
    tm, d = x_ref.shape
    per_row = d // LANES
    slot = t % 2

    def gather(tile, slot_):
        for k in range(TOP_K):
            _gather_rows(dest_ref, k * k_stride + tok0 + tile * tm, ys_hbm, buf.at[slot_, k], sem.at[slot_, k],
                         tm // GATHER_UNROLL, per_row)

    @pl.when(t == 0)
    def _():
        gather(0, 0)

    @pl.when(t + 1 < pl.num_programs(0))
    def _():
        gather(t + 1, 1 - slot)

    for k in range(TOP_K):
        _wait_rows(ys_hbm, buf.at[slot, k], sem.at[slot, k], tm // GATHER_UNROLL, per_row)
    wts = _mxu_round(w_ref[...])
    f = _mxu_round(_load_row_tiles(buf.at[slot, 0], per_row)) * wts[:, 0:1]
    for k in range(1, TOP_K):
        f = f + _mxu_round(_load_row_tiles(buf.at[slot, k], per_row)) * wts[:, k:k + 1]
    o_ref[...] = _layer_norm(alpha * x_ref[...] + g_ref[0] * f, lng_ref[...], lnb_ref[...])


def _combine(dest, ys, x, wts, gate2, ln_g, ln_b, *, tm, rows_per_mod, alpha, tok0):
    n, d = x.shape
    r = gate2.shape[1]
    vec_spec = pl.BlockSpec((1, d), lambda t, ds_: (0, 0))
    return pl.pallas_call(
        functools.partial(_combine_body, alpha=alpha, tok0=tok0, k_stride=dest.shape[0] // TOP_K),
        grid_spec=pltpu.PrefetchScalarGridSpec(
            num_scalar_prefetch=1,
            grid=(n // tm,),
            in_specs=[pl.BlockSpec(memory_space=pl.ANY),
                      pl.BlockSpec((tm, d), lambda t, ds_: (t, 0)),
                      pl.BlockSpec((tm, TOP_K), lambda t, ds_: (t, 0)),
                      pl.BlockSpec((1, r, d), lambda t, ds_: (t // rows_per_mod, 0, 0)),
                      vec_spec, vec_spec],
            out_specs=pl.BlockSpec((tm, d), lambda t, ds_: (t, 0)),
            scratch_shapes=[pltpu.VMEM((2, TOP_K, tm * (d // LANES), LANES), F32),
                            pltpu.SemaphoreType.DMA((2, TOP_K))]),
        out_shape=jax.ShapeDtypeStruct((n, d), F32),
        compiler_params=_params("arbitrary", row_gather=True),
        name="combine",
    )(dest, ys, x, wts, gate2, ln_g, ln_b)


def _top2_of_group(x, sub):
    m1 = jnp.max(x, axis=0, keepdims=True)
    i1 = jnp.min(jnp.where(x == m1, sub, EXPERTS_PER_GROUP), axis=0, keepdims=True)
    rest = jnp.where(sub == i1, -jnp.inf, x)
    m2 = jnp.max(rest, axis=0, keepdims=True)
    i2 = jnp.min(jnp.where(rest == m2, sub, EXPERTS_PER_GROUP), axis=0, keepdims=True)
    return m1 + m2, i1, i2


def _route_body(lg_ref, b_ref, e_ref, w_ref, r_ref, cnt_ref, tri_ref, base_ref, *, n_tok):
    t = pl.program_id(0)
    tt = lg_ref.shape[0]

    @pl.when(t == 0)
    def _():
        earlier = lax.broadcasted_iota(I32, (tt, tt), 0) < lax.broadcasted_iota(I32, (tt, tt), 1)
        tri_ref[...] = jnp.where(earlier, 1.0, 0.0).astype(MXU_DTYPE)
        base_ref[...] = jnp.zeros(base_ref.shape, F32)

    scores = jax.nn.sigmoid(lg_ref[...].T[:N_EXPERTS])
    biased = scores + b_ref[...]
    sub = lax.broadcasted_iota(I32, (EXPERTS_PER_GROUP, tt), 0)
    best, i1, i2 = _top2_of_group(biased[:EXPERTS_PER_GROUP], sub)
    g_sel = jnp.zeros((1, tt), I32)
    for g in range(1, N_GROUPS):
        score_g, i1_g, i2_g = _top2_of_group(biased[g * EXPERTS_PER_GROUP:(g + 1) * EXPERTS_PER_GROUP], sub)
        better = score_g > best
        best = jnp.where(better, score_g, best)
        g_sel = jnp.where(better, g, g_sel)
        i1 = jnp.where(better, i1_g, i1)
        i2 = jnp.where(better, i2_g, i2)
    eid = lax.broadcasted_iota(I32, (N_EXPERTS, tt), 0)
    valid = t * tt + lax.broadcasted_iota(I32, (1, tt), 1) < n_tok
    picks = [g_sel * EXPERTS_PER_GROUP + i1, g_sel * EXPERTS_PER_GROUP + i2]
    hit = [eid == e for e in picks]
    raw = [jnp.sum(jnp.where(h, scores, 0.0), axis=0, keepdims=True) for h in hit]
    denom = raw[0] + raw[1]
    base = base_ref[...]
    for k in range(TOP_K):
        onehot = jnp.where(hit[k] & valid, 1.0, 0.0)
        before = jnp.dot(onehot.astype(MXU_DTYPE), tri_ref[...], preferred_element_type=F32)
        rank = jnp.sum(onehot * (base + before), axis=0, keepdims=True)
        base = base + jnp.sum(onehot, axis=1, keepdims=True)
        e_ref[k:k + 1, :] = picks[k]
        w_ref[k:k + 1, :] = raw[k] / denom
        r_ref[k:k + 1, :] = rank.astype(I32)
    base_ref[...] = base
    cnt_ref[...] = jnp.broadcast_to(base, cnt_ref.shape)


def _route(logits, b_router, n_tok, n_rows):
    tm = TM_EXPERT
    tt = ROUTE_TILE
    n_steps = -(-n_tok // tt)
    n_pad = n_steps * tt
    pick_spec = pl.BlockSpec((TOP_K, tt), lambda t: (0, t))
    e_idx, wts, rank, cnt = pl.pallas_call(
        functools.partial(_route_body, n_tok=n_tok),
        grid=(n_steps,),
        in_specs=[pl.BlockSpec((tt, LANES), lambda t: (t, 0)), pl.BlockSpec((N_EXPERTS, 1), lambda t: (0, 0))],
        out_specs=[pick_spec, pick_spec, pick_spec, pl.BlockSpec((N_EXPERTS, LANES), lambda t: (0, 0))],
        out_shape=[jax.ShapeDtypeStruct((TOP_K, n_pad), I32), jax.ShapeDtypeStruct((TOP_K, n_pad), F32),
                   jax.ShapeDtypeStruct((TOP_K, n_pad), I32), jax.ShapeDtypeStruct((N_EXPERTS, LANES), F32)],
        scratch_shapes=[pltpu.VMEM((tt, tt), MXU_DTYPE), pltpu.VMEM((N_EXPERTS, 1), F32)],
        compiler_params=_params("arbitrary"),
        name="route",
    )(logits, b_router.astype(F32).reshape(N_EXPERTS, 1))
    counts = cnt[:, 0].astype(I32)
    padded = (counts + tm - 1) // tm * tm
    pend = jnp.cumsum(padded)
    pstart = pend - padded
    expert_ids = jnp.arange(N_EXPERTS, dtype=I32)[:, None, None]
    dest = rank + jnp.sum(jnp.where(e_idx[None] == expert_ids, pstart[:, None, None], 0), axis=0)
    gaps = jnp.stack([pstart + counts, pend], axis=1).reshape(-1).astype(I32)
    n_tiles = n_rows // tm
    n_used = (pend[-1] // tm).astype(I32)
    tile_ids = jnp.arange(n_tiles, dtype=I32)
    tile_start = jnp.minimum(tile_ids, n_used - 1) * tm
    tile_e = jnp.minimum(jnp.sum(pend[None, :] <= tile_start[:, None], axis=1), N_EXPERTS - 1).astype(I32)
    tile_valid = jnp.clip((pstart + counts)[tile_e] - tile_ids * tm, 0, tm).astype(I32)
    return gaps, tile_e, tile_valid, n_used.reshape(1), dest.reshape(-1), wts[:, :n_tok].T


def _top_k_indices(x, k):
    iota = lax.broadcasted_iota(I32, x.shape, x.ndim - 1)
    picks = []
    for _ in range(k):
        top = jnp.max(x, axis=-1, keepdims=True)
        idx = jnp.min(jnp.where(x == top, iota, x.shape[-1]), axis=-1, keepdims=True)
        picks.append(idx)
        x = jnp.where(iota == idx, -jnp.inf, x)
    return jnp.concatenate(picks, axis=-1)


def _rotary_tables(pos):
    inv_freq = ROPE_THETA ** (-jnp.arange(ROT_HALF, dtype=F32) / ROT_HALF)
    ang = pos.astype(F32)[:, None] * inv_freq[None, :]
    ones = jnp.ones((pos.shape[0], LANES - ROT_DIM), F32)
    cos_t = jnp.concatenate([jnp.cos(ang), jnp.cos(ang), ones], axis=1)
    sin_t = jnp.concatenate([jnp.sin(ang), jnp.sin(ang), 0.0 * ones], axis=1)
    return cos_t, sin_t


def kernel(x_prompt, x_sample, cache_k, cache_v, state_conv, page_table, c_prompt, c_sample,
           w_ada, b_ada, ln_g, ln_b, w_qkv, w_o, conv_w_in, conv_w_dw, conv_ln_g, conv_ln_b,
           conv_w_out, w_router, b_router, w_gate, w_up, w_down):
    batch, seq, d = x_prompt.shape
    db, dec_seq, _ = x_sample.shape
    assert dec_seq == 1 and seq % TM == 0 and TM % MOBA_BLOCK == 0 and d == N_HEADS * HEAD_DIM
    depth = w_ada.shape[0]
    n_pages = page_table.shape[1]
    past_len = n_pages * PAGE_SIZE
    assert n_pages % KMEAN_PAGES == 0 and n_pages % PAGES_PER_BLOCK == 0
    n_full = n_pages // PAGES_PER_BLOCK
    topk_s = min(MOBA_TOPK, n_full)
    alpha = (2 * depth) ** 0.25
    n_p = batch * seq
    n_all = n_p + db
    assert n_p % db == 0 and n_p % ROUTE_TILE == 0 and ROUTE_TILE % TM == 0
    n_buf = -(-n_all // ROUTE_TILE) * ROUTE_TILE
    tps = seq // TM

    c_rows = -(-(batch + db) // 8) * 8
    c_all = jnp.concatenate([c_prompt, c_sample, jnp.zeros((c_rows - batch - db, d), F32)], axis=0)
    mod = _ada(c_all, w_ada, b_ada)

    wr_pad = jnp.pad(w_router, ((0, 0), (0, LANES - N_EXPERTS))).astype(MXU_DTYPE)
    n_assign = n_all * TOP_K
    n_rows = -(-(n_assign + N_EXPERTS * (TM_EXPERT - 1)) // TM_EXPERT) * TM_EXPERT

    cos_p, sin_p = _rotary_tables(jnp.arange(seq))
    cos_s, sin_s = _rotary_tables(past_len + jnp.zeros((db,), I32))

    xp = x_prompt.reshape(n_p, d)
    xs = x_sample.reshape(db, d)
    kp_pages, vp_pages, ks_rows, vs_rows, conv_p, conv_s = [], [], [], [], [], []
    for i in range(depth):
        mp = [m[:, None, :] for m in jnp.split(mod[i, :batch], 6, axis=-1)]
        ms = [m[None] for m in jnp.split(mod[i, batch:batch + db], 6, axis=-1)]
        if i % 2 == 0:
            ia = i // 2
            q, kp, vp, kx, v16, km = _qkv_prompt(xp, mp[0], mp[1], w_qkv[ia], cos_p, sin_p, batch, seq)
            km = km.transpose(0, 2, 1, 3, 4).reshape(batch, N_KV_HEADS, seq // MOBA_BLOCK, HEAD_DIM)
            attn_p = _moba_prompt(q, kx, v16, km, batch, seq)
            kp_pages.append(kp)
            vp_pages.append(vp)

            qkv_s = _qkv_sample(xs, ms[0][0], ms[1][0], w_qkv[ia], cos_s, sin_s)
            nq = N_HEADS * HEAD_DIM
            nk = N_KV_HEADS * HEAD_DIM
            q_s = qkv_s[:, :nq].reshape(db, N_HEADS, HEAD_DIM)
            k_s = qkv_s[:, nq:nq + nk].reshape(db, N_KV_HEADS, HEAD_DIM)
            v_s = qkv_s[:, nq + nk:].reshape(db, N_KV_HEADS, HEAD_DIM)
            ks_rows.append(k_s[:, :, None, :])
            vs_rows.append(v_s[:, :, None, :])
            if topk_s > 0:
                kmean_s = _kmean_sample(page_table, cache_k, ia)
                kvh = jnp.arange(N_HEADS) // GQA_GROUP
                gate_s = jnp.einsum("bhd,bnhd->bhn", q_s, kmean_s[:, :, kvh])
                sel = _top_k_indices(gate_s, topk_s)
                sel_pages = (sel[..., None] * PAGES_PER_BLOCK + jnp.arange(PAGES_PER_BLOCK)).reshape(db, N_HEADS, -1)
                phys = jnp.take_along_axis(page_table[:, None, :], sel_pages, axis=2).astype(I32)
                attn_s = _attn_sample(phys.reshape(-1), q_s.reshape(db * N_HEADS, 1, HEAD_DIM),
                                      k_s.reshape(db * N_KV_HEADS, 1, HEAD_DIM),
                                      v_s.reshape(db * N_KV_HEADS, 1, HEAD_DIM), cache_k, cache_v, ia)
                attn_s = attn_s.reshape(db, nq)
            else:
                attn_s = jnp.repeat(v_s, GQA_GROUP, axis=1).reshape(db, nq)
            mixer_p, mixer_s, w_mix = attn_p, attn_s, w_o[ia]
            mode_p, mode_s = "attn", "attn"
        else:
            ic = i // 2
            u_p = _glu(xp, mp[0], mp[1], conv_w_in[ic], TM, tps)
            u_s = _glu(xs, ms[0], ms[1], conv_w_in[ic], db, 1)
            cg, cb = conv_ln_g[ic][None], conv_ln_b[ic][None]
            mixer_p = (u_p, conv_w_dw[ic], cg, cb)
            mixer_s = (u_s, state_conv[ic].transpose(1, 0, 2), conv_w_dw[ic], cg, cb)
            w_mix = conv_w_out[ic]
            mode_p, mode_s = "conv", "conv_step"
            conv_p.append(u_p.reshape(batch, seq, d)[:, seq - (CONV_WIDTH - 1):])
            conv_s.append(jnp.concatenate([state_conv[ic][:, 1:], u_s[:, None, :]], axis=1))

        x1p, h2_all, lg_all = _post(mode_p, mixer_p, w_mix, xp, mp[2], ln_g[i, 0][None], ln_b[i, 0][None],
                                    mp[3], mp[4], wr_pad, tm=TM, rows_per_mod=tps, alpha=alpha,
                                    seq=seq, out_rows=n_buf)
        x1s, h2_all, lg_all = _post(mode_s, mixer_s, w_mix, xs, ms[2], ln_g[i, 0][None], ln_b[i, 0][None],
                                    ms[3], ms[4], wr_pad, tm=db, rows_per_mod=1, alpha=alpha,
                                    into=(h2_all, lg_all, n_p // db))
        gaps, tile_e, tile_valid, n_used, dest, wts = _route(lg_all, b_router, n_all, n_rows)
        ys = _experts(dest, gaps, tile_e, tile_valid, n_used, h2_all, w_gate, w_up, w_down, i, n_all, n_rows)
        xp = _combine(dest, ys, x1p, wts[:n_p], mp[5], ln_g[i, 1][None], ln_b[i, 1][None],
                      tm=TM_COMBINE, rows_per_mod=seq // TM_COMBINE, alpha=alpha, tok0=0)
        xs = _combine(dest, ys, x1s, wts[n_p:], ms[5], ln_g[i, 1][None], ln_b[i, 1][None],
                      tm=db, rows_per_mod=1, alpha=alpha, tok0=n_p)

    k_prompt = jnp.concatenate(kp_pages, axis=2)
    v_prompt = jnp.concatenate(vp_pages, axis=2)
    return (xp.reshape(batch, seq, d), xs.reshape(db, 1, d), k_prompt, v_prompt, jnp.stack(conv_p, axis=0),
            jnp.stack(ks_rows, axis=1), jnp.stack(vs_rows, axis=1), jnp.stack(conv_s, axis=0))
```
